```python
import jax, jax.numpy as jnp
from jax import lax
import numpy as np

D_MODEL = 2048
BATCH = 1
SEQ = 8192
DEPTH = 4

HEAD_DIM = 64
NSA_WIDTH = D_MODEL // 2
NSA_HEADS = NSA_WIDTH // HEAD_DIM
NSA_KV_HEADS = NSA_HEADS // 4
NSA_KV_WIDTH = NSA_KV_HEADS * HEAD_DIM
CMP_BLOCK = 32
CMP_STRIDE = 16
CMP_HIDDEN = 4 * HEAD_DIM
SEL_BLOCK = 64
SEL_TOPK = 16
WINDOW = 512
Q_BLOCK = 128
SEL_FORCE = 1.0e4
MLSTM_WIDTH = D_MODEL // 4
MLSTM_HEADS = 4
MLSTM_HEAD_DIM = MLSTM_WIDTH // MLSTM_HEADS
MLSTM_CHUNK = 64
MLSTM_CONV = 4
CONV_CHANNELS = D_MODEL // 4
CONV_WIDTH = 31
MIX_WIDTH = NSA_WIDTH + MLSTM_WIDTH + CONV_CHANNELS
IN_SPLITS = (NSA_WIDTH, NSA_KV_WIDTH, NSA_KV_WIDTH, NSA_KV_WIDTH, NSA_KV_WIDTH, NSA_KV_WIDTH, NSA_KV_WIDTH, 3 * NSA_HEADS, MLSTM_WIDTH, MLSTM_WIDTH, MLSTM_WIDTH, MLSTM_WIDTH, MLSTM_HEADS, MLSTM_HEADS, 2 * CONV_CHANNELS)
IN_WIDTH = sum(IN_SPLITS)
N_GROUPS = 4
EXPERTS_PER_GROUP = 8
N_EXPERTS = N_GROUPS * EXPERTS_PER_GROUP
EXPERT_HIDDEN = D_MODEL // 4
TOP_K_INNER = 2
MOE_BLOCK = 128
NORM_EPS = 1e-6
LN_EPS = 1e-5

kernel_name = 'hybrid_nsa_mlstm_conformer_hmoe'


def rmsnorm(x, g):
    x32 = x.astype(jnp.float32)
    y = x32 * lax.rsqrt(jnp.mean(x32 * x32, axis=-1, keepdims=True) + NORM_EPS)
    return (y * g.astype(jnp.float32)).astype(x.dtype)


def head_rmsnorm(h, n_heads, g):
    B, S, W = h.shape
    h32 = h.astype(jnp.float32).reshape(B, S, n_heads, W // n_heads)
    h32 = h32 * lax.rsqrt(jnp.mean(h32 * h32, axis=-1, keepdims=True) + NORM_EPS)
    return (h32.reshape(B, S, W) * g.astype(jnp.float32)).astype(h.dtype)


def layernorm(x, g, b):
    x32 = x.astype(jnp.float32)
    mu = jnp.mean(x32, axis=-1, keepdims=True)
    var = jnp.mean(jnp.square(x32 - mu), axis=-1, keepdims=True)
    y = (x32 - mu) * lax.rsqrt(var + LN_EPS) * g.astype(jnp.float32) + b.astype(jnp.float32)
    return y.astype(x.dtype)


def causal_depthwise_conv(x, w, b):
    W, C = w.shape
    xp = jnp.pad(x, ((0, 0), (W - 1, 0), (0, 0)))
    y = lax.conv_general_dilated(xp, w.astype(x.dtype)[:, None, :], window_strides=(1,), padding='VALID', dimension_numbers=('NWC', 'WIO', 'NWC'), feature_group_count=C)
    return y + b.astype(x.dtype)


def masked_softmax(s, mask):
    s = jnp.where(mask, s, -jnp.inf)
    m = jnp.max(s, axis=-1, keepdims=True)
    m = jnp.where(jnp.isfinite(m), m, 0.0)
    p = jnp.exp(s - m)
    return p / jnp.maximum(jnp.sum(p, axis=-1, keepdims=True), 1e-30)


def alibi_slopes(n):
    return jnp.asarray(2.0 ** (-8.0 * np.arange(1, n + 1) / n), dtype=jnp.float32)


def selection_overlap(n_cmp, n_blk):
    cs = np.arange(n_cmp) * CMP_STRIDE
    ss = np.arange(n_blk) * SEL_BLOCK
    ov = np.minimum(cs[:, None] + CMP_BLOCK, ss[None, :] + SEL_BLOCK) - np.maximum(cs[:, None], ss[None, :])
    return jnp.asarray(np.clip(ov, 0, None) / CMP_STRIDE, dtype=jnp.float32)


def compress_blocks(x, idx, pos, w1, w2):
    blk = x[:, idx] + pos.astype(jnp.float32)[None, None, :, None, :]
    B, n, l, G, Dh = blk.shape
    blk = blk.transpose(0, 1, 3, 2, 4).reshape(B, n, G, l * Dh)
    return jax.nn.silu(blk @ w1.astype(jnp.float32)) @ w2.astype(jnp.float32)


def nsa_attention(q, k_cmp, v_cmp, k_sel, v_sel, k_win, v_win, gate_pre, cmp_pos_k, cmp_w1_k, cmp_w2_k, cmp_pos_v, cmp_w1_v, cmp_w2_v):
    f32 = jnp.float32
    B, S, _ = q.shape
    H, G, Dh = NSA_HEADS, NSA_KV_HEADS, HEAD_DIM
    HPG = H // G
    q = q.astype(f32).reshape(B, S, G, HPG, Dh) * (Dh ** -0.5)
    k_cmp, v_cmp, k_sel, v_sel, k_win, v_win = [t.astype(f32).reshape(B, S, G, Dh) for t in (k_cmp, v_cmp, k_sel, v_sel, k_win, v_win)]
    gates = jax.nn.sigmoid(gate_pre.astype(f32)).reshape(B, S, G, HPG, 3)
    slopes = alibi_slopes(H).reshape(G, HPG)[None, None, :, :, None]

    n_cmp = (S - CMP_BLOCK) // CMP_STRIDE + 1
    cmp_idx = np.arange(n_cmp)[:, None] * CMP_STRIDE + np.arange(CMP_BLOCK)[None, :]
    kc = compress_blocks(k_cmp, cmp_idx, cmp_pos_k, cmp_w1_k, cmp_w2_k)
    vc = compress_blocks(v_cmp, cmp_idx, cmp_pos_v, cmp_w1_v, cmp_w2_v)
    cmp_end = jnp.asarray(cmp_idx[:, -1], dtype=jnp.int32)

    n_blk = S // SEL_BLOCK
    n_sel = min(SEL_TOPK, n_blk)
    overlap = selection_overlap(n_cmp, n_blk)
    ksb = k_sel.reshape(B, n_blk, SEL_BLOCK, G, Dh).transpose(0, 3, 1, 2, 4)
    vsb = v_sel.reshape(B, n_blk, SEL_BLOCK, G, Dh).transpose(0, 3, 1, 2, 4)
    blk_ids = jnp.arange(n_blk)
    b_ix = jnp.arange(B)[:, None, None, None]
    g_ix = jnp.arange(G)[None, None, :, None]

    kwp = jnp.pad(k_win, ((0, 0), (WINDOW, 0), (0, 0), (0, 0)))
    vwp = jnp.pad(v_win, ((0, 0), (WINDOW, 0), (0, 0), (0, 0)))
    win_off = jnp.arange(Q_BLOCK + WINDOW)

    n_qb = S // Q_BLOCK
    q_blocks = q.reshape(B, n_qb, Q_BLOCK, G, HPG, Dh).swapaxes(0, 1)
    g_blocks = gates.reshape(B, n_qb, Q_BLOCK, G, HPG, 3).swapaxes(0, 1)

    def block_fn(args):
        qi, qb, gb = args
        t = qi * Q_BLOCK + jnp.arange(Q_BLOCK)
        d_c = t[:, None] - cmp_end[None, :]
        s_c = jnp.einsum('bqghd,bngd->bqghn', qb, kc) - slopes * d_c.astype(f32)[None, :, None, None, :]
        p_c = masked_softmax(s_c, (d_c >= 0)[None, :, None, None, :])
        o_c = jnp.einsum('bqghn,bngd->bqghd', p_c, vc)
        imp = jnp.einsum('bqgn,nj->bqgj', jnp.sum(p_c, axis=3), overlap)
        cur = t // SEL_BLOCK
        forced = (blk_ids[None, :] == 0) | (blk_ids[None, :] == cur[:, None]) | (blk_ids[None, :] == cur[:, None] - 1)
        causal_b = blk_ids[None, :] <= cur[:, None]
        imp = jnp.where(forced[None, :, None, :], SEL_FORCE, jnp.where(causal_b[None, :, None, :], imp, -SEL_FORCE))
        _, sel = lax.top_k(imp, n_sel)
        ks = ksb[b_ix, g_ix, sel].reshape(B, Q_BLOCK, G, n_sel * SEL_BLOCK, Dh)
        vs = vsb[b_ix, g_ix, sel].reshape(B, Q_BLOCK, G, n_sel * SEL_BLOCK, Dh)
        pos = (sel[..., None] * SEL_BLOCK + jnp.arange(SEL_BLOCK)).reshape(B, Q_BLOCK, G, n_sel * SEL_BLOCK)
        d_s = t[None, :, None, None] - pos
        s_s = jnp.einsum('bqghd,bqgkd->bqghk', qb, ks) - slopes * d_s.astype(f32)[:, :, :, None, :]
        p_s = masked_softmax(s_s, (d_s >= 0)[:, :, :, None, :])
        o_s = jnp.einsum('bqghk,bqgkd->bqghd', p_s, vs)
        kw = lax.dynamic_slice_in_dim(kwp, qi * Q_BLOCK, Q_BLOCK + WINDOW, axis=1)
        vw = lax.dynamic_slice_in_dim(vwp, qi * Q_BLOCK, Q_BLOCK + WINDOW, axis=1)
        kpos = qi * Q_BLOCK - WINDOW + win_off
        d_w = t[:, None] - kpos[None, :]
        m_w = (d_w >= 0) & (d_w < WINDOW) & (kpos[None, :] >= 0)
        s_w = jnp.einsum('bqghd,bkgd->bqghk', qb, kw) - slopes * d_w.astype(f32)[None, :, None, None, :]
        p_w = masked_softmax(s_w, m_w[None, :, None, None, :])
        o_w = jnp.einsum('bqghk,bkgd->bqghd', p_w, vw)
        return gb[..., 0:1] * o_c + gb[..., 1:2] * o_s + gb[..., 2:3] * o_w

    out = lax.map(block_fn, (jnp.arange(n_qb), q_blocks, g_blocks))
    return out.swapaxes(0, 1).reshape(B, S, H * Dh)


def mlstm_chunkwise(q, k, v, i_pre, f_pre):
    f32 = jnp.float32
    B, S, H, Dh = q.shape
    L = MLSTM_CHUNK
    NC = S // L

    def chunks(t):
        t = t.astype(f32).reshape((B, NC, L, H) + t.shape[3:])
        return jnp.moveaxis(t, 3, 1)

    qc, kc, vc = chunks(q), chunks(k) * (Dh ** -0.5), chunks(v)
    ig = chunks(i_pre)
    b = jnp.cumsum(jax.nn.log_sigmoid(chunks(f_pre)), axis=-1)
    b_end = b[..., -1]
    a = b_end[..., None] - b + ig
    a_max = jnp.max(a, axis=-1)
    w = jnp.exp(a - a_max[..., None])
    C_chunk = jnp.einsum('bhcsk,bhcsv->bhckv', kc * w[..., None], vc)
    n_chunk = jnp.einsum('bhcs,bhcsk->bhck', w, kc)

    def step(carry, inp):
        C, n, m = carry
        be, am, Cc, nc = inp
        m_new = jnp.maximum(be + m, am)
        sp = jnp.exp(be + m - m_new)
        sc = jnp.exp(am - m_new)
        return (sp[..., None, None] * C + sc[..., None, None] * Cc, sp[..., None] * n + sc[..., None] * nc, m_new), (C, n, m)

    init = (jnp.zeros((B, H, Dh, Dh), f32), jnp.zeros((B, H, Dh), f32), jnp.zeros((B, H), f32))
    xs = (jnp.moveaxis(b_end, 2, 0), jnp.moveaxis(a_max, 2, 0), jnp.moveaxis(C_chunk, 2, 0), jnp.moveaxis(n_chunk, 2, 0))
    _, (C0, n0, m0) = lax.scan(step, init, xs)
    C0 = jnp.moveaxis(C0, 0, 2)
    n0 = jnp.moveaxis(n0, 0, 2)
    m0 = jnp.moveaxis(m0, 0, 2)

    causal = jnp.tril(jnp.ones((L, L), dtype=bool))
    D = jnp.where(causal, b[..., :, None] - b[..., None, :] + ig[..., None, :], -jnp.inf)
    inter = b + m0[..., None]
    m_t = jnp.maximum(inter, jnp.max(D, axis=-1))
    s = jnp.einsum('bhctd,bhcsd->bhcts', qc, kc) * jnp.exp(D - m_t[..., None])
    w_inter = jnp.exp(inter - m_t)
    num = jnp.einsum('bhcts,bhcsd->bhctd', s, vc) + w_inter[..., None] * jnp.einsum('bhctk,bhckv->bhctv', qc, C0)
    den = jnp.sum(s, axis=-1) + w_inter * jnp.einsum('bhctk,bhck->bhct', qc, n0)
    h = num / jnp.maximum(jnp.abs(den), jnp.exp(-m_t))[..., None]
    return jnp.moveaxis(h, 1, 3).reshape(B, S, H, Dh)


def hybrid_mixer(h, w_in, cmp_pos_k, cmp_w1_k, cmp_w2_k, cmp_pos_v, cmp_w1_v, cmp_w2_v, nsa_out_g, mlstm_conv_w, mlstm_conv_b, mlstm_i_bias, mlstm_f_bias, mlstm_out_g, conv_w, conv_b, conv_ln_g, conv_ln_b, w_out):
    B, S, _ = h.shape
    points = [int(p) for p in np.cumsum(IN_SPLITS)[:-1]]
    (q, k_c, v_c, k_s, v_s, k_w, v_w, g_nsa, mq, mk, mv, mo, mi, mf, cu) = jnp.split(h @ w_in, points, axis=-1)
    a_out = nsa_attention(q, k_c, v_c, k_s, v_s, k_w, v_w, g_nsa, cmp_pos_k, cmp_w1_k, cmp_w2_k, cmp_pos_v, cmp_w1_v, cmp_w2_v)
    a_out = head_rmsnorm(a_out.astype(h.dtype), NSA_HEADS, nsa_out_g)
    qk = jax.nn.silu(causal_depthwise_conv(jnp.concatenate([mq, mk], axis=-1), mlstm_conv_w, mlstm_conv_b))
    mq, mk = jnp.split(qk, 2, axis=-1)
    shp = (B, S, MLSTM_HEADS, MLSTM_HEAD_DIM)
    hm = mlstm_chunkwise(mq.reshape(shp), mk.reshape(shp), mv.reshape(shp), mi + mlstm_i_bias, mf + mlstm_f_bias)
    hm = (jax.nn.sigmoid(mo.astype(jnp.float32)) * hm.reshape(B, S, MLSTM_WIDTH)).astype(h.dtype)
    hm = head_rmsnorm(hm, MLSTM_HEADS, mlstm_out_g)
    ca, cg = jnp.split(cu, 2, axis=-1)
    c = causal_depthwise_conv(ca * jax.nn.sigmoid(cg), conv_w, conv_b)
    c = jax.nn.silu(layernorm(c, conv_ln_g, conv_ln_b))
    return jnp.concatenate([a_out, hm, c], axis=-1) @ w_out


def hier_moe(xn, router_w_group, router_b_group, router_w_expert, router_b_expert, w_gate, w_up, w_down):
    f32 = jnp.float32
    B, S, D = xn.shape
    T = B * S
    x2 = xn.reshape(T, D)
    pg = jax.nn.softmax((x2 @ router_w_group).astype(f32) + router_b_group.astype(f32), axis=-1)
    pg_top, grp = lax.top_k(pg, 1)
    le = (x2 @ router_w_expert).astype(f32).reshape(T, N_GROUPS, EXPERTS_PER_GROUP) + router_b_expert.astype(f32)
    le = jnp.take_along_axis(le, jnp.broadcast_to(grp[:, :, None], (T, 1, EXPERTS_PER_GROUP)), axis=1)[:, 0]
    pe_top, e_local = lax.top_k(jax.nn.softmax(le, axis=-1), TOP_K_INNER)
    gate = pg_top * pe_top / jnp.sum(pe_top, axis=-1, keepdims=True)
    expert = grp * EXPERTS_PER_GROUP + e_local
    flat_e = expert.reshape(-1)
    flat_w = gate.reshape(-1)
    flat_tok = jnp.repeat(jnp.arange(T, dtype=jnp.int32), TOP_K_INNER)
    A = flat_e.shape[0]
    order = jnp.argsort(flat_e)
    e_sorted = flat_e[order]
    counts = jnp.bincount(flat_e, length=N_EXPERTS)
    starts = jnp.cumsum(counts) - counts
    padded = (counts + MOE_BLOCK - 1) // MOE_BLOCK * MOE_BLOCK
    pends = jnp.cumsum(padded)
    dest = (pends - padded)[e_sorted] + jnp.arange(A) - starts[e_sorted]
    n_blocks = (A + MOE_BLOCK - 1) // MOE_BLOCK + N_EXPERTS
    P = n_blocks * MOE_BLOCK
    row_tok = jnp.zeros((P,), jnp.int32).at[dest].set(flat_tok[order])
    row_w = jnp.zeros((P,), f32).at[dest].set(flat_w[order])
    blk_e = jnp.minimum(jnp.searchsorted(pends, jnp.arange(n_blocks) * MOE_BLOCK, side='right'), N_EXPERTS - 1)

    def block_fn(args):
        tok, wt, e = args
        xb = x2[tok]
        hb = jax.nn.silu(xb @ w_gate[e]) * (xb @ w_up[e])
        return (hb @ w_down[e]) * wt[:, None].astype(xb.dtype)

    yb = lax.map(block_fn, (row_tok.reshape(n_blocks, MOE_BLOCK), row_w.reshape(n_blocks, MOE_BLOCK), blk_e))
    y = jnp.zeros((T, D), x2.dtype).at[row_tok].add(yb.reshape(P, D))
    return y.reshape(B, S, D)


def setup_inputs(seed: int = 0) -> dict:
    key = jax.random.key(seed)
    ks = jax.random.split(key, 40)
    f32 = jnp.float32
    L, D = DEPTH, D_MODEL

    def nrm(k, shape, scale):
        return jax.random.normal(k, shape, f32) * scale

    def gain(k, shape):
        return 1.0 + 0.02 * jax.random.normal(k, shape, f32)

    return {
        'x': nrm(ks[0], (BATCH, SEQ, D), 1.0),
        'attn_norm_g': gain(ks[1], (L, D)),
        'w_in': nrm(ks[2], (L, D, IN_WIDTH), D ** -0.5),
        'cmp_pos_k': nrm(ks[3], (L, CMP_BLOCK, HEAD_DIM), 0.1),
        'cmp_w1_k': nrm(ks[4], (L, CMP_BLOCK * HEAD_DIM, CMP_HIDDEN), (CMP_BLOCK * HEAD_DIM) ** -0.5),
        'cmp_w2_k': nrm(ks[5], (L, CMP_HIDDEN, HEAD_DIM), CMP_HIDDEN ** -0.5),
        'cmp_pos_v': nrm(ks[6], (L, CMP_BLOCK, HEAD_DIM), 0.1),
        'cmp_w1_v': nrm(ks[7], (L, CMP_BLOCK * HEAD_DIM, CMP_HIDDEN), (CMP_BLOCK * HEAD_DIM) ** -0.5),
        'cmp_w2_v': nrm(ks[8], (L, CMP_HIDDEN, HEAD_DIM), CMP_HIDDEN ** -0.5),
        'nsa_out_g': gain(ks[9], (L, NSA_WIDTH)),
        'mlstm_conv_w': nrm(ks[10], (L, MLSTM_CONV, 2 * MLSTM_WIDTH), MLSTM_CONV ** -0.5),
        'mlstm_conv_b': nrm(ks[11], (L, 2 * MLSTM_WIDTH), 0.01),
        'mlstm_i_bias': nrm(ks[12], (L, MLSTM_HEADS), 0.1),
        'mlstm_f_bias': 3.0 + 3.0 * jax.random.uniform(ks[13], (L, MLSTM_HEADS), f32),
        'mlstm_out_g': gain(ks[14], (L, MLSTM_WIDTH)),
        'conv_w': nrm(ks[15], (L, CONV_WIDTH, CONV_CHANNELS), CONV_WIDTH ** -0.5),
        'conv_b': nrm(ks[16], (L, CONV_CHANNELS), 0.01),
        'conv_ln_g': gain(ks[17], (L, CONV_CHANNELS)),
        'conv_ln_b': nrm(ks[18], (L, CONV_CHANNELS), 0.01),
        'w_out': nrm(ks[19], (L, MIX_WIDTH, D), MIX_WIDTH ** -0.5),
        'ffn_norm_g': gain(ks[20], (L, D)),
        'router_w_group': nrm(ks[21], (L, D, N_GROUPS), D ** -0.5),
        'router_b_group': nrm(ks[22], (L, N_GROUPS), 0.01),
        'router_w_expert': nrm(ks[23], (L, D, N_EXPERTS), D ** -0.5),
        'router_b_expert': nrm(ks[24], (L, N_GROUPS, EXPERTS_PER_GROUP), 0.01),
        'expert_w_gate': nrm(ks[25], (L, N_EXPERTS, D, EXPERT_HIDDEN), D ** -0.5),
        'expert_w_up': nrm(ks[26], (L, N_EXPERTS, D, EXPERT_HIDDEN), D ** -0.5),
        'expert_w_down': nrm(ks[27], (L, N_EXPERTS, EXPERT_HIDDEN, D), EXPERT_HIDDEN ** -0.5),
        'final_norm_g': gain(ks[28], (D,)),
    }


def reference(x, attn_norm_g, w_in, cmp_pos_k, cmp_w1_k, cmp_w2_k, cmp_pos_v, cmp_w1_v, cmp_w2_v, nsa_out_g, mlstm_conv_w, mlstm_conv_b, mlstm_i_bias, mlstm_f_bias, mlstm_out_g, conv_w, conv_b, conv_ln_g, conv_ln_b, w_out, ffn_norm_g, router_w_group, router_b_group, router_w_expert, router_b_expert, expert_w_gate, expert_w_up, expert_w_down, final_norm_g):
    for l in range(DEPTH):
        h = rmsnorm(x, attn_norm_g[l])
        x = x + hybrid_mixer(h, w_in[l], cmp_pos_k[l], cmp_w1_k[l], cmp_w2_k[l], cmp_pos_v[l], cmp_w1_v[l], cmp_w2_v[l], nsa_out_g[l], mlstm_conv_w[l], mlstm_conv_b[l], mlstm_i_bias[l], mlstm_f_bias[l], mlstm_out_g[l], conv_w[l], conv_b[l], conv_ln_g[l], conv_ln_b[l], w_out[l])
        h = rmsnorm(x, ffn_norm_g[l])
        x = x + hier_moe(h, router_w_group[l], router_b_group[l], router_w_expert[l], router_b_expert[l], expert_w_gate[l], expert_w_up[l], expert_w_down[l])
    return rmsnorm(x, final_norm_g)
```

```python
import functools

import numpy as np
import jax
import jax.numpy as jnp
from jax import lax
from jax.experimental import pallas as pl
from jax.experimental.pallas import tpu as pltpu

F32 = jnp.float32
BF16 = jnp.bfloat16

D_MODEL = 2048
HEAD_DIM = 64
NSA_WIDTH = D_MODEL // 2
NSA_HEADS = NSA_WIDTH // HEAD_DIM
NSA_KV_HEADS = NSA_HEADS // 4
HPG = NSA_HEADS // NSA_KV_HEADS
NSA_KV_WIDTH = NSA_KV_HEADS * HEAD_DIM
CMP_BLOCK = 32
CMP_STRIDE = 16
CMP_HIDDEN = 4 * HEAD_DIM
SEL_BLOCK = 64
SEL_TOPK = 16
WINDOW = 512
Q_BLOCK = 128
SEL_FORCE = 1.0e4
MLSTM_WIDTH = D_MODEL // 4
MLSTM_HEADS = 4
MLSTM_HEAD_DIM = MLSTM_WIDTH // MLSTM_HEADS
MLSTM_CONV = 4
CONV_CHANNELS = D_MODEL // 4
CONV_WIDTH = 31
N_GROUPS = 4
EXPERTS_PER_GROUP = 8
N_EXPERTS = N_GROUPS * EXPERTS_PER_GROUP
EXPERT_HIDDEN = D_MODEL // 4
NORM_EPS = 1e-6
LN_EPS = 1e-5

NEG = -1.0e30
QL = HPG * Q_BLOCK
MLSTM_CHUNK = 256
MOE_ROWS = 256
VMEM_LIMIT = 52 * 1024 * 1024

RM_QK, RM_CU, RM_V, RM_O, RM_KC, RM_VC, RM_SMALL = 0, 1024, 2048, 2560, 3072, 3328, 3584
RM_WIDTH = 3840
SM_GATES, SM_I, SM_F = 0, 48, 52
FT_Q, FT_KS, FT_VS, FT_KW, FT_VW = 0, 1024, 1280, 1536, 1792
FT_WIDTH = 2048


def _cparams(sem, vmem=VMEM_LIMIT):
    return pltpu.CompilerParams(dimension_semantics=sem, vmem_limit_bytes=vmem)


def _sigmoid(x):
    return 1.0 / (1.0 + jnp.exp(-x))


def _rms_rows(x, g):
    ms = jnp.mean(x * x, axis=-1, keepdims=True)
    return x * lax.rsqrt(ms + NORM_EPS) * g


def _norm_mm_kernel(x_ref, g_ref, w_ref, o_ref, h_ref):
    @pl.when(pl.program_id(1) == 0)
    def _():
        h_ref[...] = _rms_rows(x_ref[...], g_ref[...]).astype(BF16)

    o_ref[...] = jnp.dot(h_ref[...], w_ref[...], preferred_element_type=F32).astype(o_ref.dtype)


def norm_matmul(x, g, w, tm, tn):
    M, K = x.shape
    N = w.shape[1]
    return pl.pallas_call(
        _norm_mm_kernel,
        out_shape=jax.ShapeDtypeStruct((M, N), F32),
        grid=(M // tm, N // tn),
        in_specs=[pl.BlockSpec((tm, K), lambda i, j: (i, 0)),
                  pl.BlockSpec((1, K), lambda i, j: (0, 0)),
                  pl.BlockSpec((K, tn), lambda i, j: (0, j))],
        out_specs=pl.BlockSpec((tm, tn), lambda i, j: (i, j)),
        scratch_shapes=[pltpu.VMEM((tm, K), BF16)],
        compiler_params=_cparams(("parallel", "arbitrary")),
        name="norm_matmul",
    )(x, g, w)


def _norm_mm_t_kernel(x_ref, g_ref, wt_ref, o_ref, h_ref):
    @pl.when(pl.program_id(1) == 0)
    def _():
        h_ref[...] = _rms_rows(x_ref[...], g_ref[...]).astype(BF16)

    o_ref[...] = lax.dot_general(wt_ref[...], h_ref[...], (((1,), (1,)), ((), ())),
                                 preferred_element_type=F32).astype(o_ref.dtype)


def norm_matmul_t(x, g, wt, tm, tn):
    M, K = x.shape
    N = wt.shape[0]
    return pl.pallas_call(
        _norm_mm_t_kernel,
        out_shape=jax.ShapeDtypeStruct((N, M), BF16),
        grid=(M // tm, N // tn),
        in_specs=[pl.BlockSpec((tm, K), lambda i, j: (i, 0)),
                  pl.BlockSpec((1, K), lambda i, j: (0, 0)),
                  pl.BlockSpec((tn, K), lambda i, j: (j, 0))],
        out_specs=pl.BlockSpec((tn, tm), lambda i, j: (j, i)),
        scratch_shapes=[pltpu.VMEM((tm, K), BF16)],
        compiler_params=_cparams(("parallel", "arbitrary")),
        name="norm_matmul_t",
    )(x, g, wt)


def _out_mm_kernel(a_ref, m_ref, c_ref, x_ref, w_ref, o_ref):
    h = jnp.concatenate([a_ref[...], m_ref[...], c_ref[...]], axis=-1).astype(BF16)
    o_ref[...] = x_ref[...] + jnp.dot(h, w_ref[...], preferred_element_type=F32)


def out_proj(a, m, c, x, w, tm, tn):
    M = x.shape[0]
    N = w.shape[1]
    return pl.pallas_call(
        _out_mm_kernel,
        out_shape=jax.ShapeDtypeStruct((M, N), F32),
        grid=(N // tn, M // tm),
        in_specs=[pl.BlockSpec((tm, a.shape[1]), lambda j, i: (i, 0)),
                  pl.BlockSpec((tm, m.shape[1]), lambda j, i: (i, 0)),
                  pl.BlockSpec((tm, c.shape[1]), lambda j, i: (i, 0)),
                  pl.BlockSpec((tm, tn), lambda j, i: (i, j)),
                  pl.BlockSpec((w.shape[0], tn), lambda j, i: (0, j))],
        out_specs=pl.BlockSpec((tm, tn), lambda j, i: (i, j)),
        compiler_params=_cparams(("parallel", "parallel")),
        name="out_proj",
    )(a, m, c, x, w)


def _compress_kernel(blk_ref, pos_ref, w1_ref, w2_ref, o_ref):
    x = (blk_ref[0, 0] + pos_ref[0]).astype(BF16)
    hid = jnp.dot(x, w1_ref[0], preferred_element_type=F32)
    hid = hid * _sigmoid(hid)
    o_ref[0, 0] = jnp.dot(hid.astype(BF16), w2_ref[0], preferred_element_type=F32)


def compress(blk, pos, w1, w2):
    _, G, NCP, LD = blk.shape
    return pl.pallas_call(
        _compress_kernel,
        out_shape=jax.ShapeDtypeStruct((2, G, NCP, HEAD_DIM), F32),
        grid=(2, G),
        in_specs=[pl.BlockSpec((1, 1, NCP, LD), lambda a, g: (a, g, 0, 0)),
                  pl.BlockSpec((1, 1, LD), lambda a, g: (a, 0, 0)),
                  pl.BlockSpec((1, LD, CMP_HIDDEN), lambda a, g: (a, 0, 0)),
                  pl.BlockSpec((1, CMP_HIDDEN, HEAD_DIM), lambda a, g: (a, 0, 0))],
        out_specs=pl.BlockSpec((1, 1, NCP, HEAD_DIM), lambda a, g: (a, g, 0, 0)),
        compiler_params=_cparams(("parallel", "parallel")),
        name="nsa_compress",
    )(blk, pos, w1, w2)


def _load_qt(q_ref):
    q = q_ref[...]
    qt = jnp.concatenate([q[h * HEAD_DIM:(h + 1) * HEAD_DIM, :] for h in range(HPG)], axis=1)
    return qt * (HEAD_DIM ** -0.5)


def _cmp_attn_kernel(q_ref, kc_ref, vc_ref, sl_ref, ov_ref, oc_ref, sel_ref, *, n_sel):
    qb = pl.program_id(1)
    qt = _load_qt(q_ref)
    s = jnp.dot(kc_ref[0].astype(BF16), qt, preferred_element_type=F32)
    ncp = s.shape[0]
    n_io = lax.broadcasted_iota(jnp.int32, (ncp, QL), 0)
    l_io = lax.broadcasted_iota(jnp.int32, (ncp, QL), 1)
    t = qb * Q_BLOCK + (l_io & (Q_BLOCK - 1))
    d = t - (n_io * CMP_STRIDE + (CMP_BLOCK - 1))
    s = jnp.where(d >= 0, s - sl_ref[0] * d.astype(F32), -jnp.inf)
    m = jnp.max(s, axis=0, keepdims=True)
    m = jnp.where(m > -jnp.inf, m, 0.0)
    p = jnp.exp(s - m)
    p = p / jnp.maximum(jnp.sum(p, axis=0, keepdims=True), 1e-30)
    oc_ref[0, 0] = lax.dot_general(vc_ref[0].astype(BF16), p.astype(BF16), (((0,), (0,)), ((), ())),
                                   preferred_element_type=F32)

    ps = p[:, 0:Q_BLOCK]
    for h in range(1, HPG):
        ps = ps + p[:, h * Q_BLOCK:(h + 1) * Q_BLOCK]
    hi = ps.astype(BF16)
    r1 = ps - hi.astype(F32)
    mid = r1.astype(BF16)
    lo = (r1 - mid.astype(F32)).astype(BF16)
    ov = ov_ref[...]
    imp = (jnp.dot(ov, hi, preferred_element_type=F32) + jnp.dot(ov, mid, preferred_element_type=F32)
           + jnp.dot(ov, lo, preferred_element_type=F32))

    n_blk = imp.shape[0]
    j_io = lax.broadcasted_iota(jnp.int32, (n_blk, Q_BLOCK), 0)
    tq = qb * Q_BLOCK + lax.broadcasted_iota(jnp.int32, (n_blk, Q_BLOCK), 1)
    cur = tq // SEL_BLOCK
    forced = (j_io == 0) | (j_io == cur) | (j_io == cur - 1)
    v = jnp.where(forced, SEL_FORCE, jnp.where(j_io <= cur, imp, -SEL_FORCE))
    sel = jnp.zeros((n_blk, Q_BLOCK), F32)
    for _ in range(n_sel):
        mx = jnp.max(v, axis=0, keepdims=True)
        idx = jnp.min(jnp.where(v == mx, j_io, n_blk), axis=0, keepdims=True)
        pick = j_io == idx
        v = jnp.where(pick, -jnp.inf, v)
        sel = jnp.where(pick, 1.0, sel)
    sel_ref[0, 0] = jnp.where(sel > 0.5, 0.0, NEG)


def cmp_attention(feat_t, kc, vc, slopes, ov_t, S):
    G = NSA_KV_HEADS
    n_qb = S // Q_BLOCK
    n_blk = S // SEL_BLOCK
    ncp = kc.shape[1]
    n_sel = min(SEL_TOPK, n_blk)
    return pl.pallas_call(
        functools.partial(_cmp_attn_kernel, n_sel=n_sel),
        out_shape=(jax.ShapeDtypeStruct((G, n_qb, HEAD_DIM, QL), F32),
                   jax.ShapeDtypeStruct((G, n_qb, n_blk, Q_BLOCK), F32)),
        grid=(G, n_qb),
        in_specs=[pl.BlockSpec((HPG * HEAD_DIM, Q_BLOCK), lambda g, i: (g, i)),
                  pl.BlockSpec((1, ncp, HEAD_DIM), lambda g, i: (g, 0, 0)),
                  pl.BlockSpec((1, ncp, HEAD_DIM), lambda g, i: (g, 0, 0)),
                  pl.BlockSpec((1, 1, QL), lambda g, i: (g, 0, 0)),
                  pl.BlockSpec((n_blk, ncp), lambda g, i: (0, 0))],
        out_specs=(pl.BlockSpec((1, 1, HEAD_DIM, QL), lambda g, i: (g, i, 0, 0)),
                   pl.BlockSpec((1, 1, n_blk, Q_BLOCK), lambda g, i: (g, i, 0, 0))),
        compiler_params=_cparams(("parallel", "parallel")),
        name="nsa_cmp_select",
    )(feat_t, kc, vc, slopes, ov_t)


def _online_update(carry, s, vt):
    m, l, acc = carry
    m_new = jnp.maximum(m, jnp.max(s, axis=0, keepdims=True))
    alpha = jnp.exp(m - m_new)
    p = jnp.exp(s - m_new)
    l = alpha * l + jnp.sum(p, axis=0, keepdims=True)
    acc = alpha * acc + jnp.dot(vt, p.astype(BF16), preferred_element_type=F32)
    return m_new, l, acc


def _sel_win_kernel(q_ref, ks_ref, vs_ref, kw_ref, vw_ref, selb_ref, oc_ref, sl_ref, gate_ref, gain_ref,
                    o_ref, *, tk):
    qb = pl.program_id(1)
    qt = _load_qt(q_ref)
    slope = sl_ref[0]
    t_row = qb * Q_BLOCK + (lax.broadcasted_iota(jnp.int32, (1, QL), 1) & (Q_BLOCK - 1))
    init = (jnp.full((1, QL), NEG, F32), jnp.zeros((1, QL), F32), jnp.zeros((HEAD_DIM, QL), F32))
    bpt = tk // SEL_BLOCK

    def scores(k_ref, k0, n):
        kt = k_ref[:, pl.ds(k0, n)]
        s = lax.dot_general(kt, qt, (((0,), (0,)), ((), ())), preferred_element_type=F32)
        pos = k0 + lax.broadcasted_iota(jnp.int32, (n, QL), 0)
        d = t_row - pos
        return s - slope * d.astype(F32), d

    def sel_body(kt_i, carry):
        k0 = pl.multiple_of(kt_i * tk, tk)
        s, d = scores(ks_ref, k0, tk)
        rows = [jnp.broadcast_to(selb_ref[0, 0, pl.ds(kt_i * bpt + j, 1), :], (SEL_BLOCK, Q_BLOCK))
                for j in range(bpt)]
        sb = jnp.concatenate(rows, axis=0)
        sb = jnp.concatenate([sb] * HPG, axis=1)
        s = jnp.where(d >= 0, s + sb, NEG)
        return _online_update(carry, s, vs_ref[:, pl.ds(k0, tk)])

    n_sel_tiles = (qb * Q_BLOCK + Q_BLOCK - 1) // tk + 1
    m_s, l_s, acc_s = lax.fori_loop(0, n_sel_tiles, sel_body, init)

    def win_body(i, carry):
        k0 = pl.multiple_of((qb - i) * Q_BLOCK, Q_BLOCK)
        s, d = scores(kw_ref, k0, Q_BLOCK)
        s = jnp.where((d >= 0) & (d < WINDOW), s, NEG)
        return _online_update(carry, s, vw_ref[:, pl.ds(k0, Q_BLOCK)])

    n_win_tiles = jnp.minimum(qb, WINDOW // Q_BLOCK) + 1
    m_w, l_w, acc_w = lax.fori_loop(0, n_win_tiles, win_body, init)

    gts = _sigmoid(gate_ref[0, 0])
    o = gts[0:1] * oc_ref[0, 0] + gts[1:2] * (acc_s / l_s) + gts[2:3] * (acc_w / l_w)
    ms = jnp.mean(o * o, axis=0, keepdims=True)
    y = o * lax.rsqrt(ms + NORM_EPS) * gain_ref[0]
    yt = jnp.concatenate([y[:, h * Q_BLOCK:(h + 1) * Q_BLOCK] for h in range(HPG)], axis=0)
    o_ref[...] = yt.T


def sel_win_attention(feat_t, selb, oc_t, slopes, gates_t, gain_t, S):
    G = NSA_KV_HEADS
    n_qb = S // Q_BLOCK
    n_blk = S // SEL_BLOCK
    tk = min(512, S)
    kv_spec = lambda base: pl.BlockSpec((HEAD_DIM, S), lambda g, i: (base // HEAD_DIM + g, 0))
    return pl.pallas_call(
        functools.partial(_sel_win_kernel, tk=tk),
        out_shape=jax.ShapeDtypeStruct((S, NSA_WIDTH), F32),
        grid=(G, n_qb),
        in_specs=[pl.BlockSpec((HPG * HEAD_DIM, Q_BLOCK), lambda g, i: (g, i)),
                  kv_spec(FT_KS), kv_spec(FT_VS), kv_spec(FT_KW), kv_spec(FT_VW),
                  pl.BlockSpec((1, 1, n_blk, Q_BLOCK), lambda g, i: (g, i, 0, 0)),
                  pl.BlockSpec((1, 1, HEAD_DIM, QL), lambda g, i: (g, i, 0, 0)),
                  pl.BlockSpec((1, 1, QL), lambda g, i: (g, 0, 0)),
                  pl.BlockSpec((1, 1, 3, QL), lambda g, i: (g, i, 0, 0)),
                  pl.BlockSpec((1, HEAD_DIM, QL), lambda g, i: (g, 0, 0))],
        out_specs=pl.BlockSpec((Q_BLOCK, HPG * HEAD_DIM), lambda g, i: (i, g)),
        compiler_params=_cparams(("parallel", "parallel")),
        name="nsa_sel_win",
    )(feat_t, feat_t, feat_t, feat_t, feat_t, selb, oc_t, slopes, gates_t, gain_t)


def _conv_kernel(*refs, width, glu, post):
    if post == "ln_silu":
        x_ref, halo_ref, w_ref, b_ref, lg_ref, lb_ref, o_ref, u_ref = refs
    else:
        x_ref, halo_ref, w_ref, b_ref, o_ref, u_ref = refs
    i = pl.program_id(0)
    T = o_ref.shape[0]
    H = halo_ref.shape[0]
    C = o_ref.shape[1]

    def pre(v):
        return v[:, :C] * _sigmoid(v[:, C:]) if glu else v

    u_ref[0:H, :] = jnp.where(i > 0, pre(halo_ref[...]), 0.0)
    u_ref[H:H + T, :] = pre(x_ref[...])
    acc = jnp.broadcast_to(b_ref[...], (T, C))
    base = H - (width - 1)
    for k in range(width):
        acc = acc + w_ref[k:k + 1, :] * u_ref[base + k:base + k + T, :]
    if post == "ln_silu":
        mu = jnp.mean(acc, axis=-1, keepdims=True)
        xc = acc - mu
        var = jnp.mean(xc * xc, axis=-1, keepdims=True)
        acc = xc * lax.rsqrt(var + LN_EPS) * lg_ref[...] + lb_ref[...]
    o_ref[...] = acc * _sigmoid(acc)


def causal_conv(x, col_block, w, b, ln=None, *, glu, T):
    S = x.shape[0]
    width, C = w.shape
    cin = 2 * C if glu else C
    H = -(-(width - 1) // 8) * 8
    post = "ln_silu" if ln is not None else "silu"
    in_specs = [pl.BlockSpec((T, cin), lambda i: (i, col_block)),
                pl.BlockSpec((H, cin), lambda i: (jnp.maximum(i * (T // H) - 1, 0), col_block)),
                pl.BlockSpec((width, C), lambda i: (0, 0)),
                pl.BlockSpec((1, C), lambda i: (0, 0))]
    args = [x, x, w, b]
    if ln is not None:
        in_specs += [pl.BlockSpec((1, C), lambda i: (0, 0))] * 2
        args += list(ln)
    return pl.pallas_call(
        functools.partial(_conv_kernel, width=width, glu=glu, post=post),
        out_shape=jax.ShapeDtypeStruct((S, C), F32),
        grid=(S // T,),
        in_specs=in_specs,
        out_specs=pl.BlockSpec((T, C), lambda i: (i, 0)),
        scratch_shapes=[pltpu.VMEM((H + T, C), F32)],
        compiler_params=_cparams(("parallel",)),
        name="causal_conv_glu" if glu else "causal_conv",
    )(*args)


def _log_sigmoid(x):
    return jnp.minimum(x, 0.0) - jnp.log(1.0 + jnp.exp(-jnp.abs(x)))


def _mlstm_kernel(qk_ref, v_ref, o_ref, sm_ref, gt_ref, bcol_ref, brow_ref, gain_ref, out_ref, c_ref, m_ref):
    L = qk_ref.shape[0]
    DH = MLSTM_HEAD_DIM
    W = MLSTM_WIDTH

    @pl.when(pl.program_id(0) == 0)
    def _():
        c_ref[...] = jnp.zeros_like(c_ref)
        m_ref[...] = jnp.zeros_like(m_ref)

    small = sm_ref[...]
    lane = lax.broadcasted_iota(jnp.int32, small.shape, 1)
    t_io = lax.broadcasted_iota(jnp.int32, (L, L), 0)
    s_io = lax.broadcasted_iota(jnp.int32, (L, L), 1)
    causal = s_io <= t_io
    ones_col = (lax.broadcasted_iota(jnp.int32, (L, DH), 1) == 0).astype(BF16)

    def column(c):
        return jnp.sum(jnp.where(lane == c, small, 0.0), axis=1, keepdims=True)

    for h in range(MLSTM_HEADS):
        q = qk_ref[:, h * DH:(h + 1) * DH].astype(BF16)
        k = qk_ref[:, W + h * DH:W + (h + 1) * DH] * (DH ** -0.5)
        v = v_ref[:, h * DH:(h + 1) * DH].astype(BF16)
        vaug = jnp.concatenate([v, ones_col], axis=1)
        i_col = column(SM_I + h) + bcol_ref[h:h + 1, :]
        f_col = column(SM_F + h) + bcol_ref[MLSTM_HEADS + h:MLSTM_HEADS + h + 1, :]
        i_row = gt_ref[h:h + 1, :] + brow_ref[h:h + 1, :]
        f_row = gt_ref[MLSTM_HEADS + h:MLSTM_HEADS + h + 1, :] + brow_ref[MLSTM_HEADS + h:MLSTM_HEADS + h + 1, :]
        lf_col = _log_sigmoid(f_col)
        lf_row = _log_sigmoid(f_row)
        b_col = jnp.sum(jnp.where(causal, lf_row, 0.0), axis=1, keepdims=True)
        b_row = jnp.sum(jnp.where(t_io <= s_io, lf_col, 0.0), axis=0, keepdims=True)
        b_end = jnp.sum(lf_row, axis=1, keepdims=True)
        m0 = m_ref[h:h + 1, 0:1]
        c0 = c_ref[h]

        dmat = jnp.where(causal, b_col - b_row + i_row, -jnp.inf)
        inter = b_col + m0
        m_t = jnp.maximum(inter, jnp.max(dmat, axis=1, keepdims=True))
        qk = lax.dot_general(q, k.astype(BF16), (((1,), (1,)), ((), ())), preferred_element_type=F32)
        smat = qk * jnp.exp(dmat - m_t)
        w_inter = jnp.exp(inter - m_t)
        r = (jnp.dot(smat.astype(BF16), vaug, preferred_element_type=F32)
             + w_inter * jnp.dot(q, c0.astype(BF16), preferred_element_type=F32))
        num = r[:, :DH]
        den = jnp.sum(r[:, DH:], axis=1, keepdims=True)
        hh = num / jnp.maximum(jnp.abs(den), jnp.exp(-m_t))

        a_row = b_end - b_row + i_row
        a_col = b_end - b_col + i_col
        m_new = jnp.maximum(b_end + m0, jnp.max(a_row, axis=1, keepdims=True))
        kw = (k * jnp.exp(a_col - m_new)).astype(BF16)
        c_ref[h] = jnp.exp(b_end + m0 - m_new) * c0 + lax.dot_general(
            kw, vaug, (((0,), (0,)), ((), ())), preferred_element_type=F32)
        m_ref[h:h + 1, :] = jnp.broadcast_to(m_new, (1, m_ref.shape[1]))

        y = _sigmoid(o_ref[:, h * DH:(h + 1) * DH]) * hh
        ms = jnp.mean(y * y, axis=-1, keepdims=True)
        out_ref[:, h * DH:(h + 1) * DH] = y * lax.rsqrt(ms + NORM_EPS) * gain_ref[:, h * DH:(h + 1) * DH]


def mlstm(qk, rm, gates_t, bias_col, bias_row, gain, S):
    L = min(MLSTM_CHUNK, S)
    W = MLSTM_WIDTH
    return pl.pallas_call(
        _mlstm_kernel,
        out_shape=jax.ShapeDtypeStruct((S, W), F32),
        grid=(S // L,),
        in_specs=[pl.BlockSpec((L, 2 * W), lambda c: (c, 0)),
                  pl.BlockSpec((L, W), lambda c: (c, RM_V // W)),
                  pl.BlockSpec((L, W), lambda c: (c, RM_O // W)),
                  pl.BlockSpec((L, 128), lambda c: (c, RM_SMALL // 128)),
                  pl.BlockSpec((2 * MLSTM_HEADS, L), lambda c: (0, c)),
                  pl.BlockSpec((2 * MLSTM_HEADS, 1), lambda c: (0, 0)),
                  pl.BlockSpec((2 * MLSTM_HEADS, 1), lambda c: (0, 0)),
                  pl.BlockSpec((1, W), lambda c: (0, 0))],
        out_specs=pl.BlockSpec((L, W), lambda c: (c, 0)),
        scratch_shapes=[pltpu.VMEM((MLSTM_HEADS, MLSTM_HEAD_DIM, 2 * MLSTM_HEAD_DIM), F32),
                        pltpu.VMEM((8, 128), F32)],
        compiler_params=_cparams(("arbitrary",)),
        name="mlstm",
    )(qk, rm, rm, rm, gates_t, bias_col, bias_row, gain)


def _router_kernel(x_ref, g_ref, wr_ref, br_ref, xn_ref, eid_ref, gate_ref):
    xn = _rms_rows(x_ref[...], g_ref[...])
    xn_ref[...] = xn.astype(xn_ref.dtype)
    logits = lax.dot_general(wr_ref[...], xn, (((1,), (1,)), ((), ())), precision=lax.Precision.HIGHEST,
                             preferred_element_type=F32) + br_ref[...]
    tm = logits.shape[1]
    lg = logits[N_EXPERTS:N_EXPERTS + N_GROUPS, :]
    eg = jnp.exp(lg - jnp.max(lg, axis=0, keepdims=True))
    pg = eg / jnp.sum(eg, axis=0, keepdims=True)
    pg_top = jnp.max(pg, axis=0, keepdims=True)
    g_io = lax.broadcasted_iota(jnp.int32, (N_GROUPS, tm), 0)
    grp = jnp.min(jnp.where(pg == pg_top, g_io, N_GROUPS), axis=0, keepdims=True)
    le = logits[0:EXPERTS_PER_GROUP, :]
    for g in range(1, N_GROUPS):
        le = jnp.where(grp == g, logits[g * EXPERTS_PER_GROUP:(g + 1) * EXPERTS_PER_GROUP, :], le)
    ee = jnp.exp(le - jnp.max(le, axis=0, keepdims=True))
    pe = ee / jnp.sum(ee, axis=0, keepdims=True)
    e_io = lax.broadcasted_iota(jnp.int32, (EXPERTS_PER_GROUP, tm), 0)
    p1 = jnp.max(pe, axis=0, keepdims=True)
    i1 = jnp.min(jnp.where(pe == p1, e_io, EXPERTS_PER_GROUP), axis=0, keepdims=True)
    pe2 = jnp.where(e_io == i1, -1.0, pe)
    p2 = jnp.max(pe2, axis=0, keepdims=True)
    i2 = jnp.min(jnp.where(pe2 == p2, e_io, EXPERTS_PER_GROUP), axis=0, keepdims=True)
    tot = p1 + p2
    eid_ref[...] = jnp.concatenate([grp * EXPERTS_PER_GROUP + i1, grp * EXPERTS_PER_GROUP + i2], axis=0)
    gate_ref[...] = jnp.concatenate([pg_top * p1 / tot, pg_top * p2 / tot], axis=0)


def router(x, g, wr_t, br_t, tm):
    S, D = x.shape
    R = wr_t.shape[0]
    return pl.pallas_call(
        _router_kernel,
        out_shape=(jax.ShapeDtypeStruct((S, D), BF16),
                   jax.ShapeDtypeStruct((2, S), jnp.int32),
                   jax.ShapeDtypeStruct((2, S), F32)),
        grid=(S // tm,),
        in_specs=[pl.BlockSpec((tm, D), lambda i: (i, 0)),
                  pl.BlockSpec((1, D), lambda i: (0, 0)),
                  pl.BlockSpec((R, D), lambda i: (0, 0)),
                  pl.BlockSpec((R, 1), lambda i: (0, 0))],
        out_specs=(pl.BlockSpec((tm, D), lambda i: (i, 0)),
                   pl.BlockSpec((2, tm), lambda i: (0, i)),
                   pl.BlockSpec((2, tm), lambda i: (0, i))),
        compiler_params=_cparams(("parallel",)),
        name="moe_router",
    )(x, g, wr_t, br_t)


def _expert_kernel(be_ref, nu_ref, xs_ref, w_ref, wg_ref, wu_ref, wd_ref, o_ref, wg_s, wu_s, wd_s):
    b = pl.program_id(0)
    e = be_ref[b]
    prev = be_ref[jnp.maximum(b - 1, 0)]

    @pl.when((b == 0) | (e != prev))
    def _():
        wg_s[...] = wg_ref[0].astype(BF16)
        wu_s[...] = wu_ref[0].astype(BF16)
        wd_s[...] = wd_ref[0].astype(BF16)

    @pl.when(b < nu_ref[0])
    def _():
        x = xs_ref[...]
        hg = jnp.dot(x, wg_s[...], preferred_element_type=F32)
        hu = jnp.dot(x, wu_s[...], preferred_element_type=F32)
        hb = (hg * _sigmoid(hg) * hu).astype(BF16)
        o_ref[...] = jnp.dot(hb, wd_s[...], preferred_element_type=F32) * w_ref[...]

    @pl.when(b >= nu_ref[0])
    def _():
        o_ref[...] = jnp.zeros_like(o_ref)


def expert_ffn(blk_e, n_used, xs, row_w, w_gate, w_up, w_down):
    P, D = xs.shape
    n_blocks = P // MOE_ROWS
    Hd = w_gate.shape[2]
    grid_spec = pltpu.PrefetchScalarGridSpec(
        num_scalar_prefetch=2,
        grid=(n_blocks,),
        in_specs=[pl.BlockSpec((MOE_ROWS, D), lambda b, be, nu: (b, 0)),
                  pl.BlockSpec((MOE_ROWS, 1), lambda b, be, nu: (b, 0)),
                  pl.BlockSpec((1, D, Hd), lambda b, be, nu: (be[b], 0, 0)),
                  pl.BlockSpec((1, D, Hd), lambda b, be, nu: (be[b], 0, 0)),
                  pl.BlockSpec((1, Hd, D), lambda b, be, nu: (be[b], 0, 0))],
        out_specs=pl.BlockSpec((MOE_ROWS, D), lambda b, be, nu: (b, 0)),
        scratch_shapes=[pltpu.VMEM((D, Hd), BF16), pltpu.VMEM((D, Hd), BF16), pltpu.VMEM((Hd, D), BF16)],
    )
    return pl.pallas_call(
        _expert_kernel,
        out_shape=jax.ShapeDtypeStruct((P, D), F32),
        grid_spec=grid_spec,
        compiler_params=_cparams(("arbitrary",)),
        name="moe_experts",
    )(blk_e, n_used, xs, row_w, w_gate, w_up, w_down)


def _final_norm_kernel(x_ref, g_ref, o_ref):
    o_ref[...] = _rms_rows(x_ref[...], g_ref[...])


def final_norm(x, g, tm):
    S, D = x.shape
    return pl.pallas_call(
        _final_norm_kernel,
        out_shape=jax.ShapeDtypeStruct((S, D), F32),
        grid=(S // tm,),
        in_specs=[pl.BlockSpec((tm, D), lambda i: (i, 0)), pl.BlockSpec((1, D), lambda i: (0, 0))],
        out_specs=pl.BlockSpec((tm, D), lambda i: (i, 0)),
        compiler_params=_cparams(("parallel",)),
        name="final_norm",
    )(x, g)


def _alibi_lanes():
    sl = 2.0 ** (-8.0 * np.arange(1, NSA_HEADS + 1) / NSA_HEADS)
    sl = np.repeat(sl.reshape(NSA_KV_HEADS, HPG, 1), Q_BLOCK, axis=2).reshape(NSA_KV_HEADS, 1, QL)
    return jnp.asarray(sl, F32)


def _overlap_t(n_cmp_pad, n_cmp, n_blk):
    cs = np.arange(n_cmp_pad) * CMP_STRIDE
    ss = np.arange(n_blk) * SEL_BLOCK
    ov = np.minimum(cs[None, :] + CMP_BLOCK, ss[:, None] + SEL_BLOCK) - np.maximum(cs[None, :], ss[:, None])
    ov = np.clip(ov, 0, None) / CMP_STRIDE
    ov[:, n_cmp:] = 0.0
    return jnp.asarray(ov, BF16)


def mixer(x, p, S):
    G = NSA_KV_HEADS
    tm = min(1024, S)
    w_in = p["w_in"]
    c = np.cumsum((0, NSA_WIDTH) + (NSA_KV_WIDTH,) * 6 + (3 * NSA_HEADS,) + (MLSTM_WIDTH,) * 4
                  + (MLSTM_HEADS,) * 2 + (2 * CONV_CHANNELS,))
    (q0, kc0, vc0, ks0, vs0, kw0, vw0, gt0, mq0, mk0, mv0, mo0, mi0, mf0, cu0, end) = [int(v) for v in c]
    w_rm = jnp.concatenate(
        [w_in[:, mq0:mv0], w_in[:, cu0:end], w_in[:, mv0:mi0], w_in[:, kc0:ks0], w_in[:, gt0:mq0],
         w_in[:, mi0:cu0], jnp.zeros((D_MODEL, RM_WIDTH - RM_SMALL - 56), F32)], axis=1).astype(BF16)
    w_ft = jnp.concatenate([w_in[:, q0:kc0], w_in[:, ks0:gt0]], axis=1).T.astype(BF16)
    g = p["attn_norm_g"][None, :]
    rm = norm_matmul(x, g, w_rm, tm, 768)
    feat_t = norm_matmul_t(x, g, w_ft, tm, 512)

    ncp = S // CMP_STRIDE
    n_cmp = (S - CMP_BLOCK) // CMP_STRIDE + 1
    n_blk = S // SEL_BLOCK
    n_qb = S // Q_BLOCK
    kv = jnp.stack([rm[:, RM_KC:RM_KC + NSA_KV_WIDTH], rm[:, RM_VC:RM_VC + NSA_KV_WIDTH]])
    rows = kv.reshape(2, ncp, CMP_STRIDE, G, HEAD_DIM).transpose(0, 3, 1, 2, 4).reshape(2, G, ncp, -1)
    nxt = jnp.concatenate([rows[:, :, 1:], jnp.zeros_like(rows[:, :, :1])], axis=2)
    blk = jnp.concatenate([rows, nxt], axis=-1)
    pos = jnp.stack([p["cmp_pos_k"].reshape(1, -1), p["cmp_pos_v"].reshape(1, -1)])
    w1 = jnp.stack([p["cmp_w1_k"], p["cmp_w1_v"]]).astype(BF16)
    w2 = jnp.stack([p["cmp_w2_k"], p["cmp_w2_v"]]).astype(BF16)
    kvc = compress(blk, pos, w1, w2)
    slopes = _alibi_lanes()
    oc_t, selb = cmp_attention(feat_t, kvc[0], kvc[1], slopes, _overlap_t(ncp, n_cmp, n_blk), S)
    small = rm[:, RM_SMALL:RM_SMALL + 128]
    gates_t = small[:, SM_GATES:SM_GATES + 3 * NSA_HEADS].reshape(n_qb, Q_BLOCK, G, HPG, 3)
    gates_t = gates_t.transpose(2, 0, 4, 3, 1).reshape(G, n_qb, 3, QL)
    gain_t = jnp.broadcast_to(p["nsa_out_g"].reshape(G, HPG, HEAD_DIM, 1).transpose(0, 2, 1, 3),
                              (G, HEAD_DIM, HPG, Q_BLOCK)).reshape(G, HEAD_DIM, QL)
    a_out = sel_win_attention(feat_t, selb, oc_t, slopes, gates_t, gain_t, S)

    tc = min(512, S)
    qk = causal_conv(rm, RM_QK // (2 * MLSTM_WIDTH), p["mlstm_conv_w"], p["mlstm_conv_b"][None, :],
                     glu=False, T=tc)
    gates_m = small[:, SM_I:SM_I + 2 * MLSTM_HEADS].T
    bias = jnp.concatenate([p["mlstm_i_bias"], p["mlstm_f_bias"]])
    hm = mlstm(qk, rm, gates_m, bias[:, None], bias[:, None], p["mlstm_out_g"][None, :], S)

    cv = causal_conv(rm, RM_CU // (2 * CONV_CHANNELS), p["conv_w"], p["conv_b"][None, :],
                     (p["conv_ln_g"][None, :], p["conv_ln_b"][None, :]), glu=True, T=tc)

    return out_proj(a_out, hm, cv, x, p["w_out"].astype(BF16), min(512, S), 1024)


def moe(x, p, S):
    T = S
    tm = min(512, S)
    wr_t = jnp.concatenate([p["router_w_expert"].T, p["router_w_group"].T,
                            jnp.zeros((12, D_MODEL), F32)], axis=0)
    br_t = jnp.concatenate([p["router_b_expert"].reshape(-1), p["router_b_group"], jnp.zeros((12,), F32)])[:, None]
    xn, eid, gate = router(x, p["ffn_norm_g"][None, :], wr_t, br_t, tm)

    flat_e = eid.T.reshape(-1)
    flat_w = gate.T.reshape(-1)
    A = flat_e.shape[0]
    onehot = (flat_e[:, None] == jnp.arange(N_EXPERTS)[None, :]).astype(jnp.int32)
    rank = jnp.take_along_axis(jnp.cumsum(onehot, axis=0), flat_e[:, None], axis=1)[:, 0] - 1
    counts = jnp.sum(onehot, axis=0)
    padded = (counts + MOE_ROWS - 1) // MOE_ROWS * MOE_ROWS
    pends = jnp.cumsum(padded)
    dest = (pends - padded)[flat_e] + rank
    n_blocks = A // MOE_ROWS + N_EXPERTS
    P = n_blocks * MOE_ROWS
    row_tok = jnp.zeros((P,), jnp.int32).at[dest].set(jnp.arange(A, dtype=jnp.int32) // 2)
    row_w = jnp.zeros((P,), F32).at[dest].set(flat_w)
    n_used = (pends[-1] // MOE_ROWS).astype(jnp.int32)
    blk_start = jnp.minimum(jnp.arange(n_blocks), n_used - 1) * MOE_ROWS
    blk_e = jnp.minimum(jnp.searchsorted(pends, blk_start, side="right"), N_EXPERTS - 1).astype(jnp.int32)

    xs = xn[row_tok]
    yb = expert_ffn(blk_e, n_used[None], xs, row_w[:, None], p["expert_w_gate"], p["expert_w_up"],
                    p["expert_w_down"])
    pos2 = dest.reshape(T, 2)
    return x + yb[pos2[:, 0]] + yb[pos2[:, 1]]


_LAYER_KEYS = ("attn_norm_g", "w_in", "cmp_pos_k", "cmp_w1_k", "cmp_w2_k", "cmp_pos_v", "cmp_w1_v", "cmp_w2_v",
               "nsa_out_g", "mlstm_conv_w", "mlstm_conv_b", "mlstm_i_bias", "mlstm_f_bias", "mlstm_out_g",
               "conv_w", "conv_b", "conv_ln_g", "conv_ln_b", "w_out", "ffn_norm_g", "router_w_group",
               "router_b_group", "router_w_expert", "router_b_expert", "expert_w_gate", "expert_w_up",
               "expert_w_down")


def kernel(x, attn_norm_g, w_in, cmp_pos_k, cmp_w1_k, cmp_w2_k, cmp_pos_v, cmp_w1_v, cmp_w2_v, nsa_out_g, mlstm_conv_w, mlstm_conv_b, mlstm_i_bias, mlstm_f_bias, mlstm_out_g, conv_w, conv_b, conv_ln_g, conv_ln_b, w_out, ffn_norm_g, router_w_group, router_b_group, router_w_expert, router_b_expert, expert_w_gate, expert_w_up, expert_w_down, final_norm_g):
    stacked = (attn_norm_g, w_in, cmp_pos_k, cmp_w1_k, cmp_w2_k, cmp_pos_v, cmp_w1_v, cmp_w2_v, nsa_out_g,
               mlstm_conv_w, mlstm_conv_b, mlstm_i_bias, mlstm_f_bias, mlstm_out_g, conv_w, conv_b, conv_ln_g,
               conv_ln_b, w_out, ffn_norm_g, router_w_group, router_b_group, router_w_expert, router_b_expert,
               expert_w_gate, expert_w_up, expert_w_down)
    B, S, D = x.shape
    assert B == 1 and D == D_MODEL and S % 1024 == 0
    h = x.reshape(S, D)
    for l in range(attn_norm_g.shape[0]):
        p = {k: v[l] for k, v in zip(_LAYER_KEYS, stacked)}
        h = mixer(h, p, S)
        h = moe(h, p, S)
    return final_norm(h, final_norm_g[None, :], min(512, S)).reshape(B, S, D)
```

```python
import functools

import numpy as np
import jax
import jax.numpy as jnp
from jax import lax
from jax.experimental import pallas as pl
from jax.experimental.pallas import tpu as pltpu

F32 = jnp.float32
BF16 = jnp.bfloat16

D_MODEL = 2048
HEAD_DIM = 64
NSA_WIDTH = D_MODEL // 2
NSA_HEADS = NSA_WIDTH // HEAD_DIM
NSA_KV_HEADS = NSA_HEADS // 4
HPG = NSA_HEADS // NSA_KV_HEADS
NSA_KV_WIDTH = NSA_KV_HEADS * HEAD_DIM
CMP_BLOCK = 32
CMP_STRIDE = 16
CMP_HIDDEN = 4 * HEAD_DIM
SEL_BLOCK = 64
SEL_TOPK = 16
WINDOW = 512
Q_BLOCK = 128
SEL_FORCE = 1.0e4
MLSTM_WIDTH = D_MODEL // 4
MLSTM_HEADS = 4
MLSTM_HEAD_DIM = MLSTM_WIDTH // MLSTM_HEADS
MLSTM_CONV = 4
CONV_CHANNELS = D_MODEL // 4
CONV_WIDTH = 31
N_GROUPS = 4
EXPERTS_PER_GROUP = 8
N_EXPERTS = N_GROUPS * EXPERTS_PER_GROUP
EXPERT_HIDDEN = D_MODEL // 4
NORM_EPS = 1e-6
LN_EPS = 1e-5

NEG = -1.0e30
QL = HPG * Q_BLOCK
MLSTM_CHUNK = 256
MOE_ROWS = 256
VMEM_LIMIT = 52 * 1024 * 1024

RM_QK, RM_CU, RM_V, RM_O, RM_KC, RM_VC, RM_SMALL = 0, 1024, 2048, 2560, 3072, 3328, 3584
RM_WIDTH = 3840
SM_GATES, SM_I, SM_F = 0, 48, 52
FT_Q, FT_KS, FT_VS, FT_KW, FT_VW = 0, 1024, 1280, 1536, 1792
FT_WIDTH = 2048


def _cparams(sem, vmem=VMEM_LIMIT):
    return pltpu.CompilerParams(dimension_semantics=sem, vmem_limit_bytes=vmem)


def _sigmoid(x):
    return 1.0 / (1.0 + jnp.exp(-x))


def _rms_rows(x, g):
    ms = jnp.mean(x * x, axis=-1, keepdims=True)
    return x * lax.rsqrt(ms + NORM_EPS) * g


def _norm_mm_kernel(x_ref, g_ref, w_ref, o_ref, h_ref):
    @pl.when(pl.program_id(1) == 0)
    def _():
        h_ref[...] = _rms_rows(x_ref[...], g_ref[...]).astype(BF16)

    o_ref[...] = jnp.dot(h_ref[...], w_ref[...], preferred_element_type=F32).astype(o_ref.dtype)


def norm_matmul(x, g, w, tm, tn):
    M, K = x.shape
    N = w.shape[1]
    return pl.pallas_call(
        _norm_mm_kernel,
        out_shape=jax.ShapeDtypeStruct((M, N), F32),
        grid=(M // tm, N // tn),
        in_specs=[pl.BlockSpec((tm, K), lambda i, j: (i, 0)),
                  pl.BlockSpec((1, K), lambda i, j: (0, 0)),
                  pl.BlockSpec((K, tn), lambda i, j: (0, j))],
        out_specs=pl.BlockSpec((tm, tn), lambda i, j: (i, j)),
        scratch_shapes=[pltpu.VMEM((tm, K), BF16)],
        compiler_params=_cparams(("parallel", "arbitrary")),
        name="norm_matmul",
    )(x, g, w)


def _norm_mm_t_kernel(x_ref, g_ref, wt_ref, sc_ref, o_ref, h_ref):
    @pl.when(pl.program_id(1) == 0)
    def _():
        h_ref[...] = _rms_rows(x_ref[...], g_ref[...]).astype(BF16)

    o = lax.dot_general(wt_ref[...], h_ref[...], (((1,), (1,)), ((), ())), preferred_element_type=F32)
    o_ref[...] = (o * sc_ref[...]).astype(o_ref.dtype)


def norm_matmul_t(x, g, wt, row_scale, tm, tn):
    M, K = x.shape
    N = wt.shape[0]
    return pl.pallas_call(
        _norm_mm_t_kernel,
        out_shape=jax.ShapeDtypeStruct((N, M), BF16),
        grid=(M // tm, N // tn),
        in_specs=[pl.BlockSpec((tm, K), lambda i, j: (i, 0)),
                  pl.BlockSpec((1, K), lambda i, j: (0, 0)),
                  pl.BlockSpec((tn, K), lambda i, j: (j, 0)),
                  pl.BlockSpec((tn, 1), lambda i, j: (j, 0))],
        out_specs=pl.BlockSpec((tn, tm), lambda i, j: (j, i)),
        scratch_shapes=[pltpu.VMEM((tm, K), BF16)],
        compiler_params=_cparams(("parallel", "arbitrary")),
        name="norm_matmul_t",
    )(x, g, wt, row_scale)


def _out_mm_kernel(a_ref, m_ref, c_ref, x_ref, w_ref, o_ref):
    h = jnp.concatenate([a_ref[...], m_ref[...], c_ref[...]], axis=-1).astype(BF16)
    o_ref[...] = x_ref[...] + jnp.dot(h, w_ref[...], preferred_element_type=F32)


def out_proj(a, m, c, x, w, tm, tn):
    M = x.shape[0]
    N = w.shape[1]
    return pl.pallas_call(
        _out_mm_kernel,
        out_shape=jax.ShapeDtypeStruct((M, N), F32),
        grid=(N // tn, M // tm),
        in_specs=[pl.BlockSpec((tm, a.shape[1]), lambda j, i: (i, 0)),
                  pl.BlockSpec((tm, m.shape[1]), lambda j, i: (i, 0)),
                  pl.BlockSpec((tm, c.shape[1]), lambda j, i: (i, 0)),
                  pl.BlockSpec((tm, tn), lambda j, i: (i, j)),
                  pl.BlockSpec((w.shape[0], tn), lambda j, i: (0, j))],
        out_specs=pl.BlockSpec((tm, tn), lambda j, i: (i, j)),
        compiler_params=_cparams(("parallel", "parallel")),
        name="out_proj",
    )(a, m, c, x, w)


def _compress_kernel(blk_ref, pos_ref, w1_ref, w2_ref, o_ref):
    x = (blk_ref[0, 0] + pos_ref[0]).astype(BF16)
    hid = jnp.dot(x, w1_ref[0], preferred_element_type=F32)
    hid = hid * _sigmoid(hid)
    o_ref[0, 0] = jnp.dot(hid.astype(BF16), w2_ref[0], preferred_element_type=F32)


def compress(blk, pos, w1, w2):
    _, G, NCP, LD = blk.shape
    return pl.pallas_call(
        _compress_kernel,
        out_shape=jax.ShapeDtypeStruct((2, G, NCP, HEAD_DIM), F32),
        grid=(2, G),
        in_specs=[pl.BlockSpec((1, 1, NCP, LD), lambda a, g: (a, g, 0, 0)),
                  pl.BlockSpec((1, 1, LD), lambda a, g: (a, 0, 0)),
                  pl.BlockSpec((1, LD, CMP_HIDDEN), lambda a, g: (a, 0, 0)),
                  pl.BlockSpec((1, CMP_HIDDEN, HEAD_DIM), lambda a, g: (a, 0, 0))],
        out_specs=pl.BlockSpec((1, 1, NCP, HEAD_DIM), lambda a, g: (a, g, 0, 0)),
        compiler_params=_cparams(("parallel", "parallel")),
        name="nsa_compress",
    )(blk, pos, w1, w2)


def _load_qt(q_ref):
    q = q_ref[...]
    return jnp.concatenate([q[h * HEAD_DIM:(h + 1) * HEAD_DIM, :] for h in range(HPG)], axis=1)


def _cmp_attn_kernel(q_ref, kc_ref, vc_ref, sl_ref, ov_ref, oc_ref, sel_ref, cnt_ref, *, n_sel):
    qb = pl.program_id(1)
    qt = _load_qt(q_ref)
    s = jnp.dot(kc_ref[0].astype(BF16), qt, preferred_element_type=F32)
    ncp = s.shape[0]
    n_io = lax.broadcasted_iota(jnp.int32, (ncp, QL), 0)
    l_io = lax.broadcasted_iota(jnp.int32, (ncp, QL), 1)
    t = qb * Q_BLOCK + (l_io & (Q_BLOCK - 1))
    d = t - (n_io * CMP_STRIDE + (CMP_BLOCK - 1))
    s = jnp.where(d >= 0, s - sl_ref[0] * d.astype(F32), -jnp.inf)
    m = jnp.max(s, axis=0, keepdims=True)
    m = jnp.where(m > -jnp.inf, m, 0.0)
    p = jnp.exp2(s - m)
    p = p / jnp.maximum(jnp.sum(p, axis=0, keepdims=True), 1e-30)
    oc_ref[0, 0] = lax.dot_general(vc_ref[0].astype(BF16), p.astype(BF16), (((0,), (0,)), ((), ())),
                                   preferred_element_type=F32)

    ps = p[:, 0:Q_BLOCK]
    for h in range(1, HPG):
        ps = ps + p[:, h * Q_BLOCK:(h + 1) * Q_BLOCK]
    hi = ps.astype(BF16)
    r1 = ps - hi.astype(F32)
    mid = r1.astype(BF16)
    lo = (r1 - mid.astype(F32)).astype(BF16)
    ov = ov_ref[...]
    imp = (jnp.dot(ov, hi, preferred_element_type=F32) + jnp.dot(ov, mid, preferred_element_type=F32)
           + jnp.dot(ov, lo, preferred_element_type=F32))

    n_blk = imp.shape[0]
    j_io = lax.broadcasted_iota(jnp.int32, (n_blk, Q_BLOCK), 0)
    tq = qb * Q_BLOCK + lax.broadcasted_iota(jnp.int32, (n_blk, Q_BLOCK), 1)
    cur = tq // SEL_BLOCK
    forced = (j_io == 0) | (j_io == cur) | (j_io == cur - 1)
    v = jnp.where(forced, SEL_FORCE, jnp.where(j_io <= cur, imp, -SEL_FORCE))
    sel = jnp.zeros((n_blk, Q_BLOCK), F32)
    for _ in range(n_sel):
        mx = jnp.max(v, axis=0, keepdims=True)
        idx = jnp.min(jnp.where(v == mx, j_io, n_blk), axis=0, keepdims=True)
        pick = j_io == idx
        v = jnp.where(pick, -jnp.inf, v)
        sel = jnp.where(pick, 1.0, sel)
    live = (sel > 0.5) & (j_io <= cur)
    sel_ref[0, 0, 0:n_blk, :] = jnp.where(live, 0.0, NEG)
    sel_ref[0, 0, n_blk:n_blk + 8, :] = jnp.full((8, Q_BLOCK), NEG, F32)
    cnt_ref[0, 0] = lax.dot_general(jnp.ones((8, Q_BLOCK), BF16), jnp.where(live, 1.0, 0.0).astype(BF16),
                                    (((1,), (1,)), ((), ())), preferred_element_type=F32)


def cmp_attention(feat_t, kc, vc, slopes, ov_t, S):
    G = NSA_KV_HEADS
    n_qb = S // Q_BLOCK
    n_blk = S // SEL_BLOCK
    ncp = kc.shape[1]
    n_sel = min(SEL_TOPK, n_blk)
    return pl.pallas_call(
        functools.partial(_cmp_attn_kernel, n_sel=n_sel),
        out_shape=(jax.ShapeDtypeStruct((G, n_qb, HEAD_DIM, QL), F32),
                   jax.ShapeDtypeStruct((G, n_qb, n_blk + 8, Q_BLOCK), F32),
                   jax.ShapeDtypeStruct((G, n_qb, 8, n_blk), F32)),
        grid=(G, n_qb),
        in_specs=[pl.BlockSpec((HPG * HEAD_DIM, Q_BLOCK), lambda g, i: (g, i)),
                  pl.BlockSpec((1, ncp, HEAD_DIM), lambda g, i: (g, 0, 0)),
                  pl.BlockSpec((1, ncp, HEAD_DIM), lambda g, i: (g, 0, 0)),
                  pl.BlockSpec((1, 1, QL), lambda g, i: (g, 0, 0)),
                  pl.BlockSpec((n_blk, ncp), lambda g, i: (0, 0))],
        out_specs=(pl.BlockSpec((1, 1, HEAD_DIM, QL), lambda g, i: (g, i, 0, 0)),
                   pl.BlockSpec((1, 1, n_blk + 8, Q_BLOCK), lambda g, i: (g, i, 0, 0)),
                   pl.BlockSpec((1, 1, 8, n_blk), lambda g, i: (g, i, 0, 0))),
        compiler_params=_cparams(("parallel", "parallel")),
        name="nsa_cmp_select",
    )(feat_t, kc, vc, slopes, ov_t)


KT = 128
SUP = 4
AUG = 16


def _sel_win_kernel(ids_ref, nsup_ref, q_ref, ks_ref, vs_ref, kw_ref, vw_ref, selb_ref, oc_ref, sl_ref, gate_ref,
                    gain_ref, o_ref, *, nt):
    g = pl.program_id(0)
    qb = pl.program_id(1)
    n_qb = pl.num_programs(1)
    qt = _load_qt(q_ref)
    slope_rows = sl_ref[0]

    def q_aug(mask_rows):
        aug = jnp.concatenate([slope_rows, mask_rows], axis=0).astype(BF16)
        return jnp.concatenate([qt, aug], axis=0)

    def k_aug(kt, hi_lane, with_blocks):
        n = kt.shape[1]
        r = lax.broadcasted_iota(jnp.int32, (8, n), 0)
        lane = lax.broadcasted_iota(jnp.int32, (8, n), 1)
        lo = (lane & (KT - 1)).astype(F32)
        pos_rows = jnp.where(r < 3, hi_lane, jnp.where(r < 6, lo, 0.0))
        if with_blocks:
            blk_rows = jnp.where((lane // SEL_BLOCK) == r, 1.0, 0.0)
        else:
            blk_rows = jnp.zeros((8, n), F32)
        return jnp.concatenate([kt, jnp.concatenate([pos_rows, blk_rows], axis=0).astype(BF16)], axis=0)

    def v_aug(vt):
        n = vt.shape[1]
        ones = jnp.where(lax.broadcasted_iota(jnp.int32, (AUG, n), 0) == 0, 1.0, 0.0).astype(BF16)
        return jnp.concatenate([vt, ones], axis=0)

    def mask_rows(block_ids):
        r = lax.broadcasted_iota(jnp.int32, (8, Q_BLOCK), 0)
        rows = jnp.zeros((8, Q_BLOCK), F32)
        for i, b in enumerate(block_ids):
            row = jnp.broadcast_to(selb_ref[0, 0, pl.ds(b, 1), :], (8, Q_BLOCK))
            rows = jnp.where(r == i, row, rows)
        return jnp.concatenate([rows] * HPG, axis=1)

    def tdot(a, b):
        return lax.dot_general(a, b, (((0,), (0,)), ((), ())), preferred_element_type=F32)

    row_io = lax.broadcasted_iota(jnp.int32, (KT, QL), 0)
    qoff = lax.broadcasted_iota(jnp.int32, (KT, QL), 1) & (Q_BLOCK - 1)

    nw = WINDOW // KT + 1
    tiles_w = [qb - (nw - 1) + i for i in range(nw)]
    k0w = [pl.multiple_of(jnp.maximum(t, 0) * KT, KT) for t in tiles_w]
    ktw = jnp.concatenate([kw_ref[:, pl.ds(k0, KT)] for k0 in k0w], axis=1)
    vtw = jnp.concatenate([vw_ref[:, pl.ds(k0, KT)] for k0 in k0w], axis=1)
    hi_w = ((lax.broadcasted_iota(jnp.int32, (1, nw * KT), 1) // KT - (nw - 1)) * KT).astype(F32)
    s = tdot(k_aug(ktw, hi_w, False), q_aug(jnp.zeros((8, QL), F32)))
    parts = []
    for i in range(nw):
        si = s[i * KT:(i + 1) * KT]
        if i == 0:
            si = jnp.where((row_io > qoff) & (tiles_w[i] >= 0), si, NEG)
        elif i == nw - 1:
            si = jnp.where(row_io <= qoff, si, NEG)
        else:
            si = jnp.where(tiles_w[i] >= 0, si, NEG)
        parts.append(si)
    s = jnp.concatenate(parts, axis=0)
    m_w = jnp.max(s, axis=0, keepdims=True)
    acc_w = jnp.dot(v_aug(vtw), jnp.exp2(s - m_w).astype(BF16), preferred_element_type=F32)

    k0d = pl.multiple_of(qb * KT, KT)
    zero_hi = jnp.zeros((1, KT), F32)
    s = tdot(k_aug(ks_ref[:, pl.ds(k0d, KT)], zero_hi, True), q_aug(mask_rows([2 * qb, 2 * qb + 1])))
    s = jnp.where(row_io <= qoff, s, NEG)
    m_s = jnp.max(s, axis=0, keepdims=True)
    acc_s = jnp.dot(v_aug(vs_ref[:, pl.ds(k0d, KT)]), jnp.exp2(s - m_s).astype(BF16),
                    preferred_element_type=F32)

    lane_s = lax.broadcasted_iota(jnp.int32, (1, SUP * KT), 1)

    def sup_body(si, carry):
        m, acc = carry
        base = (g * n_qb + qb) * nt + si * SUP
        tids = [ids_ref[base + i] for i in range(SUP)]
        tcl = [jnp.minimum(t, nt - 1) for t in tids]
        k0s = [pl.multiple_of(t * KT, KT) for t in tcl]
        kt = jnp.concatenate([ks_ref[:, pl.ds(k0, KT)] for k0 in k0s], axis=1)
        vt = jnp.concatenate([vs_ref[:, pl.ds(k0, KT)] for k0 in k0s], axis=1)
        hi = (tcl[SUP - 1] - qb) * KT
        for i in range(SUP - 2, -1, -1):
            hi = jnp.where(lane_s < (i + 1) * KT, (tcl[i] - qb) * KT, hi)
        blocks = [2 * t + j for t in tids for j in range(2)]
        s = tdot(k_aug(kt, hi.astype(F32), True), q_aug(mask_rows(blocks)))
        m_new = jnp.maximum(m, jnp.max(s, axis=0, keepdims=True))
        p = jnp.exp2(s - m_new).astype(BF16)
        acc = jnp.exp2(m - m_new) * acc + jnp.dot(v_aug(vt), p, preferred_element_type=F32)
        return m_new, acc

    m_s, acc_s = lax.fori_loop(0, nsup_ref[g * n_qb + qb], sup_body, (m_s, acc_s))

    gts = _sigmoid(gate_ref[0, 0])
    o_s = acc_s[0:HEAD_DIM] / acc_s[HEAD_DIM:HEAD_DIM + 1]
    o_w = acc_w[0:HEAD_DIM] / acc_w[HEAD_DIM:HEAD_DIM + 1]
    o = gts[0:1] * oc_ref[0, 0] + gts[1:2] * o_s + gts[2:3] * o_w
    ms = jnp.mean(o * o, axis=0, keepdims=True)
    y = o * lax.rsqrt(ms + NORM_EPS) * gain_ref[0]
    yt = jnp.concatenate([y[:, h * Q_BLOCK:(h + 1) * Q_BLOCK] for h in range(HPG)], axis=0)
    o_ref[...] = yt.T


def sel_win_attention(tile_ids, n_sup, feat_t, selb, oc_t, slope_rows, gates_t, gain_t, S):
    G = NSA_KV_HEADS
    n_qb = S // Q_BLOCK
    n_blk = S // SEL_BLOCK
    nt = S // KT
    kv_spec = lambda base: pl.BlockSpec((HEAD_DIM, S), lambda g, i, ids, ns: (base // HEAD_DIM + g, 0))
    grid_spec = pltpu.PrefetchScalarGridSpec(
        num_scalar_prefetch=2,
        grid=(G, n_qb),
        in_specs=[pl.BlockSpec((HPG * HEAD_DIM, Q_BLOCK), lambda g, i, ids, ns: (g, i)),
                  kv_spec(FT_KS), kv_spec(FT_VS), kv_spec(FT_KW), kv_spec(FT_VW),
                  pl.BlockSpec((1, 1, n_blk + 8, Q_BLOCK), lambda g, i, ids, ns: (g, i, 0, 0)),
                  pl.BlockSpec((1, 1, HEAD_DIM, QL), lambda g, i, ids, ns: (g, i, 0, 0)),
                  pl.BlockSpec((1, 8, QL), lambda g, i, ids, ns: (g, 0, 0)),
                  pl.BlockSpec((1, 1, 3, QL), lambda g, i, ids, ns: (g, i, 0, 0)),
                  pl.BlockSpec((1, HEAD_DIM, QL), lambda g, i, ids, ns: (g, 0, 0))],
        out_specs=pl.BlockSpec((Q_BLOCK, HPG * HEAD_DIM), lambda g, i, ids, ns: (i, g)),
    )
    return pl.pallas_call(
        functools.partial(_sel_win_kernel, nt=nt),
        out_shape=jax.ShapeDtypeStruct((S, NSA_WIDTH), F32),
        grid_spec=grid_spec,
        compiler_params=_cparams(("parallel", "parallel")),
        name="nsa_sel_win",
    )(tile_ids, n_sup, feat_t, feat_t, feat_t, feat_t, feat_t, selb, oc_t, slope_rows, gates_t, gain_t)


def _conv_kernel(*refs, width, glu, post):
    if post == "ln_silu":
        x_ref, halo_ref, w_ref, b_ref, lg_ref, lb_ref, o_ref, u_ref = refs
    else:
        x_ref, halo_ref, w_ref, b_ref, o_ref, u_ref = refs
    i = pl.program_id(0)
    T = o_ref.shape[0]
    H = halo_ref.shape[0]
    C = o_ref.shape[1]

    def pre(v):
        return v[:, :C] * _sigmoid(v[:, C:]) if glu else v

    u_ref[0:H, :] = jnp.where(i > 0, pre(halo_ref[...]), 0.0)
    u_ref[H:H + T, :] = pre(x_ref[...])
    acc = jnp.broadcast_to(b_ref[...], (T, C))
    base = H - (width - 1)
    for k in range(width):
        acc = acc + w_ref[k:k + 1, :] * u_ref[base + k:base + k + T, :]
    if post == "ln_silu":
        mu = jnp.mean(acc, axis=-1, keepdims=True)
        xc = acc - mu
        var = jnp.mean(xc * xc, axis=-1, keepdims=True)
        acc = xc * lax.rsqrt(var + LN_EPS) * lg_ref[...] + lb_ref[...]
    o_ref[...] = acc * _sigmoid(acc)


def causal_conv(x, col_block, w, b, ln=None, *, glu, T):
    S = x.shape[0]
    width, C = w.shape
    cin = 2 * C if glu else C
    H = -(-(width - 1) // 8) * 8
    post = "ln_silu" if ln is not None else "silu"
    in_specs = [pl.BlockSpec((T, cin), lambda i: (i, col_block)),
                pl.BlockSpec((H, cin), lambda i: (jnp.maximum(i * (T // H) - 1, 0), col_block)),
                pl.BlockSpec((width, C), lambda i: (0, 0)),
                pl.BlockSpec((1, C), lambda i: (0, 0))]
    args = [x, x, w, b]
    if ln is not None:
        in_specs += [pl.BlockSpec((1, C), lambda i: (0, 0))] * 2
        args += list(ln)
    return pl.pallas_call(
        functools.partial(_conv_kernel, width=width, glu=glu, post=post),
        out_shape=jax.ShapeDtypeStruct((S, C), F32),
        grid=(S // T,),
        in_specs=in_specs,
        out_specs=pl.BlockSpec((T, C), lambda i: (i, 0)),
        scratch_shapes=[pltpu.VMEM((H + T, C), F32)],
        compiler_params=_cparams(("parallel",)),
        name="causal_conv_glu" if glu else "causal_conv",
    )(*args)


def _log_sigmoid(x):
    return jnp.minimum(x, 0.0) - jnp.log(1.0 + jnp.exp(-jnp.abs(x)))


def _mlstm_kernel(qk_ref, v_ref, o_ref, sm_ref, gt_ref, bcol_ref, brow_ref, gain_ref, out_ref, c_ref, m_ref):
    L = qk_ref.shape[0]
    DH = MLSTM_HEAD_DIM
    W = MLSTM_WIDTH

    @pl.when(pl.program_id(0) == 0)
    def _():
        c_ref[...] = jnp.zeros_like(c_ref)
        m_ref[...] = jnp.zeros_like(m_ref)

    small = sm_ref[...]
    lane = lax.broadcasted_iota(jnp.int32, small.shape, 1)
    t_io = lax.broadcasted_iota(jnp.int32, (L, L), 0)
    s_io = lax.broadcasted_iota(jnp.int32, (L, L), 1)
    causal = s_io <= t_io
    ones_col = (lax.broadcasted_iota(jnp.int32, (L, DH), 1) == 0).astype(BF16)

    def column(c):
        return jnp.sum(jnp.where(lane == c, small, 0.0), axis=1, keepdims=True)

    for h in range(MLSTM_HEADS):
        q = qk_ref[:, h * DH:(h + 1) * DH].astype(BF16)
        k = qk_ref[:, W + h * DH:W + (h + 1) * DH] * (DH ** -0.5)
        v = v_ref[:, h * DH:(h + 1) * DH].astype(BF16)
        vaug = jnp.concatenate([v, ones_col], axis=1)
        i_col = column(SM_I + h) + bcol_ref[h:h + 1, :]
        f_col = column(SM_F + h) + bcol_ref[MLSTM_HEADS + h:MLSTM_HEADS + h + 1, :]
        i_row = gt_ref[h:h + 1, :] + brow_ref[h:h + 1, :]
        f_row = gt_ref[MLSTM_HEADS + h:MLSTM_HEADS + h + 1, :] + brow_ref[MLSTM_HEADS + h:MLSTM_HEADS + h + 1, :]
        lf_col = _log_sigmoid(f_col)
        lf_row = _log_sigmoid(f_row)
        b_col = jnp.sum(jnp.where(causal, lf_row, 0.0), axis=1, keepdims=True)
        b_row = jnp.sum(jnp.where(t_io <= s_io, lf_col, 0.0), axis=0, keepdims=True)
        b_end = jnp.sum(lf_row, axis=1, keepdims=True)
        m0 = m_ref[h:h + 1, 0:1]
        c0 = c_ref[h]

        dmat = jnp.where(causal, b_col - b_row + i_row, -jnp.inf)
        inter = b_col + m0
        m_t = jnp.maximum(inter, jnp.max(dmat, axis=1, keepdims=True))
        qk = lax.dot_general(q, k.astype(BF16), (((1,), (1,)), ((), ())), preferred_element_type=F32)
        smat = qk * jnp.exp(dmat - m_t)
        w_inter = jnp.exp(inter - m_t)
        r = (jnp.dot(smat.astype(BF16), vaug, preferred_element_type=F32)
             + w_inter * jnp.dot(q, c0.astype(BF16), preferred_element_type=F32))
        num = r[:, :DH]
        den = jnp.sum(r[:, DH:], axis=1, keepdims=True)
        hh = num / jnp.maximum(jnp.abs(den), jnp.exp(-m_t))

        a_row = b_end - b_row + i_row
        a_col = b_end - b_col + i_col
        m_new = jnp.maximum(b_end + m0, jnp.max(a_row, axis=1, keepdims=True))
        kw = (k * jnp.exp(a_col - m_new)).astype(BF16)
        c_ref[h] = jnp.exp(b_end + m0 - m_new) * c0 + lax.dot_general(
            kw, vaug, (((0,), (0,)), ((), ())), preferred_element_type=F32)
        m_ref[h:h + 1, :] = jnp.broadcast_to(m_new, (1, m_ref.shape[1]))

        y = _sigmoid(o_ref[:, h * DH:(h + 1) * DH]) * hh
        ms = jnp.mean(y * y, axis=-1, keepdims=True)
        out_ref[:, h * DH:(h + 1) * DH] = y * lax.rsqrt(ms + NORM_EPS) * gain_ref[:, h * DH:(h + 1) * DH]


def mlstm(qk, rm, gates_t, bias_col, bias_row, gain, S):
    L = min(MLSTM_CHUNK, S)
    W = MLSTM_WIDTH
    return pl.pallas_call(
        _mlstm_kernel,
        out_shape=jax.ShapeDtypeStruct((S, W), F32),
        grid=(S // L,),
        in_specs=[pl.BlockSpec((L, 2 * W), lambda c: (c, 0)),
                  pl.BlockSpec((L, W), lambda c: (c, RM_V // W)),
                  pl.BlockSpec((L, W), lambda c: (c, RM_O // W)),
                  pl.BlockSpec((L, 128), lambda c: (c, RM_SMALL // 128)),
                  pl.BlockSpec((2 * MLSTM_HEADS, L), lambda c: (0, c)),
                  pl.BlockSpec((2 * MLSTM_HEADS, 1), lambda c: (0, 0)),
                  pl.BlockSpec((2 * MLSTM_HEADS, 1), lambda c: (0, 0)),
                  pl.BlockSpec((1, W), lambda c: (0, 0))],
        out_specs=pl.BlockSpec((L, W), lambda c: (c, 0)),
        scratch_shapes=[pltpu.VMEM((MLSTM_HEADS, MLSTM_HEAD_DIM, 2 * MLSTM_HEAD_DIM), F32),
                        pltpu.VMEM((8, 128), F32)],
        compiler_params=_cparams(("arbitrary",)),
        name="mlstm",
    )(qk, rm, rm, rm, gates_t, bias_col, bias_row, gain)


def _router_kernel(x_ref, g_ref, wr_ref, br_ref, xn_ref, eid_ref, gate_ref):
    xn = _rms_rows(x_ref[...], g_ref[...])
    xn_ref[...] = xn.astype(xn_ref.dtype)
    logits = lax.dot_general(wr_ref[...], xn, (((1,), (1,)), ((), ())), precision=lax.Precision.HIGHEST,
                             preferred_element_type=F32) + br_ref[...]
    tm = logits.shape[1]
    lg = logits[N_EXPERTS:N_EXPERTS + N_GROUPS, :]
    eg = jnp.exp(lg - jnp.max(lg, axis=0, keepdims=True))
    pg = eg / jnp.sum(eg, axis=0, keepdims=True)
    pg_top = jnp.max(pg, axis=0, keepdims=True)
    g_io = lax.broadcasted_iota(jnp.int32, (N_GROUPS, tm), 0)
    grp = jnp.min(jnp.where(pg == pg_top, g_io, N_GROUPS), axis=0, keepdims=True)
    le = logits[0:EXPERTS_PER_GROUP, :]
    for g in range(1, N_GROUPS):
        le = jnp.where(grp == g, logits[g * EXPERTS_PER_GROUP:(g + 1) * EXPERTS_PER_GROUP, :], le)
    ee = jnp.exp(le - jnp.max(le, axis=0, keepdims=True))
    pe = ee / jnp.sum(ee, axis=0, keepdims=True)
    e_io = lax.broadcasted_iota(jnp.int32, (EXPERTS_PER_GROUP, tm), 0)
    p1 = jnp.max(pe, axis=0, keepdims=True)
    i1 = jnp.min(jnp.where(pe == p1, e_io, EXPERTS_PER_GROUP), axis=0, keepdims=True)
    pe2 = jnp.where(e_io == i1, -1.0, pe)
    p2 = jnp.max(pe2, axis=0, keepdims=True)
    i2 = jnp.min(jnp.where(pe2 == p2, e_io, EXPERTS_PER_GROUP), axis=0, keepdims=True)
    tot = p1 + p2
    eid_ref[...] = jnp.concatenate([grp * EXPERTS_PER_GROUP + i1, grp * EXPERTS_PER_GROUP + i2], axis=0)
    gate_ref[...] = jnp.concatenate([pg_top * p1 / tot, pg_top * p2 / tot], axis=0)


def router(x, g, wr_t, br_t, tm):
    S, D = x.shape
    R = wr_t.shape[0]
    return pl.pallas_call(
        _router_kernel,
        out_shape=(jax.ShapeDtypeStruct((S, D), F32),
                   jax.ShapeDtypeStruct((2, S), jnp.int32),
                   jax.ShapeDtypeStruct((2, S), F32)),
        grid=(S // tm,),
        in_specs=[pl.BlockSpec((tm, D), lambda i: (i, 0)),
                  pl.BlockSpec((1, D), lambda i: (0, 0)),
                  pl.BlockSpec((R, D), lambda i: (0, 0)),
                  pl.BlockSpec((R, 1), lambda i: (0, 0))],
        out_specs=(pl.BlockSpec((tm, D), lambda i: (i, 0)),
                   pl.BlockSpec((2, tm), lambda i: (0, i)),
                   pl.BlockSpec((2, tm), lambda i: (0, i))),
        compiler_params=_cparams(("parallel",)),
        name="moe_router",
    )(x, g, wr_t, br_t)


def _row_gather(idx_ref, idx0, stride, src_hbm, dst, sem, n_rows):
    def body(r, carry):
        i = idx_ref[idx0 + r * stride]
        pltpu.make_async_copy(src_hbm.at[pl.ds(i, 1), :], dst.at[pl.ds(r, 1), :], sem).start()
        return carry

    lax.fori_loop(0, n_rows, body, 0)


def _rows_wait(src_hbm, dst, sem):
    pltpu.make_async_copy(src_hbm.at[pl.ds(0, dst.shape[0]), :], dst, sem).wait()


def _expert_kernel(be_ref, nu_ref, tok_ref, xn_hbm, w_ref, wg_ref, wu_ref, wd_ref, o_ref,
                   xbuf, sem, wg_s, wu_s, wd_s):
    b = pl.program_id(0)
    n_used = nu_ref[0]
    e = be_ref[b]
    prev = be_ref[jnp.maximum(b - 1, 0)]

    def gather(blk, slot):
        _row_gather(tok_ref, blk * MOE_ROWS, 1, xn_hbm, xbuf.at[slot], sem.at[slot], MOE_ROWS)

    @pl.when(b == 0)
    def _():
        gather(0, 0)

    @pl.when(b + 1 < n_used)
    def _():
        gather(b + 1, (b + 1) % 2)

    @pl.when((b == 0) | (e != prev))
    def _():
        wg_s[...] = wg_ref[0, 0].astype(BF16)
        wu_s[...] = wu_ref[0, 0].astype(BF16)
        wd_s[...] = wd_ref[0, 0].astype(BF16)

    @pl.when(b < n_used)
    def _():
        slot = b % 2
        _rows_wait(xn_hbm, xbuf.at[slot], sem.at[slot])
        x = xbuf[slot].astype(BF16)
        hg = jnp.dot(x, wg_s[...], preferred_element_type=F32)
        hu = jnp.dot(x, wu_s[...], preferred_element_type=F32)
        hb = (hg * _sigmoid(hg) * hu).astype(BF16)
        o_ref[...] = jnp.dot(hb, wd_s[...], preferred_element_type=F32) * w_ref[...]

    @pl.when(b >= n_used)
    def _():
        o_ref[...] = jnp.zeros_like(o_ref)


def expert_ffn(blk_e, n_used, row_tok, xn, row_w, w_gate, w_up, w_down, layer):
    P = row_tok.shape[0]
    D = xn.shape[1]
    n_blocks = P // MOE_ROWS
    Hd = w_gate.shape[3]
    grid_spec = pltpu.PrefetchScalarGridSpec(
        num_scalar_prefetch=3,
        grid=(n_blocks,),
        in_specs=[pl.BlockSpec(memory_space=pl.ANY),
                  pl.BlockSpec((MOE_ROWS, 1), lambda b, be, nu, tk: (b, 0)),
                  pl.BlockSpec((1, 1, D, Hd), lambda b, be, nu, tk: (layer, be[b], 0, 0)),
                  pl.BlockSpec((1, 1, D, Hd), lambda b, be, nu, tk: (layer, be[b], 0, 0)),
                  pl.BlockSpec((1, 1, Hd, D), lambda b, be, nu, tk: (layer, be[b], 0, 0))],
        out_specs=pl.BlockSpec((MOE_ROWS, D), lambda b, be, nu, tk: (b, 0)),
        scratch_shapes=[pltpu.VMEM((2, MOE_ROWS, D), F32), pltpu.SemaphoreType.DMA((2,)),
                        pltpu.VMEM((D, Hd), BF16), pltpu.VMEM((D, Hd), BF16), pltpu.VMEM((Hd, D), BF16)],
    )
    return pl.pallas_call(
        _expert_kernel,
        out_shape=jax.ShapeDtypeStruct((P, D), F32),
        grid_spec=grid_spec,
        compiler_params=_cparams(("arbitrary",)),
        name="moe_experts",
    )(blk_e, n_used, row_tok, xn, row_w, w_gate, w_up, w_down)


def _combine_kernel(pos_ref, x_ref, yb_hbm, o_ref, ybuf, sem):
    i = pl.program_id(0)
    n = pl.num_programs(0)
    tt = x_ref.shape[0]

    def gather(blk, slot):
        for k in range(2):
            _row_gather(pos_ref, blk * tt * 2 + k, 2, yb_hbm, ybuf.at[slot, k], sem.at[slot], tt)

    @pl.when(i == 0)
    def _():
        gather(0, 0)

    @pl.when(i + 1 < n)
    def _():
        gather(i + 1, (i + 1) % 2)

    slot = i % 2
    for k in range(2):
        _rows_wait(yb_hbm, ybuf.at[slot, k], sem.at[slot])
    o_ref[...] = x_ref[...] + ybuf[slot, 0] + ybuf[slot, 1]


def moe_combine(pos, x, yb, tt):
    S, D = x.shape
    grid_spec = pltpu.PrefetchScalarGridSpec(
        num_scalar_prefetch=1,
        grid=(S // tt,),
        in_specs=[pl.BlockSpec((tt, D), lambda i, pos: (i, 0)),
                  pl.BlockSpec(memory_space=pl.ANY)],
        out_specs=pl.BlockSpec((tt, D), lambda i, pos: (i, 0)),
        scratch_shapes=[pltpu.VMEM((2, 2, tt, D), F32), pltpu.SemaphoreType.DMA((2,))],
    )
    return pl.pallas_call(
        _combine_kernel,
        out_shape=jax.ShapeDtypeStruct((S, D), F32),
        grid_spec=grid_spec,
        compiler_params=_cparams(("arbitrary",)),
        name="moe_combine",
    )(pos, x, yb)


def _final_norm_kernel(x_ref, g_ref, o_ref):
    o_ref[...] = _rms_rows(x_ref[...], g_ref[...])


def final_norm(x, g, tm):
    S, D = x.shape
    return pl.pallas_call(
        _final_norm_kernel,
        out_shape=jax.ShapeDtypeStruct((S, D), F32),
        grid=(S // tm,),
        in_specs=[pl.BlockSpec((tm, D), lambda i: (i, 0)), pl.BlockSpec((1, D), lambda i: (0, 0))],
        out_specs=pl.BlockSpec((tm, D), lambda i: (i, 0)),
        compiler_params=_cparams(("parallel",)),
        name="final_norm",
    )(x, g)


LOG2E = float(np.log2(np.e))


def _alibi_lanes():
    sl = 2.0 ** (-8.0 * np.arange(1, NSA_HEADS + 1) / NSA_HEADS) * LOG2E
    sl = np.repeat(sl.reshape(NSA_KV_HEADS, HPG, 1), Q_BLOCK, axis=2).reshape(NSA_KV_HEADS, 1, QL)
    sl = jnp.asarray(sl, F32)
    s1 = sl.astype(BF16).astype(F32)
    s2 = (sl - s1).astype(BF16).astype(F32)
    s3 = (sl - s1 - s2).astype(BF16).astype(F32)
    zero = jnp.zeros_like(sl)
    return sl, jnp.concatenate([s1, s2, s3, s1, s2, s3, zero, zero], axis=1)


def _active_tiles(cnt, S):
    G, n_qb, n_blk = cnt.shape
    nt = S // KT
    act = (cnt > 0.5).reshape(G, n_qb, nt, KT // SEL_BLOCK).any(-1)
    tile = jnp.arange(nt, dtype=jnp.int32)
    act = act & (tile[None, None, :] < (jnp.arange(n_qb) * (Q_BLOCK // KT))[None, :, None])
    rank = jnp.cumsum(act.astype(jnp.int32), axis=-1) - 1
    n_act = rank[..., -1] + 1
    hit = act[..., None, :] & (rank[..., None, :] == tile[None, None, :, None])
    ids = jnp.sum(jnp.where(hit, tile[None, None, None, :], 0), axis=-1)
    ids = jnp.where(tile[None, None, :] < n_act[..., None], ids, nt).astype(jnp.int32)
    return ids.reshape(-1), ((n_act + SUP - 1) // SUP).astype(jnp.int32).reshape(-1)


def _overlap_t(n_cmp_pad, n_cmp, n_blk):
    cs = np.arange(n_cmp_pad) * CMP_STRIDE
    ss = np.arange(n_blk) * SEL_BLOCK
    ov = np.minimum(cs[None, :] + CMP_BLOCK, ss[:, None] + SEL_BLOCK) - np.maximum(cs[None, :], ss[:, None])
    ov = np.clip(ov, 0, None) / CMP_STRIDE
    ov[:, n_cmp:] = 0.0
    return jnp.asarray(ov, BF16)


def mixer(x, p, S):
    G = NSA_KV_HEADS
    tm = min(1024, S)
    w_in = p["w_in"]
    c = np.cumsum((0, NSA_WIDTH) + (NSA_KV_WIDTH,) * 6 + (3 * NSA_HEADS,) + (MLSTM_WIDTH,) * 4
                  + (MLSTM_HEADS,) * 2 + (2 * CONV_CHANNELS,))
    (q0, kc0, vc0, ks0, vs0, kw0, vw0, gt0, mq0, mk0, mv0, mo0, mi0, mf0, cu0, end) = [int(v) for v in c]
    w_rm = jnp.concatenate(
        [w_in[:, mq0:mv0], w_in[:, cu0:end], w_in[:, mv0:mi0], w_in[:, kc0:ks0], w_in[:, gt0:mq0],
         w_in[:, mi0:cu0], jnp.zeros((D_MODEL, RM_WIDTH - RM_SMALL - 56), F32)], axis=1).astype(BF16)
    w_ft = jnp.concatenate([w_in[:, q0:kc0], w_in[:, ks0:gt0]], axis=1).T.astype(BF16)
    g = p["attn_norm_g"][None, :]
    rm = norm_matmul(x, g, w_rm, tm, 768)
    q_scale = jnp.where(jnp.arange(FT_WIDTH) < FT_KS, HEAD_DIM ** -0.5 * LOG2E, 1.0).astype(F32)[:, None]
    feat_t = norm_matmul_t(x, g, w_ft, q_scale, tm, 512)

    ncp = S // CMP_STRIDE
    n_cmp = (S - CMP_BLOCK) // CMP_STRIDE + 1
    n_blk = S // SEL_BLOCK
    n_qb = S // Q_BLOCK
    kv = jnp.stack([rm[:, RM_KC:RM_KC + NSA_KV_WIDTH], rm[:, RM_VC:RM_VC + NSA_KV_WIDTH]])
    rows = kv.reshape(2, ncp, CMP_STRIDE, G, HEAD_DIM).transpose(0, 3, 1, 2, 4).reshape(2, G, ncp, -1)
    nxt = jnp.concatenate([rows[:, :, 1:], jnp.zeros_like(rows[:, :, :1])], axis=2)
    blk = jnp.concatenate([rows, nxt], axis=-1)
    pos = jnp.stack([p["cmp_pos_k"].reshape(1, -1), p["cmp_pos_v"].reshape(1, -1)])
    w1 = jnp.stack([p["cmp_w1_k"], p["cmp_w1_v"]]).astype(BF16)
    w2 = jnp.stack([p["cmp_w2_k"], p["cmp_w2_v"]]).astype(BF16)
    kvc = compress(blk, pos, w1, w2)
    slopes, slope_rows = _alibi_lanes()
    oc_t, selb, cnt = cmp_attention(feat_t, kvc[0], kvc[1], slopes, _overlap_t(ncp, n_cmp, n_blk), S)
    tile_ids, n_sup = _active_tiles(cnt[:, :, 0, :], S)
    small = rm[:, RM_SMALL:RM_SMALL + 128]
    gates_t = small[:, SM_GATES:SM_GATES + 3 * NSA_HEADS].reshape(n_qb, Q_BLOCK, G, HPG, 3)
    gates_t = gates_t.transpose(2, 0, 4, 3, 1).reshape(G, n_qb, 3, QL)
    gain_t = jnp.broadcast_to(p["nsa_out_g"].reshape(G, HPG, HEAD_DIM, 1).transpose(0, 2, 1, 3),
                              (G, HEAD_DIM, HPG, Q_BLOCK)).reshape(G, HEAD_DIM, QL)
    a_out = sel_win_attention(tile_ids, n_sup, feat_t, selb, oc_t, slope_rows, gates_t, gain_t, S)

    tc = min(512, S)
    qk = causal_conv(rm, RM_QK // (2 * MLSTM_WIDTH), p["mlstm_conv_w"], p["mlstm_conv_b"][None, :],
                     glu=False, T=tc)
    gates_m = small[:, SM_I:SM_I + 2 * MLSTM_HEADS].T
    bias = jnp.concatenate([p["mlstm_i_bias"], p["mlstm_f_bias"]])
    hm = mlstm(qk, rm, gates_m, bias[:, None], bias[:, None], p["mlstm_out_g"][None, :], S)

    cv = causal_conv(rm, RM_CU // (2 * CONV_CHANNELS), p["conv_w"], p["conv_b"][None, :],
                     (p["conv_ln_g"][None, :], p["conv_ln_b"][None, :]), glu=True, T=tc)

    return out_proj(a_out, hm, cv, x, p["w_out"].astype(BF16), min(512, S), 1024)


def moe(x, p, stacked_w, layer, S):
    tm = min(512, S)
    wr_t = jnp.concatenate([p["router_w_expert"].T, p["router_w_group"].T,
                            jnp.zeros((12, D_MODEL), F32)], axis=0)
    br_t = jnp.concatenate([p["router_b_expert"].reshape(-1), p["router_b_group"], jnp.zeros((12,), F32)])[:, None]
    xn, eid, gate = router(x, p["ffn_norm_g"][None, :], wr_t, br_t, tm)

    flat_e = eid.T.reshape(-1)
    flat_w = gate.T.reshape(-1)
    A = flat_e.shape[0]
    onehot = (flat_e[:, None] == jnp.arange(N_EXPERTS)[None, :]).astype(jnp.int32)
    rank = jnp.take_along_axis(jnp.cumsum(onehot, axis=0), flat_e[:, None], axis=1)[:, 0] - 1
    counts = jnp.sum(onehot, axis=0)
    padded = (counts + MOE_ROWS - 1) // MOE_ROWS * MOE_ROWS
    pends = jnp.cumsum(padded)
    dest = (pends - padded)[flat_e] + rank
    n_blocks = A // MOE_ROWS + N_EXPERTS
    P = n_blocks * MOE_ROWS
    row_tok = jnp.zeros((P,), jnp.int32).at[dest].set(jnp.arange(A, dtype=jnp.int32) // 2)
    row_w = jnp.zeros((P,), F32).at[dest].set(flat_w)
    n_used = (pends[-1] // MOE_ROWS).astype(jnp.int32)
    blk_start = jnp.minimum(jnp.arange(n_blocks), n_used - 1) * MOE_ROWS
    blk_e = jnp.minimum(jnp.sum(pends[None, :] <= blk_start[:, None], axis=1), N_EXPERTS - 1).astype(jnp.int32)

    yb = expert_ffn(blk_e, n_used[None], row_tok, xn, row_w[:, None], stacked_w["expert_w_gate"],
                    stacked_w["expert_w_up"], stacked_w["expert_w_down"], layer)
    return moe_combine(dest.astype(jnp.int32), x, yb, min(256, S))


_LAYER_KEYS = ("attn_norm_g", "w_in", "cmp_pos_k", "cmp_w1_k", "cmp_w2_k", "cmp_pos_v", "cmp_w1_v", "cmp_w2_v",
               "nsa_out_g", "mlstm_conv_w", "mlstm_conv_b", "mlstm_i_bias", "mlstm_f_bias", "mlstm_out_g",
               "conv_w", "conv_b", "conv_ln_g", "conv_ln_b", "w_out", "ffn_norm_g", "router_w_group",
               "router_b_group", "router_w_expert", "router_b_expert", "expert_w_gate", "expert_w_up",
               "expert_w_down")


def kernel(x, attn_norm_g, w_in, cmp_pos_k, cmp_w1_k, cmp_w2_k, cmp_pos_v, cmp_w1_v, cmp_w2_v, nsa_out_g, mlstm_conv_w, mlstm_conv_b, mlstm_i_bias, mlstm_f_bias, mlstm_out_g, conv_w, conv_b, conv_ln_g, conv_ln_b, w_out, ffn_norm_g, router_w_group, router_b_group, router_w_expert, router_b_expert, expert_w_gate, expert_w_up, expert_w_down, final_norm_g):
    stacked = (attn_norm_g, w_in, cmp_pos_k, cmp_w1_k, cmp_w2_k, cmp_pos_v, cmp_w1_v, cmp_w2_v, nsa_out_g,
               mlstm_conv_w, mlstm_conv_b, mlstm_i_bias, mlstm_f_bias, mlstm_out_g, conv_w, conv_b, conv_ln_g,
               conv_ln_b, w_out, ffn_norm_g, router_w_group, router_b_group, router_w_expert, router_b_expert,
               expert_w_gate, expert_w_up, expert_w_down)
    B, S, D = x.shape
    assert B == 1 and D == D_MODEL and S % 1024 == 0
    h = x.reshape(S, D)
    stacked = dict(zip(_LAYER_KEYS, stacked))
    for l in range(attn_norm_g.shape[0]):
        p = {k: v[l] for k, v in stacked.items() if not k.startswith("expert_w")}
        h = mixer(h, p, S)
        h = moe(h, p, stacked, l, S)
    return final_norm(h, final_norm_g[None, :], min(512, S)).reshape(B, S, D)
```

```python
import functools

import numpy as np
import jax
import jax.numpy as jnp
from jax import lax
from jax.experimental import pallas as pl
from jax.experimental.pallas import tpu as pltpu

F32 = jnp.float32
BF16 = jnp.bfloat16

D_MODEL = 2048
HEAD_DIM = 64
NSA_WIDTH = D_MODEL // 2
NSA_HEADS = NSA_WIDTH // HEAD_DIM
NSA_KV_HEADS = NSA_HEADS // 4
HPG = NSA_HEADS // NSA_KV_HEADS
NSA_KV_WIDTH = NSA_KV_HEADS * HEAD_DIM
CMP_BLOCK = 32
CMP_STRIDE = 16
CMP_HIDDEN = 4 * HEAD_DIM
SEL_BLOCK = 64
SEL_TOPK = 16
WINDOW = 512
Q_BLOCK = 128
SEL_FORCE = 1.0e4
MLSTM_WIDTH = D_MODEL // 4
MLSTM_HEADS = 4
MLSTM_HEAD_DIM = MLSTM_WIDTH // MLSTM_HEADS
MLSTM_CONV = 4
CONV_CHANNELS = D_MODEL // 4
CONV_WIDTH = 31
N_GROUPS = 4
EXPERTS_PER_GROUP = 8
N_EXPERTS = N_GROUPS * EXPERTS_PER_GROUP
EXPERT_HIDDEN = D_MODEL // 4
NORM_EPS = 1e-6
LN_EPS = 1e-5

NEG = -1.0e30
QL = HPG * Q_BLOCK
MLSTM_CHUNK = 256
MOE_ROWS = 256
VMEM_LIMIT = 52 * 1024 * 1024

RM_QK, RM_CU, RM_V, RM_O, RM_KC, RM_VC, RM_SMALL = 0, 1024, 2048, 2560, 3072, 3328, 3584
RM_WIDTH = 3840
SM_GATES, SM_I, SM_F = 0, 48, 52
FT_Q, FT_KS, FT_VS, FT_KW, FT_VW = 0, 1024, 1280, 1536, 1792
FT_WIDTH = 2048


def _cparams(sem, vmem=VMEM_LIMIT):
    return pltpu.CompilerParams(dimension_semantics=sem, vmem_limit_bytes=vmem)


def _sigmoid(x):
    return 1.0 / (1.0 + jnp.exp(-x))


def _rms_rows(x, g):
    ms = jnp.mean(x * x, axis=-1, keepdims=True)
    return x * lax.rsqrt(ms + NORM_EPS) * g


def _norm_mm_kernel(x_ref, g_ref, w_ref, o_ref, h_ref):
    @pl.when(pl.program_id(1) == 0)
    def _():
        h_ref[...] = _rms_rows(x_ref[...], g_ref[...]).astype(BF16)

    o_ref[...] = jnp.dot(h_ref[...], w_ref[...], preferred_element_type=F32).astype(o_ref.dtype)


def norm_matmul(x, g, w, tm, tn):
    M, K = x.shape
    N = w.shape[1]
    return pl.pallas_call(
        _norm_mm_kernel,
        out_shape=jax.ShapeDtypeStruct((M, N), F32),
        grid=(M // tm, N // tn),
        in_specs=[pl.BlockSpec((tm, K), lambda i, j: (i, 0)),
                  pl.BlockSpec((1, K), lambda i, j: (0, 0)),
                  pl.BlockSpec((K, tn), lambda i, j: (0, j))],
        out_specs=pl.BlockSpec((tm, tn), lambda i, j: (i, j)),
        scratch_shapes=[pltpu.VMEM((tm, K), BF16)],
        compiler_params=_cparams(("parallel", "arbitrary")),
        name="norm_matmul",
    )(x, g, w)


def _norm_mm_t_kernel(x_ref, g_ref, wt_ref, sc_ref, o_ref, h_ref):
    @pl.when(pl.program_id(1) == 0)
    def _():
        h_ref[...] = _rms_rows(x_ref[...], g_ref[...]).astype(BF16)

    o = lax.dot_general(wt_ref[...], h_ref[...], (((1,), (1,)), ((), ())), preferred_element_type=F32)
    o_ref[...] = (o * sc_ref[...]).astype(o_ref.dtype)


def norm_matmul_t(x, g, wt, row_scale, tm, tn):
    M, K = x.shape
    N = wt.shape[0]
    return pl.pallas_call(
        _norm_mm_t_kernel,
        out_shape=jax.ShapeDtypeStruct((N, M), BF16),
        grid=(M // tm, N // tn),
        in_specs=[pl.BlockSpec((tm, K), lambda i, j: (i, 0)),
                  pl.BlockSpec((1, K), lambda i, j: (0, 0)),
                  pl.BlockSpec((tn, K), lambda i, j: (j, 0)),
                  pl.BlockSpec((tn, 1), lambda i, j: (j, 0))],
        out_specs=pl.BlockSpec((tn, tm), lambda i, j: (j, i)),
        scratch_shapes=[pltpu.VMEM((tm, K), BF16)],
        compiler_params=_cparams(("parallel", "arbitrary")),
        name="norm_matmul_t",
    )(x, g, wt, row_scale)


def _out_mm_kernel(a_ref, m_ref, c_ref, x_ref, w_ref, o_ref):
    h = jnp.concatenate([a_ref[...], m_ref[...], c_ref[...]], axis=-1).astype(BF16)
    o_ref[...] = x_ref[...] + jnp.dot(h, w_ref[...], preferred_element_type=F32)


def out_proj(a, m, c, x, w, tm, tn):
    M = x.shape[0]
    N = w.shape[1]
    return pl.pallas_call(
        _out_mm_kernel,
        out_shape=jax.ShapeDtypeStruct((M, N), F32),
        grid=(N // tn, M // tm),
        in_specs=[pl.BlockSpec((tm, a.shape[1]), lambda j, i: (i, 0)),
                  pl.BlockSpec((tm, m.shape[1]), lambda j, i: (i, 0)),
                  pl.BlockSpec((tm, c.shape[1]), lambda j, i: (i, 0)),
                  pl.BlockSpec((tm, tn), lambda j, i: (i, j)),
                  pl.BlockSpec((w.shape[0], tn), lambda j, i: (0, j))],
        out_specs=pl.BlockSpec((tm, tn), lambda j, i: (i, j)),
        compiler_params=_cparams(("parallel", "parallel")),
        name="out_proj",
    )(a, m, c, x, w)


def _compress_kernel(blk_ref, pos_ref, w1_ref, w2t_ref, o_ref):
    x = (blk_ref[0, 0] + pos_ref[0]).astype(BF16)
    hid = jnp.dot(x, w1_ref[0], preferred_element_type=F32)
    hid = hid * _sigmoid(hid)
    o_ref[0, 0] = lax.dot_general(w2t_ref[0], hid.astype(BF16), (((1,), (1,)), ((), ())),
                                  preferred_element_type=F32).astype(o_ref.dtype)


def compress(blk, pos, w1, w2t):
    _, G, NCP, LD = blk.shape
    return pl.pallas_call(
        _compress_kernel,
        out_shape=jax.ShapeDtypeStruct((2, G, HEAD_DIM, NCP), BF16),
        grid=(2, G),
        in_specs=[pl.BlockSpec((1, 1, NCP, LD), lambda a, g: (a, g, 0, 0)),
                  pl.BlockSpec((1, 1, LD), lambda a, g: (a, 0, 0)),
                  pl.BlockSpec((1, LD, CMP_HIDDEN), lambda a, g: (a, 0, 0)),
                  pl.BlockSpec((1, HEAD_DIM, CMP_HIDDEN), lambda a, g: (a, 0, 0))],
        out_specs=pl.BlockSpec((1, 1, HEAD_DIM, NCP), lambda a, g: (a, g, 0, 0)),
        compiler_params=_cparams(("parallel", "parallel")),
        name="nsa_compress",
    )(blk, pos, w1, w2t)


def _load_qt(q_ref):
    q = q_ref[...]
    return jnp.concatenate([q[h * HEAD_DIM:(h + 1) * HEAD_DIM, :] for h in range(HPG)], axis=1)


def _tdot(a, b):
    return lax.dot_general(a, b, (((0,), (0,)), ((), ())), preferred_element_type=F32)


def _cmp_attn_kernel(q_ref, kct_ref, vct_ref, sl_ref, ov_ref, oc_ref, sel_ref, cnt_ref, imp_ref, *, n_sel, cch):
    qb = pl.program_id(1)
    qt = _load_qt(q_ref)
    q_aug = jnp.concatenate([qt, jnp.concatenate([sl_ref[0], jnp.zeros((8, QL), F32)], axis=0).astype(BF16)],
                            axis=0)
    ncp = kct_ref.shape[2]
    n_chunks = ncp // cch
    per_qb = Q_BLOCK // CMP_STRIDE
    last_end = CMP_BLOCK - 1
    need = jnp.minimum((qb * per_qb + per_qb - 2) // cch + 1, n_chunks)
    qoff = lax.broadcasted_iota(jnp.int32, (1, QL), 1) & (Q_BLOCK - 1)
    col_ok = (qb * Q_BLOCK + qoff) >= last_end

    for c in range(1, n_chunks + 1):
        @pl.when(need == c)
        def _(c=c):
            R = c * cch
            r8 = lax.broadcasted_iota(jnp.int32, (8, R), 0)
            n8 = lax.broadcasted_iota(jnp.int32, (8, R), 1)
            hi = ((n8 // per_qb - qb) * Q_BLOCK).astype(F32)
            lo = ((n8 % per_qb) * CMP_STRIDE + last_end).astype(F32)
            pos_rows = jnp.where(r8 < 3, hi, jnp.where(r8 < 6, lo, 0.0))
            k_aug = jnp.concatenate(
                [kct_ref[0, :, 0:R], jnp.concatenate([pos_rows, jnp.zeros((8, R), F32)], axis=0).astype(BF16)],
                axis=0)
            s = _tdot(k_aug, q_aug)
            t0 = max(R - 2 * cch, 0)
            n_io = t0 + lax.broadcasted_iota(jnp.int32, (R - t0, QL), 0)
            tail = jnp.where(n_io * CMP_STRIDE + last_end - qb * Q_BLOCK <= qoff, s[t0:R], NEG)
            s = tail if t0 == 0 else jnp.concatenate([s[0:t0], tail], axis=0)
            m = jnp.max(s, axis=0, keepdims=True)
            p = jnp.exp2(s - m)
            l = jnp.maximum(jnp.sum(p, axis=0, keepdims=True), 1e-30)
            p = p * jnp.where(col_ok, 1.0 / l, 0.0)
            oc_ref[0, 0] = jnp.dot(vct_ref[0, :, 0:R], p.astype(BF16), preferred_element_type=F32)
            ps = p[:, 0:Q_BLOCK]
            for h in range(1, HPG):
                ps = ps + p[:, h * Q_BLOCK:(h + 1) * Q_BLOCK]
            hi_p = ps.astype(BF16)
            r1 = ps - hi_p.astype(F32)
            mid_p = r1.astype(BF16)
            lo_p = (r1 - mid_p.astype(F32)).astype(BF16)
            ov = ov_ref[:, 0:R]
            imp_ref[...] = (jnp.dot(ov, hi_p, preferred_element_type=F32)
                            + jnp.dot(ov, mid_p, preferred_element_type=F32)
                            + jnp.dot(ov, lo_p, preferred_element_type=F32))

    imp = imp_ref[...]
    n_blk = imp.shape[0]
    j_io = lax.broadcasted_iota(jnp.int32, (n_blk, Q_BLOCK), 0)
    tq = qb * Q_BLOCK + lax.broadcasted_iota(jnp.int32, (n_blk, Q_BLOCK), 1)
    cur = tq // SEL_BLOCK
    forced = (j_io == 0) | (j_io == cur) | (j_io == cur - 1)
    v = jnp.where(forced, -jnp.inf, jnp.where(j_io <= cur, imp, -SEL_FORCE))
    sel = jnp.where(forced, 1.0, 0.0)
    for _ in range(n_sel - 3):
        mx = jnp.max(v, axis=0, keepdims=True)
        idx = jnp.min(jnp.where(v == mx, j_io, n_blk), axis=0, keepdims=True)
        pick = j_io == idx
        v = jnp.where(pick, -jnp.inf, v)
        sel = jnp.where(pick, 1.0, sel)
    live = (sel > 0.5) & (j_io <= cur)
    sel_ref[0, 0, 0:n_blk, :] = jnp.where(live, 0.0, NEG)
    sel_ref[0, 0, n_blk:n_blk + 8, :] = jnp.full((8, Q_BLOCK), NEG, F32)
    cnt_ref[0, 0] = lax.dot_general(jnp.ones((8, Q_BLOCK), BF16), jnp.where(live, 1.0, 0.0).astype(BF16),
                                    (((1,), (1,)), ((), ())), preferred_element_type=F32)


def cmp_attention(feat_t, kc_t, vc_t, slope_rows, ov_t, S):
    G = NSA_KV_HEADS
    n_qb = S // Q_BLOCK
    n_blk = S // SEL_BLOCK
    ncp = kc_t.shape[2]
    n_sel = min(SEL_TOPK, n_blk)
    return pl.pallas_call(
        functools.partial(_cmp_attn_kernel, n_sel=n_sel, cch=min(128, ncp)),
        out_shape=(jax.ShapeDtypeStruct((G, n_qb, HEAD_DIM, QL), F32),
                   jax.ShapeDtypeStruct((G, n_qb, n_blk + 8, Q_BLOCK), F32),
                   jax.ShapeDtypeStruct((G, n_qb, 8, n_blk), F32)),
        grid=(G, n_qb),
        in_specs=[pl.BlockSpec((HPG * HEAD_DIM, Q_BLOCK), lambda g, i: (g, i)),
                  pl.BlockSpec((1, HEAD_DIM, ncp), lambda g, i: (g, 0, 0)),
                  pl.BlockSpec((1, HEAD_DIM, ncp), lambda g, i: (g, 0, 0)),
                  pl.BlockSpec((1, 8, QL), lambda g, i: (g, 0, 0)),
                  pl.BlockSpec((n_blk, ncp), lambda g, i: (0, 0))],
        out_specs=(pl.BlockSpec((1, 1, HEAD_DIM, QL), lambda g, i: (g, i, 0, 0)),
                   pl.BlockSpec((1, 1, n_blk + 8, Q_BLOCK), lambda g, i: (g, i, 0, 0)),
                   pl.BlockSpec((1, 1, 8, n_blk), lambda g, i: (g, i, 0, 0))),
        scratch_shapes=[pltpu.VMEM((n_blk, Q_BLOCK), F32)],
        compiler_params=_cparams(("parallel", "parallel")),
        name="nsa_cmp_select",
    )(feat_t, kc_t, vc_t, slope_rows, ov_t)


KT = 128
SUP = 4
AUG = 16


def _sel_win_kernel(ids_ref, nsup_ref, q_ref, ks_ref, vs_ref, kw_ref, vw_ref, selb_ref, oc_ref, sl_ref, gate_ref,
                    gain_ref, o_ref, *, nt, gp):
    qb = pl.program_id(1)
    n_qb = pl.num_programs(1)

    def k_aug(kt, hi_lane, with_blocks):
        n = kt.shape[1]
        r = lax.broadcasted_iota(jnp.int32, (8, n), 0)
        lane = lax.broadcasted_iota(jnp.int32, (8, n), 1)
        lo = (lane & (KT - 1)).astype(F32)
        pos_rows = jnp.where(r < 3, hi_lane, jnp.where(r < 6, lo, 0.0))
        if with_blocks:
            blk_rows = jnp.where((lane // SEL_BLOCK) == r, 1.0, 0.0)
        else:
            blk_rows = jnp.zeros((8, n), F32)
        return jnp.concatenate([kt, jnp.concatenate([pos_rows, blk_rows], axis=0).astype(BF16)], axis=0)

    def v_aug(vt):
        n = vt.shape[1]
        ones = jnp.where(lax.broadcasted_iota(jnp.int32, (AUG, n), 0) == 0, 1.0, 0.0).astype(BF16)
        return jnp.concatenate([vt, ones], axis=0)

    def tdot(a, b):
        return lax.dot_general(a, b, (((0,), (0,)), ((), ())), preferred_element_type=F32)

    row_io = lax.broadcasted_iota(jnp.int32, (KT, QL), 0)
    qoff = lax.broadcasted_iota(jnp.int32, (KT, QL), 1) & (Q_BLOCK - 1)
    lane_s = lax.broadcasted_iota(jnp.int32, (1, SUP * KT), 1)
    nw = WINDOW // KT + 1
    tiles_w = [qb - (nw - 1) + i for i in range(nw)]
    k0w = [pl.multiple_of(jnp.maximum(t, 0) * KT, KT) for t in tiles_w]
    hi_w = ((lax.broadcasted_iota(jnp.int32, (1, nw * KT), 1) // KT - (nw - 1)) * KT).astype(F32)
    k0d = pl.multiple_of(qb * KT, KT)
    zero_hi = jnp.zeros((1, KT), F32)

    def make_group(j):
        g = pl.program_id(0) * gp + j
        rows = slice(j * HEAD_DIM, (j + 1) * HEAD_DIM)
        q = q_ref[j * HPG * HEAD_DIM:(j + 1) * HPG * HEAD_DIM, :]
        qt = jnp.concatenate([q[h * HEAD_DIM:(h + 1) * HEAD_DIM, :] for h in range(HPG)], axis=1)
        slope_rows = sl_ref[j]

        def q_aug(mask):
            aug = jnp.concatenate([slope_rows, mask], axis=0).astype(BF16)
            return jnp.concatenate([qt, aug], axis=0)

        def mask_rows(block_ids):
            r = lax.broadcasted_iota(jnp.int32, (8, Q_BLOCK), 0)
            out = jnp.zeros((8, Q_BLOCK), F32)
            for i, b in enumerate(block_ids):
                row = jnp.broadcast_to(selb_ref[j, 0, pl.ds(b, 1), :], (8, Q_BLOCK))
                out = jnp.where(r == i, row, out)
            return jnp.concatenate([out] * HPG, axis=1)

        ktw = jnp.concatenate([kw_ref[rows, pl.ds(k0, KT)] for k0 in k0w], axis=1)
        vtw = jnp.concatenate([vw_ref[rows, pl.ds(k0, KT)] for k0 in k0w], axis=1)
        s = tdot(k_aug(ktw, hi_w, False), q_aug(jnp.zeros((8, QL), F32)))
        parts = []
        for i in range(nw):
            si = s[i * KT:(i + 1) * KT]
            if i == 0:
                si = jnp.where((row_io > qoff) & (tiles_w[i] >= 0), si, NEG)
            elif i == nw - 1:
                si = jnp.where(row_io <= qoff, si, NEG)
            else:
                si = jnp.where(tiles_w[i] >= 0, si, NEG)
            parts.append(si)
        s = jnp.concatenate(parts, axis=0)
        m_w = jnp.max(s, axis=0, keepdims=True)
        acc_w = jnp.dot(v_aug(vtw), jnp.exp2(s - m_w).astype(BF16), preferred_element_type=F32)

        s = tdot(k_aug(ks_ref[rows, pl.ds(k0d, KT)], zero_hi, True), q_aug(mask_rows([2 * qb, 2 * qb + 1])))
        s = jnp.where(row_io <= qoff, s, NEG)
        m_s = jnp.max(s, axis=0, keepdims=True)
        acc_s = jnp.dot(v_aug(vs_ref[rows, pl.ds(k0d, KT)]), jnp.exp2(s - m_s).astype(BF16),
                        preferred_element_type=F32)

        def update(si, carry):
            m, acc = carry
            base = (g * n_qb + qb) * nt + si * SUP
            tids = [ids_ref[base + i] for i in range(SUP)]
            tcl = [jnp.minimum(t, nt - 1) for t in tids]
            k0s = [pl.multiple_of(t * KT, KT) for t in tcl]
            kt = jnp.concatenate([ks_ref[rows, pl.ds(k0, KT)] for k0 in k0s], axis=1)
            vt = jnp.concatenate([vs_ref[rows, pl.ds(k0, KT)] for k0 in k0s], axis=1)
            hi = (tcl[SUP - 1] - qb) * KT
            for i in range(SUP - 2, -1, -1):
                hi = jnp.where(lane_s < (i + 1) * KT, (tcl[i] - qb) * KT, hi)
            blocks = [2 * t + b for t in tids for b in range(2)]
            s = tdot(k_aug(kt, hi.astype(F32), True), q_aug(mask_rows(blocks)))
            m_new = jnp.maximum(m, jnp.max(s, axis=0, keepdims=True))
            p = jnp.exp2(s - m_new).astype(BF16)
            acc = jnp.exp2(m - m_new) * acc + jnp.dot(v_aug(vt), p, preferred_element_type=F32)
            return m_new, acc

        def finish(carry):
            _, acc = carry
            gts = _sigmoid(gate_ref[j, 0])
            o_s = acc[0:HEAD_DIM] / acc[HEAD_DIM:HEAD_DIM + 1]
            o_w = acc_w[0:HEAD_DIM] / acc_w[HEAD_DIM:HEAD_DIM + 1]
            o = gts[0:1] * oc_ref[j, 0] + gts[1:2] * o_s + gts[2:3] * o_w
            ms = jnp.mean(o * o, axis=0, keepdims=True)
            y = o * lax.rsqrt(ms + NORM_EPS) * gain_ref[j]
            yt = jnp.concatenate([y[:, h * Q_BLOCK:(h + 1) * Q_BLOCK] for h in range(HPG)], axis=0)
            o_ref[:, j * HPG * HEAD_DIM:(j + 1) * HPG * HEAD_DIM] = yt.T

        return (m_s, acc_s), nsup_ref[g * n_qb + qb], update, finish

    groups = [make_group(j) for j in range(gp)]
    n_iter = groups[0][1]
    for grp in groups[1:]:
        n_iter = jnp.maximum(n_iter, grp[1])
    carries = lax.fori_loop(0, n_iter, lambda si, cs: tuple(grp[2](si, c) for grp, c in zip(groups, cs)),
                            tuple(grp[0] for grp in groups))
    for grp, c in zip(groups, carries):
        grp[3](c)


def sel_win_attention(tile_ids, n_sup, feat_t, selb, oc_t, slope_rows, gates_t, gain_t, S):
    G = NSA_KV_HEADS
    gp = 2
    n_qb = S // Q_BLOCK
    n_blk = S // SEL_BLOCK
    nt = S // KT
    kv_rows = gp * HEAD_DIM
    kv_spec = lambda base: pl.BlockSpec((kv_rows, S), lambda g, i, ids, ns: (base // kv_rows + g, 0))
    grid_spec = pltpu.PrefetchScalarGridSpec(
        num_scalar_prefetch=2,
        grid=(G // gp, n_qb),
        in_specs=[pl.BlockSpec((gp * HPG * HEAD_DIM, Q_BLOCK), lambda g, i, ids, ns: (g, i)),
                  kv_spec(FT_KS), kv_spec(FT_VS), kv_spec(FT_KW), kv_spec(FT_VW),
                  pl.BlockSpec((gp, 1, n_blk + 8, Q_BLOCK), lambda g, i, ids, ns: (g, i, 0, 0)),
                  pl.BlockSpec((gp, 1, HEAD_DIM, QL), lambda g, i, ids, ns: (g, i, 0, 0)),
                  pl.BlockSpec((gp, 8, QL), lambda g, i, ids, ns: (g, 0, 0)),
                  pl.BlockSpec((gp, 1, 3, QL), lambda g, i, ids, ns: (g, i, 0, 0)),
                  pl.BlockSpec((gp, HEAD_DIM, QL), lambda g, i, ids, ns: (g, 0, 0))],
        out_specs=pl.BlockSpec((Q_BLOCK, gp * HPG * HEAD_DIM), lambda g, i, ids, ns: (i, g)),
    )
    return pl.pallas_call(
        functools.partial(_sel_win_kernel, nt=nt, gp=gp),
        out_shape=jax.ShapeDtypeStruct((S, NSA_WIDTH), F32),
        grid_spec=grid_spec,
        compiler_params=_cparams(("parallel", "parallel")),
        name="nsa_sel_win",
    )(tile_ids, n_sup, feat_t, feat_t, feat_t, feat_t, feat_t, selb, oc_t, slope_rows, gates_t, gain_t)


def _conv_kernel(*refs, width, glu, post):
    if post == "ln_silu":
        x_ref, halo_ref, w_ref, b_ref, lg_ref, lb_ref, o_ref, u_ref = refs
    else:
        x_ref, halo_ref, w_ref, b_ref, o_ref, u_ref = refs
    i = pl.program_id(0)
    T = o_ref.shape[0]
    H = halo_ref.shape[0]
    C = o_ref.shape[1]

    def pre(v):
        return v[:, :C] * _sigmoid(v[:, C:]) if glu else v

    u_ref[0:H, :] = jnp.where(i > 0, pre(halo_ref[...]), 0.0)
    u_ref[H:H + T, :] = pre(x_ref[...])
    acc = jnp.broadcast_to(b_ref[...], (T, C))
    base = H - (width - 1)
    for k in range(width):
        acc = acc + w_ref[k:k + 1, :] * u_ref[base + k:base + k + T, :]
    if post == "ln_silu":
        mu = jnp.mean(acc, axis=-1, keepdims=True)
        xc = acc - mu
        var = jnp.mean(xc * xc, axis=-1, keepdims=True)
        acc = xc * lax.rsqrt(var + LN_EPS) * lg_ref[...] + lb_ref[...]
    o_ref[...] = acc * _sigmoid(acc)


def causal_conv(x, col_block, w, b, ln=None, *, glu, T):
    S = x.shape[0]
    width, C = w.shape
    cin = 2 * C if glu else C
    H = -(-(width - 1) // 8) * 8
    post = "ln_silu" if ln is not None else "silu"
    in_specs = [pl.BlockSpec((T, cin), lambda i: (i, col_block)),
                pl.BlockSpec((H, cin), lambda i: (jnp.maximum(i * (T // H) - 1, 0), col_block)),
                pl.BlockSpec((width, C), lambda i: (0, 0)),
                pl.BlockSpec((1, C), lambda i: (0, 0))]
    args = [x, x, w, b]
    if ln is not None:
        in_specs += [pl.BlockSpec((1, C), lambda i: (0, 0))] * 2
        args += list(ln)
    return pl.pallas_call(
        functools.partial(_conv_kernel, width=width, glu=glu, post=post),
        out_shape=jax.ShapeDtypeStruct((S, C), F32),
        grid=(S // T,),
        in_specs=in_specs,
        out_specs=pl.BlockSpec((T, C), lambda i: (i, 0)),
        scratch_shapes=[pltpu.VMEM((H + T, C), F32)],
        compiler_params=_cparams(("parallel",)),
        name="causal_conv_glu" if glu else "causal_conv",
    )(*args)


def _log_sigmoid(x):
    return jnp.minimum(x, 0.0) - jnp.log(1.0 + jnp.exp(-jnp.abs(x)))


def _mlstm_kernel(qk_ref, v_ref, o_ref, sm_ref, gt_ref, bcol_ref, brow_ref, gain_ref, out_ref, c_ref, m_ref):
    L = qk_ref.shape[0]
    DH = MLSTM_HEAD_DIM
    W = MLSTM_WIDTH

    @pl.when(pl.program_id(0) == 0)
    def _():
        c_ref[...] = jnp.zeros_like(c_ref)
        m_ref[...] = jnp.zeros_like(m_ref)

    small = sm_ref[...]
    lane = lax.broadcasted_iota(jnp.int32, small.shape, 1)
    t_io = lax.broadcasted_iota(jnp.int32, (L, L), 0)
    s_io = lax.broadcasted_iota(jnp.int32, (L, L), 1)
    causal = s_io <= t_io
    ones_col = (lax.broadcasted_iota(jnp.int32, (L, DH), 1) == 0).astype(BF16)

    def column(c):
        return jnp.sum(jnp.where(lane == c, small, 0.0), axis=1, keepdims=True)

    for h in range(MLSTM_HEADS):
        q = qk_ref[:, h * DH:(h + 1) * DH].astype(BF16)
        k = qk_ref[:, W + h * DH:W + (h + 1) * DH] * (DH ** -0.5)
        v = v_ref[:, h * DH:(h + 1) * DH].astype(BF16)
        vaug = jnp.concatenate([v, ones_col], axis=1)
        i_col = column(SM_I + h) + bcol_ref[h:h + 1, :]
        f_col = column(SM_F + h) + bcol_ref[MLSTM_HEADS + h:MLSTM_HEADS + h + 1, :]
        i_row = gt_ref[h:h + 1, :] + brow_ref[h:h + 1, :]
        f_row = gt_ref[MLSTM_HEADS + h:MLSTM_HEADS + h + 1, :] + brow_ref[MLSTM_HEADS + h:MLSTM_HEADS + h + 1, :]
        lf_col = _log_sigmoid(f_col)
        lf_row = _log_sigmoid(f_row)
        b_col = jnp.sum(jnp.where(causal, lf_row, 0.0), axis=1, keepdims=True)
        b_row = jnp.sum(jnp.where(t_io <= s_io, lf_col, 0.0), axis=0, keepdims=True)
        b_end = jnp.sum(lf_row, axis=1, keepdims=True)
        m0 = m_ref[h:h + 1, 0:1]
        c0 = c_ref[h]

        dmat = jnp.where(causal, b_col - b_row + i_row, -jnp.inf)
        inter = b_col + m0
        m_t = jnp.maximum(inter, jnp.max(dmat, axis=1, keepdims=True))
        qk = lax.dot_general(q, k.astype(BF16), (((1,), (1,)), ((), ())), preferred_element_type=F32)
        smat = qk * jnp.exp(dmat - m_t)
        w_inter = jnp.exp(inter - m_t)
        r = (jnp.dot(smat.astype(BF16), vaug, preferred_element_type=F32)
             + w_inter * jnp.dot(q, c0.astype(BF16), preferred_element_type=F32))
        num = r[:, :DH]
        den = jnp.sum(r[:, DH:], axis=1, keepdims=True)
        hh = num / jnp.maximum(jnp.abs(den), jnp.exp(-m_t))

        a_row = b_end - b_row + i_row
        a_col = b_end - b_col + i_col
        m_new = jnp.maximum(b_end + m0, jnp.max(a_row, axis=1, keepdims=True))
        kw = (k * jnp.exp(a_col - m_new)).astype(BF16)
        c_ref[h] = jnp.exp(b_end + m0 - m_new) * c0 + lax.dot_general(
            kw, vaug, (((0,), (0,)), ((), ())), preferred_element_type=F32)
        m_ref[h:h + 1, :] = jnp.broadcast_to(m_new, (1, m_ref.shape[1]))

        y = _sigmoid(o_ref[:, h * DH:(h + 1) * DH]) * hh
        ms = jnp.mean(y * y, axis=-1, keepdims=True)
        out_ref[:, h * DH:(h + 1) * DH] = y * lax.rsqrt(ms + NORM_EPS) * gain_ref[:, h * DH:(h + 1) * DH]


def mlstm(qk, rm, gates_t, bias_col, bias_row, gain, S):
    L = min(MLSTM_CHUNK, S)
    W = MLSTM_WIDTH
    return pl.pallas_call(
        _mlstm_kernel,
        out_shape=jax.ShapeDtypeStruct((S, W), F32),
        grid=(S // L,),
        in_specs=[pl.BlockSpec((L, 2 * W), lambda c: (c, 0)),
                  pl.BlockSpec((L, W), lambda c: (c, RM_V // W)),
                  pl.BlockSpec((L, W), lambda c: (c, RM_O // W)),
                  pl.BlockSpec((L, 128), lambda c: (c, RM_SMALL // 128)),
                  pl.BlockSpec((2 * MLSTM_HEADS, L), lambda c: (0, c)),
                  pl.BlockSpec((2 * MLSTM_HEADS, 1), lambda c: (0, 0)),
                  pl.BlockSpec((2 * MLSTM_HEADS, 1), lambda c: (0, 0)),
                  pl.BlockSpec((1, W), lambda c: (0, 0))],
        out_specs=pl.BlockSpec((L, W), lambda c: (c, 0)),
        scratch_shapes=[pltpu.VMEM((MLSTM_HEADS, MLSTM_HEAD_DIM, 2 * MLSTM_HEAD_DIM), F32),
                        pltpu.VMEM((8, 128), F32)],
        compiler_params=_cparams(("arbitrary",)),
        name="mlstm",
    )(qk, rm, rm, rm, gates_t, bias_col, bias_row, gain)


def _router_kernel(x_ref, g_ref, wr_ref, br_ref, xn_ref, eid_ref, gate_ref):
    xn = _rms_rows(x_ref[...], g_ref[...])
    xn_ref[...] = xn.astype(xn_ref.dtype)
    logits = lax.dot_general(wr_ref[...], xn, (((1,), (1,)), ((), ())), precision=lax.Precision.HIGHEST,
                             preferred_element_type=F32) + br_ref[...]
    tm = logits.shape[1]
    lg = logits[N_EXPERTS:N_EXPERTS + N_GROUPS, :]
    eg = jnp.exp(lg - jnp.max(lg, axis=0, keepdims=True))
    pg = eg / jnp.sum(eg, axis=0, keepdims=True)
    pg_top = jnp.max(pg, axis=0, keepdims=True)
    g_io = lax.broadcasted_iota(jnp.int32, (N_GROUPS, tm), 0)
    grp = jnp.min(jnp.where(pg == pg_top, g_io, N_GROUPS), axis=0, keepdims=True)
    le = logits[0:EXPERTS_PER_GROUP, :]
    for g in range(1, N_GROUPS):
        le = jnp.where(grp == g, logits[g * EXPERTS_PER_GROUP:(g + 1) * EXPERTS_PER_GROUP, :], le)
    ee = jnp.exp(le - jnp.max(le, axis=0, keepdims=True))
    pe = ee / jnp.sum(ee, axis=0, keepdims=True)
    e_io = lax.broadcasted_iota(jnp.int32, (EXPERTS_PER_GROUP, tm), 0)
    p1 = jnp.max(pe, axis=0, keepdims=True)
    i1 = jnp.min(jnp.where(pe == p1, e_io, EXPERTS_PER_GROUP), axis=0, keepdims=True)
    pe2 = jnp.where(e_io == i1, -1.0, pe)
    p2 = jnp.max(pe2, axis=0, keepdims=True)
    i2 = jnp.min(jnp.where(pe2 == p2, e_io, EXPERTS_PER_GROUP), axis=0, keepdims=True)
    tot = p1 + p2
    eid_ref[...] = jnp.concatenate([grp * EXPERTS_PER_GROUP + i1, grp * EXPERTS_PER_GROUP + i2], axis=0)
    gate_ref[...] = jnp.concatenate([pg_top * p1 / tot, pg_top * p2 / tot], axis=0)


def router(x, g, wr_t, br_t, tm):
    S, D = x.shape
    R = wr_t.shape[0]
    return pl.pallas_call(
        _router_kernel,
        out_shape=(jax.ShapeDtypeStruct((S, D), F32),
                   jax.ShapeDtypeStruct((2, S), jnp.int32),
                   jax.ShapeDtypeStruct((2, S), F32)),
        grid=(S // tm,),
        in_specs=[pl.BlockSpec((tm, D), lambda i: (i, 0)),
                  pl.BlockSpec((1, D), lambda i: (0, 0)),
                  pl.BlockSpec((R, D), lambda i: (0, 0)),
                  pl.BlockSpec((R, 1), lambda i: (0, 0))],
        out_specs=(pl.BlockSpec((tm, D), lambda i: (i, 0)),
                   pl.BlockSpec((2, tm), lambda i: (0, i)),
                   pl.BlockSpec((2, tm), lambda i: (0, i))),
        compiler_params=_cparams(("parallel",)),
        name="moe_router",
    )(x, g, wr_t, br_t)


def _row_gather(idx_ref, idx0, stride, src_hbm, dst, sem, n_rows):
    def body(r, carry):
        i = idx_ref[idx0 + r * stride]
        pltpu.make_async_copy(src_hbm.at[pl.ds(i, 1), :], dst.at[pl.ds(r, 1), :], sem).start()
        return carry

    lax.fori_loop(0, n_rows, body, 0, unroll=8)


def _rows_wait(src_hbm, dst, sem):
    pltpu.make_async_copy(src_hbm.at[pl.ds(0, dst.shape[0]), :], dst, sem).wait()


def _expert_kernel(be_ref, nu_ref, tok_ref, xn_hbm, wg_ref, wu_ref, wd_ref, o_ref,
                   xbuf, sem, wg_s, wu_s, wd_s):
    b = pl.program_id(0)
    n_used = nu_ref[0]
    e = be_ref[b]
    prev = be_ref[jnp.maximum(b - 1, 0)]

    def gather(blk, slot):
        _row_gather(tok_ref, blk * MOE_ROWS, 1, xn_hbm, xbuf.at[slot], sem.at[slot], MOE_ROWS)

    @pl.when(b == 0)
    def _():
        gather(0, 0)

    @pl.when(b + 1 < n_used)
    def _():
        gather(b + 1, (b + 1) % 2)

    @pl.when((b == 0) | (e != prev))
    def _():
        wg_s[...] = wg_ref[0, 0].astype(BF16)
        wu_s[...] = wu_ref[0, 0].astype(BF16)
        wd_s[...] = wd_ref[0, 0].astype(BF16)

    @pl.when(b < n_used)
    def _():
        slot = b % 2
        _rows_wait(xn_hbm, xbuf.at[slot], sem.at[slot])
        x = xbuf[slot].astype(BF16)
        hg = jnp.dot(x, wg_s[...], preferred_element_type=F32)
        hu = jnp.dot(x, wu_s[...], preferred_element_type=F32)
        hb = (hg * _sigmoid(hg) * hu).astype(BF16)
        o_ref[...] = jnp.dot(hb, wd_s[...], preferred_element_type=F32)

    @pl.when(b >= n_used)
    def _():
        o_ref[...] = jnp.zeros_like(o_ref)


def expert_ffn(blk_e, n_used, row_tok, xn, w_gate, w_up, w_down, layer):
    P = row_tok.shape[0]
    D = xn.shape[1]
    n_blocks = P // MOE_ROWS
    Hd = w_gate.shape[3]
    grid_spec = pltpu.PrefetchScalarGridSpec(
        num_scalar_prefetch=3,
        grid=(n_blocks,),
        in_specs=[pl.BlockSpec(memory_space=pl.ANY),
                  pl.BlockSpec((1, 1, D, Hd), lambda b, be, nu, tk: (layer, be[b], 0, 0)),
                  pl.BlockSpec((1, 1, D, Hd), lambda b, be, nu, tk: (layer, be[b], 0, 0)),
                  pl.BlockSpec((1, 1, Hd, D), lambda b, be, nu, tk: (layer, be[b], 0, 0))],
        out_specs=pl.BlockSpec((MOE_ROWS, D), lambda b, be, nu, tk: (b, 0)),
        scratch_shapes=[pltpu.VMEM((2, MOE_ROWS, D), F32), pltpu.SemaphoreType.DMA((2,)),
                        pltpu.VMEM((D, Hd), BF16), pltpu.VMEM((D, Hd), BF16), pltpu.VMEM((Hd, D), BF16)],
    )
    return pl.pallas_call(
        _expert_kernel,
        out_shape=jax.ShapeDtypeStruct((P, D), F32),
        grid_spec=grid_spec,
        compiler_params=_cparams(("arbitrary",)),
        name="moe_experts",
    )(blk_e, n_used, row_tok, xn, w_gate, w_up, w_down)


def _combine_kernel(pos_ref, x_ref, gate_ref, yb_hbm, o_ref, ybuf, sem):
    i = pl.program_id(0)
    n = pl.num_programs(0)
    tt = x_ref.shape[0]

    def gather(blk, slot):
        for k in range(2):
            _row_gather(pos_ref, blk * tt * 2 + k, 2, yb_hbm, ybuf.at[slot, k], sem.at[slot], tt)

    @pl.when(i == 0)
    def _():
        gather(0, 0)

    @pl.when(i + 1 < n)
    def _():
        gather(i + 1, (i + 1) % 2)

    slot = i % 2
    for k in range(2):
        _rows_wait(yb_hbm, ybuf.at[slot, k], sem.at[slot])
    gate = gate_ref[...]
    o_ref[...] = x_ref[...] + gate[:, 0:1] * ybuf[slot, 0] + gate[:, 1:2] * ybuf[slot, 1]


def moe_combine(pos, x, gate, yb, tt):
    S, D = x.shape
    grid_spec = pltpu.PrefetchScalarGridSpec(
        num_scalar_prefetch=1,
        grid=(S // tt,),
        in_specs=[pl.BlockSpec((tt, D), lambda i, pos: (i, 0)),
                  pl.BlockSpec((tt, 2), lambda i, pos: (i, 0)),
                  pl.BlockSpec(memory_space=pl.ANY)],
        out_specs=pl.BlockSpec((tt, D), lambda i, pos: (i, 0)),
        scratch_shapes=[pltpu.VMEM((2, 2, tt, D), F32), pltpu.SemaphoreType.DMA((2,))],
    )
    return pl.pallas_call(
        _combine_kernel,
        out_shape=jax.ShapeDtypeStruct((S, D), F32),
        grid_spec=grid_spec,
        compiler_params=_cparams(("arbitrary",)),
        name="moe_combine",
    )(pos, x, gate, yb)


def _final_norm_kernel(x_ref, g_ref, o_ref):
    o_ref[...] = _rms_rows(x_ref[...], g_ref[...])


def final_norm(x, g, tm):
    S, D = x.shape
    return pl.pallas_call(
        _final_norm_kernel,
        out_shape=jax.ShapeDtypeStruct((S, D), F32),
        grid=(S // tm,),
        in_specs=[pl.BlockSpec((tm, D), lambda i: (i, 0)), pl.BlockSpec((1, D), lambda i: (0, 0))],
        out_specs=pl.BlockSpec((tm, D), lambda i: (i, 0)),
        compiler_params=_cparams(("parallel",)),
        name="final_norm",
    )(x, g)


LOG2E = float(np.log2(np.e))


def _alibi_lanes():
    sl = 2.0 ** (-8.0 * np.arange(1, NSA_HEADS + 1) / NSA_HEADS) * LOG2E
    sl = np.repeat(sl.reshape(NSA_KV_HEADS, HPG, 1), Q_BLOCK, axis=2).reshape(NSA_KV_HEADS, 1, QL)
    sl = jnp.asarray(sl, F32)
    s1 = sl.astype(BF16).astype(F32)
    s2 = (sl - s1).astype(BF16).astype(F32)
    s3 = (sl - s1 - s2).astype(BF16).astype(F32)
    zero = jnp.zeros_like(sl)
    return jnp.concatenate([s1, s2, s3, s1, s2, s3, zero, zero], axis=1)


def _active_tiles(cnt, S):
    G, n_qb, n_blk = cnt.shape
    nt = S // KT
    act = (cnt > 0.5).reshape(G, n_qb, nt, KT // SEL_BLOCK).any(-1)
    tile = jnp.arange(nt, dtype=jnp.int32)
    act = act & (tile[None, None, :] < (jnp.arange(n_qb) * (Q_BLOCK // KT))[None, :, None])
    rank = jnp.cumsum(act.astype(jnp.int32), axis=-1) - 1
    n_act = rank[..., -1] + 1
    hit = act[..., None, :] & (rank[..., None, :] == tile[None, None, :, None])
    ids = jnp.sum(jnp.where(hit, tile[None, None, None, :], 0), axis=-1)
    ids = jnp.where(tile[None, None, :] < n_act[..., None], ids, nt).astype(jnp.int32)
    return ids.reshape(-1), ((n_act + SUP - 1) // SUP).astype(jnp.int32).reshape(-1)


def _overlap_t(n_cmp_pad, n_cmp, n_blk):
    cs = np.arange(n_cmp_pad) * CMP_STRIDE
    ss = np.arange(n_blk) * SEL_BLOCK
    ov = np.minimum(cs[None, :] + CMP_BLOCK, ss[:, None] + SEL_BLOCK) - np.maximum(cs[None, :], ss[:, None])
    ov = np.clip(ov, 0, None) / CMP_STRIDE
    ov[:, n_cmp:] = 0.0
    return jnp.asarray(ov, BF16)


def mixer(x, p, S):
    G = NSA_KV_HEADS
    tm = min(1024, S)
    w_in = p["w_in"]
    c = np.cumsum((0, NSA_WIDTH) + (NSA_KV_WIDTH,) * 6 + (3 * NSA_HEADS,) + (MLSTM_WIDTH,) * 4
                  + (MLSTM_HEADS,) * 2 + (2 * CONV_CHANNELS,))
    (q0, kc0, vc0, ks0, vs0, kw0, vw0, gt0, mq0, mk0, mv0, mo0, mi0, mf0, cu0, end) = [int(v) for v in c]
    w_rm = jnp.concatenate(
        [w_in[:, mq0:mv0], w_in[:, cu0:end], w_in[:, mv0:mi0], w_in[:, kc0:ks0], w_in[:, gt0:mq0],
         w_in[:, mi0:cu0], jnp.zeros((D_MODEL, RM_WIDTH - RM_SMALL - 56), F32)], axis=1).astype(BF16)
    w_ft = jnp.concatenate([w_in[:, q0:kc0], w_in[:, ks0:gt0]], axis=1).T.astype(BF16)
    g = p["attn_norm_g"][None, :]
    rm = norm_matmul(x, g, w_rm, tm, 768)
    q_scale = jnp.where(jnp.arange(FT_WIDTH) < FT_KS, HEAD_DIM ** -0.5 * LOG2E, 1.0).astype(F32)[:, None]
    feat_t = norm_matmul_t(x, g, w_ft, q_scale, tm, 512)

    ncp = S // CMP_STRIDE
    n_cmp = (S - CMP_BLOCK) // CMP_STRIDE + 1
    n_blk = S // SEL_BLOCK
    n_qb = S // Q_BLOCK
    kv = jnp.stack([rm[:, RM_KC:RM_KC + NSA_KV_WIDTH], rm[:, RM_VC:RM_VC + NSA_KV_WIDTH]])
    rows = kv.reshape(2, ncp, CMP_STRIDE, G, HEAD_DIM).transpose(0, 3, 1, 2, 4).reshape(2, G, ncp, -1)
    nxt = jnp.concatenate([rows[:, :, 1:], jnp.zeros_like(rows[:, :, :1])], axis=2)
    blk = jnp.concatenate([rows, nxt], axis=-1)
    pos = jnp.stack([p["cmp_pos_k"].reshape(1, -1), p["cmp_pos_v"].reshape(1, -1)])
    w1 = jnp.stack([p["cmp_w1_k"], p["cmp_w1_v"]]).astype(BF16)
    w2t = jnp.stack([p["cmp_w2_k"].T, p["cmp_w2_v"].T]).astype(BF16)
    kvc_t = compress(blk, pos, w1, w2t)
    slope_rows = _alibi_lanes()
    oc_t, selb, cnt = cmp_attention(feat_t, kvc_t[0], kvc_t[1], slope_rows, _overlap_t(ncp, n_cmp, n_blk), S)
    tile_ids, n_sup = _active_tiles(cnt[:, :, 0, :], S)
    small = rm[:, RM_SMALL:RM_SMALL + 128]
    gates_t = small[:, SM_GATES:SM_GATES + 3 * NSA_HEADS].reshape(n_qb, Q_BLOCK, G, HPG, 3)
    gates_t = gates_t.transpose(2, 0, 4, 3, 1).reshape(G, n_qb, 3, QL)
    gain_t = jnp.broadcast_to(p["nsa_out_g"].reshape(G, HPG, HEAD_DIM, 1).transpose(0, 2, 1, 3),
                              (G, HEAD_DIM, HPG, Q_BLOCK)).reshape(G, HEAD_DIM, QL)
    a_out = sel_win_attention(tile_ids, n_sup, feat_t, selb, oc_t, slope_rows, gates_t, gain_t, S)

    tc = min(512, S)
    qk = causal_conv(rm, RM_QK // (2 * MLSTM_WIDTH), p["mlstm_conv_w"], p["mlstm_conv_b"][None, :],
                     glu=False, T=tc)
    gates_m = small[:, SM_I:SM_I + 2 * MLSTM_HEADS].T
    bias = jnp.concatenate([p["mlstm_i_bias"], p["mlstm_f_bias"]])
    hm = mlstm(qk, rm, gates_m, bias[:, None], bias[:, None], p["mlstm_out_g"][None, :], S)

    cv = causal_conv(rm, RM_CU // (2 * CONV_CHANNELS), p["conv_w"], p["conv_b"][None, :],
                     (p["conv_ln_g"][None, :], p["conv_ln_b"][None, :]), glu=True, T=tc)

    return out_proj(a_out, hm, cv, x, p["w_out"].astype(BF16), min(512, S), 1024)


def moe(x, p, stacked_w, layer, S):
    tm = min(512, S)
    wr_t = jnp.concatenate([p["router_w_expert"].T, p["router_w_group"].T,
                            jnp.zeros((12, D_MODEL), F32)], axis=0)
    br_t = jnp.concatenate([p["router_b_expert"].reshape(-1), p["router_b_group"], jnp.zeros((12,), F32)])[:, None]
    xn, eid, gate = router(x, p["ffn_norm_g"][None, :], wr_t, br_t, tm)

    flat_e = eid.T.reshape(-1)
    A = flat_e.shape[0]
    onehot = (flat_e[:, None] == jnp.arange(N_EXPERTS)[None, :]).astype(jnp.int32)
    rank = jnp.take_along_axis(jnp.cumsum(onehot, axis=0), flat_e[:, None], axis=1)[:, 0] - 1
    counts = jnp.sum(onehot, axis=0)
    padded = (counts + MOE_ROWS - 1) // MOE_ROWS * MOE_ROWS
    pends = jnp.cumsum(padded)
    dest = (pends - padded)[flat_e] + rank
    n_blocks = A // MOE_ROWS + N_EXPERTS
    P = n_blocks * MOE_ROWS
    row_tok = jnp.zeros((P,), jnp.int32).at[dest].set(jnp.arange(A, dtype=jnp.int32) // 2, unique_indices=True)
    n_used = (pends[-1] // MOE_ROWS).astype(jnp.int32)
    blk_start = jnp.minimum(jnp.arange(n_blocks), n_used - 1) * MOE_ROWS
    blk_e = jnp.minimum(jnp.sum(pends[None, :] <= blk_start[:, None], axis=1), N_EXPERTS - 1).astype(jnp.int32)

    yb = expert_ffn(blk_e, n_used[None], row_tok, xn, stacked_w["expert_w_gate"], stacked_w["expert_w_up"],
                    stacked_w["expert_w_down"], layer)
    return moe_combine(dest.astype(jnp.int32), x, gate.T, yb, min(256, S))


_LAYER_KEYS = ("attn_norm_g", "w_in", "cmp_pos_k", "cmp_w1_k", "cmp_w2_k", "cmp_pos_v", "cmp_w1_v", "cmp_w2_v",
               "nsa_out_g", "mlstm_conv_w", "mlstm_conv_b", "mlstm_i_bias", "mlstm_f_bias", "mlstm_out_g",
               "conv_w", "conv_b", "conv_ln_g", "conv_ln_b", "w_out", "ffn_norm_g", "router_w_group",
               "router_b_group", "router_w_expert", "router_b_expert", "expert_w_gate", "expert_w_up",
               "expert_w_down")


def kernel(x, attn_norm_g, w_in, cmp_pos_k, cmp_w1_k, cmp_w2_k, cmp_pos_v, cmp_w1_v, cmp_w2_v, nsa_out_g, mlstm_conv_w, mlstm_conv_b, mlstm_i_bias, mlstm_f_bias, mlstm_out_g, conv_w, conv_b, conv_ln_g, conv_ln_b, w_out, ffn_norm_g, router_w_group, router_b_group, router_w_expert, router_b_expert, expert_w_gate, expert_w_up, expert_w_down, final_norm_g):
    stacked = (attn_norm_g, w_in, cmp_pos_k, cmp_w1_k, cmp_w2_k, cmp_pos_v, cmp_w1_v, cmp_w2_v, nsa_out_g,
               mlstm_conv_w, mlstm_conv_b, mlstm_i_bias, mlstm_f_bias, mlstm_out_g, conv_w, conv_b, conv_ln_g,
               conv_ln_b, w_out, ffn_norm_g, router_w_group, router_b_group, router_w_expert, router_b_expert,
               expert_w_gate, expert_w_up, expert_w_down)
    B, S, D = x.shape
    assert B == 1 and D == D_MODEL and S % 1024 == 0
    h = x.reshape(S, D)
    stacked = dict(zip(_LAYER_KEYS, stacked))
    for l in range(attn_norm_g.shape[0]):
        p = {k: v[l] for k, v in stacked.items() if not k.startswith("expert_w")}
        h = mixer(h, p, S)
        h = moe(h, p, stacked, l, S)
    return final_norm(h, final_norm_g[None, :], min(512, S)).reshape(B, S, D)
```

```python
import functools

import numpy as np
import jax
import jax.numpy as jnp
from jax import lax
from jax.experimental import pallas as pl
from jax.experimental.pallas import tpu as pltpu

F32 = jnp.float32
BF16 = jnp.bfloat16

D_MODEL = 2048
HEAD_DIM = 64
NSA_WIDTH = D_MODEL // 2
NSA_HEADS = NSA_WIDTH // HEAD_DIM
NSA_KV_HEADS = NSA_HEADS // 4
HPG = NSA_HEADS // NSA_KV_HEADS
NSA_KV_WIDTH = NSA_KV_HEADS * HEAD_DIM
CMP_BLOCK = 32
CMP_STRIDE = 16
CMP_HIDDEN = 4 * HEAD_DIM
SEL_BLOCK = 64
SEL_TOPK = 16
WINDOW = 512
Q_BLOCK = 128
SEL_FORCE = 1.0e4
MLSTM_WIDTH = D_MODEL // 4
MLSTM_HEADS = 4
MLSTM_HEAD_DIM = MLSTM_WIDTH // MLSTM_HEADS
MLSTM_CONV = 4
CONV_CHANNELS = D_MODEL // 4
CONV_WIDTH = 31
N_GROUPS = 4
EXPERTS_PER_GROUP = 8
N_EXPERTS = N_GROUPS * EXPERTS_PER_GROUP
EXPERT_HIDDEN = D_MODEL // 4
NORM_EPS = 1e-6
LN_EPS = 1e-5

NEG = -1.0e30
QL = HPG * Q_BLOCK
MLSTM_CHUNK = 256
MOE_ROWS = 256
VMEM_LIMIT = 52 * 1024 * 1024

RM_QK, RM_CU, RM_V, RM_O, RM_KC, RM_VC, RM_SMALL = 0, 1024, 2048, 2560, 3072, 3328, 3584
RM_WIDTH = 3840
SM_GATES, SM_I, SM_F = 0, 48, 52
FT_Q, FT_KS, FT_VS, FT_KW, FT_VW = 0, 1024, 1280, 1536, 1792
FT_WIDTH = 2048


def _cparams(sem, vmem=VMEM_LIMIT):
    return pltpu.CompilerParams(dimension_semantics=sem, vmem_limit_bytes=vmem)


def _sigmoid(x):
    return 1.0 / (1.0 + jnp.exp(-x))


def _rms_rows(x, g):
    ms = jnp.mean(x * x, axis=-1, keepdims=True)
    return x * lax.rsqrt(ms + NORM_EPS) * g


def _norm_mm_kernel(x_ref, g_ref, w_ref, o_ref, h_ref):
    @pl.when(pl.program_id(1) == 0)
    def _():
        h_ref[...] = _rms_rows(x_ref[...], g_ref[...]).astype(BF16)

    o_ref[...] = jnp.dot(h_ref[...], w_ref[...], preferred_element_type=F32).astype(o_ref.dtype)


def norm_matmul(x, g, w, tm, tn):
    M, K = x.shape
    N = w.shape[1]
    return pl.pallas_call(
        _norm_mm_kernel,
        out_shape=jax.ShapeDtypeStruct((M, N), F32),
        grid=(M // tm, N // tn),
        in_specs=[pl.BlockSpec((tm, K), lambda i, j: (i, 0)),
                  pl.BlockSpec((1, K), lambda i, j: (0, 0)),
                  pl.BlockSpec((K, tn), lambda i, j: (0, j))],
        out_specs=pl.BlockSpec((tm, tn), lambda i, j: (i, j)),
        scratch_shapes=[pltpu.VMEM((tm, K), BF16)],
        compiler_params=_cparams(("parallel", "arbitrary")),
        name="norm_matmul",
    )(x, g, w)


def _norm_mm_t_kernel(x_ref, g_ref, wt_ref, sc_ref, o_ref, h_ref):
    @pl.when(pl.program_id(1) == 0)
    def _():
        h_ref[...] = _rms_rows(x_ref[...], g_ref[...]).astype(BF16)

    o = lax.dot_general(wt_ref[...], h_ref[...], (((1,), (1,)), ((), ())), preferred_element_type=F32)
    o_ref[...] = (o * sc_ref[...]).astype(o_ref.dtype)


def norm_matmul_t(x, g, wt, row_scale, tm, tn):
    M, K = x.shape
    N = wt.shape[0]
    return pl.pallas_call(
        _norm_mm_t_kernel,
        out_shape=jax.ShapeDtypeStruct((N, M), BF16),
        grid=(M // tm, N // tn),
        in_specs=[pl.BlockSpec((tm, K), lambda i, j: (i, 0)),
                  pl.BlockSpec((1, K), lambda i, j: (0, 0)),
                  pl.BlockSpec((tn, K), lambda i, j: (j, 0)),
                  pl.BlockSpec((tn, 1), lambda i, j: (j, 0))],
        out_specs=pl.BlockSpec((tn, tm), lambda i, j: (j, i)),
        scratch_shapes=[pltpu.VMEM((tm, K), BF16)],
        compiler_params=_cparams(("parallel", "arbitrary")),
        name="norm_matmul_t",
    )(x, g, wt, row_scale)


def _out_mm_kernel(a_ref, m_ref, c_ref, x_ref, w_ref, o_ref):
    h = jnp.concatenate([a_ref[...], m_ref[...], c_ref[...]], axis=-1).astype(BF16)
    o_ref[...] = x_ref[...] + jnp.dot(h, w_ref[...], preferred_element_type=F32)


def out_proj(a, m, c, x, w, tm, tn):
    M = x.shape[0]
    N = w.shape[1]
    return pl.pallas_call(
        _out_mm_kernel,
        out_shape=jax.ShapeDtypeStruct((M, N), F32),
        grid=(N // tn, M // tm),
        in_specs=[pl.BlockSpec((tm, a.shape[1]), lambda j, i: (i, 0)),
                  pl.BlockSpec((tm, m.shape[1]), lambda j, i: (i, 0)),
                  pl.BlockSpec((tm, c.shape[1]), lambda j, i: (i, 0)),
                  pl.BlockSpec((tm, tn), lambda j, i: (i, j)),
                  pl.BlockSpec((w.shape[0], tn), lambda j, i: (0, j))],
        out_specs=pl.BlockSpec((tm, tn), lambda j, i: (i, j)),
        compiler_params=_cparams(("parallel", "parallel")),
        name="out_proj",
    )(a, m, c, x, w)


def _compress_kernel(blk_ref, pos_ref, w1_ref, w2t_ref, o_ref):
    x = (blk_ref[0, 0] + pos_ref[0]).astype(BF16)
    hid = jnp.dot(x, w1_ref[0], preferred_element_type=F32)
    hid = hid * _sigmoid(hid)
    o_ref[0, 0] = lax.dot_general(w2t_ref[0], hid.astype(BF16), (((1,), (1,)), ((), ())),
                                  preferred_element_type=F32).astype(o_ref.dtype)


def compress(blk, pos, w1, w2t):
    _, G, NCP, LD = blk.shape
    return pl.pallas_call(
        _compress_kernel,
        out_shape=jax.ShapeDtypeStruct((2, G, HEAD_DIM, NCP), BF16),
        grid=(2, G),
        in_specs=[pl.BlockSpec((1, 1, NCP, LD), lambda a, g: (a, g, 0, 0)),
                  pl.BlockSpec((1, 1, LD), lambda a, g: (a, 0, 0)),
                  pl.BlockSpec((1, LD, CMP_HIDDEN), lambda a, g: (a, 0, 0)),
                  pl.BlockSpec((1, HEAD_DIM, CMP_HIDDEN), lambda a, g: (a, 0, 0))],
        out_specs=pl.BlockSpec((1, 1, HEAD_DIM, NCP), lambda a, g: (a, g, 0, 0)),
        compiler_params=_cparams(("parallel", "parallel")),
        name="nsa_compress",
    )(blk, pos, w1, w2t)


def _tdot(a, b):
    return lax.dot_general(a, b, (((0,), (0,)), ((), ())), preferred_element_type=F32)


def _cmp_attn_kernel(q_ref, kct_ref, vct_ref, sl_ref, ov_ref, oc_ref, sel_ref, cnt_ref, imp_ref, *, n_sel, cch):
    qb = pl.program_id(1)
    gp = kct_ref.shape[0]
    q_augs = []
    for j in range(gp):
        q = q_ref[j * HPG * HEAD_DIM:(j + 1) * HPG * HEAD_DIM, :]
        qt = jnp.concatenate([q[h * HEAD_DIM:(h + 1) * HEAD_DIM, :] for h in range(HPG)], axis=1)
        aug = jnp.concatenate([sl_ref[j], jnp.zeros((8, QL), F32)], axis=0).astype(BF16)
        q_augs.append(jnp.concatenate([qt, aug], axis=0))
    ncp = kct_ref.shape[2]
    n_chunks = ncp // cch
    per_qb = Q_BLOCK // CMP_STRIDE
    last_end = CMP_BLOCK - 1
    need = jnp.minimum((qb * per_qb + per_qb - 2) // cch + 1, n_chunks)
    qoff = lax.broadcasted_iota(jnp.int32, (1, QL), 1) & (Q_BLOCK - 1)
    col_ok = (qb * Q_BLOCK + qoff) >= last_end

    for c in range(1, n_chunks + 1):
        @pl.when(need == c)
        def _(c=c):
            R = c * cch
            r8 = lax.broadcasted_iota(jnp.int32, (8, R), 0)
            n8 = lax.broadcasted_iota(jnp.int32, (8, R), 1)
            hi = ((n8 // per_qb - qb) * Q_BLOCK).astype(F32)
            lo = ((n8 % per_qb) * CMP_STRIDE + last_end).astype(F32)
            pos_rows = jnp.where(r8 < 3, hi, jnp.where(r8 < 6, lo, 0.0))
            k_rows = jnp.concatenate([pos_rows, jnp.zeros((8, R), F32)], axis=0).astype(BF16)
            t0 = max(R - 2 * cch, 0)
            n_io = t0 + lax.broadcasted_iota(jnp.int32, (R - t0, QL), 0)
            visible = n_io * CMP_STRIDE + last_end - qb * Q_BLOCK <= qoff
            ov = ov_ref[:, 0:R]
            for j in range(gp):
                s = _tdot(jnp.concatenate([kct_ref[j, :, 0:R], k_rows], axis=0), q_augs[j])
                tail = jnp.where(visible, s[t0:R], NEG)
                s = tail if t0 == 0 else jnp.concatenate([s[0:t0], tail], axis=0)
                m = jnp.max(s, axis=0, keepdims=True)
                p = jnp.exp2(s - m)
                l = jnp.maximum(jnp.sum(p, axis=0, keepdims=True), 1e-30)
                p = p * jnp.where(col_ok, 1.0 / l, 0.0)
                oc_ref[j, 0] = jnp.dot(vct_ref[j, :, 0:R], p.astype(BF16), preferred_element_type=F32)
                ps = p[:, 0:Q_BLOCK]
                for h in range(1, HPG):
                    ps = ps + p[:, h * Q_BLOCK:(h + 1) * Q_BLOCK]
                hi_p = ps.astype(BF16)
                r1 = ps - hi_p.astype(F32)
                mid_p = r1.astype(BF16)
                lo_p = (r1 - mid_p.astype(F32)).astype(BF16)
                imp_ref[j] = (jnp.dot(ov, hi_p, preferred_element_type=F32)
                              + jnp.dot(ov, mid_p, preferred_element_type=F32)
                              + jnp.dot(ov, lo_p, preferred_element_type=F32))

    n_blk = imp_ref.shape[1]
    j_io = lax.broadcasted_iota(jnp.int32, (n_blk, Q_BLOCK), 0)
    tq = qb * Q_BLOCK + lax.broadcasted_iota(jnp.int32, (n_blk, Q_BLOCK), 1)
    cur = tq // SEL_BLOCK
    forced = (j_io == 0) | (j_io == cur) | (j_io == cur - 1)
    for j in range(gp):
        v = jnp.where(forced, -jnp.inf, jnp.where(j_io <= cur, imp_ref[j], -SEL_FORCE))
        sel = jnp.where(forced, 1.0, 0.0)
        for _ in range(n_sel - 3):
            mx = jnp.max(v, axis=0, keepdims=True)
            idx = jnp.min(jnp.where(v == mx, j_io, n_blk), axis=0, keepdims=True)
            pick = j_io == idx
            v = jnp.where(pick, -jnp.inf, v)
            sel = jnp.where(pick, 1.0, sel)
        live = (sel > 0.5) & (j_io <= cur)
        sel_ref[j, 0, 0:n_blk, :] = jnp.where(live, 0.0, NEG)
        sel_ref[j, 0, n_blk:n_blk + 8, :] = jnp.full((8, Q_BLOCK), NEG, F32)
        cnt_ref[j, 0] = lax.dot_general(jnp.ones((8, Q_BLOCK), BF16), jnp.where(live, 1.0, 0.0).astype(BF16),
                                        (((1,), (1,)), ((), ())), preferred_element_type=F32)


def cmp_attention(feat_t, kc_t, vc_t, slope_rows, ov_t, S):
    G = NSA_KV_HEADS
    n_qb = S // Q_BLOCK
    n_blk = S // SEL_BLOCK
    ncp = kc_t.shape[2]
    n_sel = min(SEL_TOPK, n_blk)
    gp = G
    return pl.pallas_call(
        functools.partial(_cmp_attn_kernel, n_sel=n_sel, cch=min(128, ncp)),
        out_shape=(jax.ShapeDtypeStruct((G, n_qb, HEAD_DIM, QL), F32),
                   jax.ShapeDtypeStruct((G, n_qb, n_blk + 8, Q_BLOCK), F32),
                   jax.ShapeDtypeStruct((G, n_qb, 8, n_blk), F32)),
        grid=(G // gp, n_qb),
        in_specs=[pl.BlockSpec((gp * HPG * HEAD_DIM, Q_BLOCK), lambda g, i: (g, i)),
                  pl.BlockSpec((gp, HEAD_DIM, ncp), lambda g, i: (g, 0, 0)),
                  pl.BlockSpec((gp, HEAD_DIM, ncp), lambda g, i: (g, 0, 0)),
                  pl.BlockSpec((gp, 8, QL), lambda g, i: (g, 0, 0)),
                  pl.BlockSpec((n_blk, ncp), lambda g, i: (0, 0))],
        out_specs=(pl.BlockSpec((gp, 1, HEAD_DIM, QL), lambda g, i: (g, i, 0, 0)),
                   pl.BlockSpec((gp, 1, n_blk + 8, Q_BLOCK), lambda g, i: (g, i, 0, 0)),
                   pl.BlockSpec((gp, 1, 8, n_blk), lambda g, i: (g, i, 0, 0))),
        scratch_shapes=[pltpu.VMEM((gp, n_blk, Q_BLOCK), F32)],
        compiler_params=_cparams(("parallel", "parallel")),
        name="nsa_cmp_select",
    )(feat_t, kc_t, vc_t, slope_rows, ov_t)


KT = 128
SUP = 4
AUG = 16


def _sel_win_kernel(ids_ref, nsup_ref, q_ref, ks_ref, vs_ref, kw_ref, vw_ref, selb_ref, oc_ref, sl_ref, gate_ref,
                    gain_ref, o_ref, *, nt, gp):
    qb = pl.program_id(1)
    n_qb = pl.num_programs(1)

    def k_aug(kt, hi_lane, with_blocks):
        n = kt.shape[1]
        r = lax.broadcasted_iota(jnp.int32, (8, n), 0)
        lane = lax.broadcasted_iota(jnp.int32, (8, n), 1)
        lo = (lane & (KT - 1)).astype(F32)
        pos_rows = jnp.where(r < 3, hi_lane, jnp.where(r < 6, lo, 0.0))
        if with_blocks:
            blk_rows = jnp.where((lane // SEL_BLOCK) == r, 1.0, 0.0)
        else:
            blk_rows = jnp.zeros((8, n), F32)
        return jnp.concatenate([kt, jnp.concatenate([pos_rows, blk_rows], axis=0).astype(BF16)], axis=0)

    def v_aug(vt):
        n = vt.shape[1]
        ones = jnp.where(lax.broadcasted_iota(jnp.int32, (AUG, n), 0) == 0, 1.0, 0.0).astype(BF16)
        return jnp.concatenate([vt, ones], axis=0)

    def tdot(a, b):
        return lax.dot_general(a, b, (((0,), (0,)), ((), ())), preferred_element_type=F32)

    row_io = lax.broadcasted_iota(jnp.int32, (KT, QL), 0)
    qoff = lax.broadcasted_iota(jnp.int32, (KT, QL), 1) & (Q_BLOCK - 1)
    lane_s = lax.broadcasted_iota(jnp.int32, (1, SUP * KT), 1)
    nw = WINDOW // KT + 1
    tiles_w = [qb - (nw - 1) + i for i in range(nw)]
    k0w = [pl.multiple_of(jnp.maximum(t, 0) * KT, KT) for t in tiles_w]
    hi_w = ((lax.broadcasted_iota(jnp.int32, (1, nw * KT), 1) // KT - (nw - 1)) * KT).astype(F32)
    k0d = pl.multiple_of(qb * KT, KT)
    zero_hi = jnp.zeros((1, KT), F32)

    def make_group(j):
        g = pl.program_id(0) * gp + j
        rows = slice(j * HEAD_DIM, (j + 1) * HEAD_DIM)
        q = q_ref[j * HPG * HEAD_DIM:(j + 1) * HPG * HEAD_DIM, :]
        qt = jnp.concatenate([q[h * HEAD_DIM:(h + 1) * HEAD_DIM, :] for h in range(HPG)], axis=1)
        slope_rows = sl_ref[j]

        def q_aug(mask):
            aug = jnp.concatenate([slope_rows, mask], axis=0).astype(BF16)
            return jnp.concatenate([qt, aug], axis=0)

        def mask_rows(block_ids):
            r = lax.broadcasted_iota(jnp.int32, (8, Q_BLOCK), 0)
            out = jnp.zeros((8, Q_BLOCK), F32)
            for i, b in enumerate(block_ids):
                row = jnp.broadcast_to(selb_ref[j, 0, pl.ds(b, 1), :], (8, Q_BLOCK))
                out = jnp.where(r == i, row, out)
            return jnp.concatenate([out] * HPG, axis=1)

        ktw = jnp.concatenate([kw_ref[rows, pl.ds(k0, KT)] for k0 in k0w], axis=1)
        vtw = jnp.concatenate([vw_ref[rows, pl.ds(k0, KT)] for k0 in k0w], axis=1)
        s = tdot(k_aug(ktw, hi_w, False), q_aug(jnp.zeros((8, QL), F32)))
        parts = []
        for i in range(nw):
            si = s[i * KT:(i + 1) * KT]
            if i == 0:
                si = jnp.where((row_io > qoff) & (tiles_w[i] >= 0), si, NEG)
            elif i == nw - 1:
                si = jnp.where(row_io <= qoff, si, NEG)
            else:
                si = jnp.where(tiles_w[i] >= 0, si, NEG)
            parts.append(si)
        s = jnp.concatenate(parts, axis=0)
        m_w = jnp.max(s, axis=0, keepdims=True)
        acc_w = jnp.dot(v_aug(vtw), jnp.exp2(s - m_w).astype(BF16), preferred_element_type=F32)

        s = tdot(k_aug(ks_ref[rows, pl.ds(k0d, KT)], zero_hi, True), q_aug(mask_rows([2 * qb, 2 * qb + 1])))
        s = jnp.where(row_io <= qoff, s, NEG)
        m_s = jnp.max(s, axis=0, keepdims=True)
        acc_s = jnp.dot(v_aug(vs_ref[rows, pl.ds(k0d, KT)]), jnp.exp2(s - m_s).astype(BF16),
                        preferred_element_type=F32)

        def update(si, carry):
            m, acc = carry
            base = (g * n_qb + qb) * nt + si * SUP
            tids = [ids_ref[base + i] for i in range(SUP)]
            tcl = [jnp.minimum(t, nt - 1) for t in tids]
            k0s = [pl.multiple_of(t * KT, KT) for t in tcl]
            kt = jnp.concatenate([ks_ref[rows, pl.ds(k0, KT)] for k0 in k0s], axis=1)
            vt = jnp.concatenate([vs_ref[rows, pl.ds(k0, KT)] for k0 in k0s], axis=1)
            hi = (tcl[SUP - 1] - qb) * KT
            for i in range(SUP - 2, -1, -1):
                hi = jnp.where(lane_s < (i + 1) * KT, (tcl[i] - qb) * KT, hi)
            blocks = [2 * t + b for t in tids for b in range(2)]
            s = tdot(k_aug(kt, hi.astype(F32), True), q_aug(mask_rows(blocks)))
            m_new = jnp.maximum(m, jnp.max(s, axis=0, keepdims=True))
            p = jnp.exp2(s - m_new).astype(BF16)
            acc = jnp.exp2(m - m_new) * acc + jnp.dot(v_aug(vt), p, preferred_element_type=F32)
            return m_new, acc

        def finish(carry):
            _, acc = carry
            gts = _sigmoid(gate_ref[j, 0])
            o_s = acc[0:HEAD_DIM] / acc[HEAD_DIM:HEAD_DIM + 1]
            o_w = acc_w[0:HEAD_DIM] / acc_w[HEAD_DIM:HEAD_DIM + 1]
            o = gts[0:1] * oc_ref[j, 0] + gts[1:2] * o_s + gts[2:3] * o_w
            ms = jnp.mean(o * o, axis=0, keepdims=True)
            y = o * lax.rsqrt(ms + NORM_EPS) * gain_ref[j]
            yt = jnp.concatenate([y[:, h * Q_BLOCK:(h + 1) * Q_BLOCK] for h in range(HPG)], axis=0)
            o_ref[:, j * HPG * HEAD_DIM:(j + 1) * HPG * HEAD_DIM] = yt.T

        return (m_s, acc_s), nsup_ref[g * n_qb + qb], update, finish

    groups = [make_group(j) for j in range(gp)]
    n_iter = groups[0][1]
    for grp in groups[1:]:
        n_iter = jnp.maximum(n_iter, grp[1])
    carries = lax.fori_loop(0, n_iter, lambda si, cs: tuple(grp[2](si, c) for grp, c in zip(groups, cs)),
                            tuple(grp[0] for grp in groups))
    for grp, c in zip(groups, carries):
        grp[3](c)


def sel_win_attention(tile_ids, n_sup, feat_t, selb, oc_t, slope_rows, gates_t, gain_t, S):
    G = NSA_KV_HEADS
    gp = 2
    n_qb = S // Q_BLOCK
    n_blk = S // SEL_BLOCK
    nt = S // KT
    kv_rows = gp * HEAD_DIM
    kv_spec = lambda base: pl.BlockSpec((kv_rows, S), lambda g, i, ids, ns: (base // kv_rows + g, 0))
    grid_spec = pltpu.PrefetchScalarGridSpec(
        num_scalar_prefetch=2,
        grid=(G // gp, n_qb),
        in_specs=[pl.BlockSpec((gp * HPG * HEAD_DIM, Q_BLOCK), lambda g, i, ids, ns: (g, i)),
                  kv_spec(FT_KS), kv_spec(FT_VS), kv_spec(FT_KW), kv_spec(FT_VW),
                  pl.BlockSpec((gp, 1, n_blk + 8, Q_BLOCK), lambda g, i, ids, ns: (g, i, 0, 0)),
                  pl.BlockSpec((gp, 1, HEAD_DIM, QL), lambda g, i, ids, ns: (g, i, 0, 0)),
                  pl.BlockSpec((gp, 8, QL), lambda g, i, ids, ns: (g, 0, 0)),
                  pl.BlockSpec((gp, 1, 3, QL), lambda g, i, ids, ns: (g, i, 0, 0)),
                  pl.BlockSpec((gp, HEAD_DIM, QL), lambda g, i, ids, ns: (g, 0, 0))],
        out_specs=pl.BlockSpec((Q_BLOCK, gp * HPG * HEAD_DIM), lambda g, i, ids, ns: (i, g)),
    )
    return pl.pallas_call(
        functools.partial(_sel_win_kernel, nt=nt, gp=gp),
        out_shape=jax.ShapeDtypeStruct((S, NSA_WIDTH), F32),
        grid_spec=grid_spec,
        compiler_params=_cparams(("parallel", "parallel")),
        name="nsa_sel_win",
    )(tile_ids, n_sup, feat_t, feat_t, feat_t, feat_t, feat_t, selb, oc_t, slope_rows, gates_t, gain_t)


def _conv_kernel(*refs, width, glu, post):
    if post == "ln_silu":
        x_ref, halo_ref, w_ref, b_ref, lg_ref, lb_ref, o_ref, u_ref = refs
    else:
        x_ref, halo_ref, w_ref, b_ref, o_ref, u_ref = refs
    i = pl.program_id(0)
    T = o_ref.shape[0]
    H = halo_ref.shape[0]
    C = o_ref.shape[1]

    def pre(v):
        return v[:, :C] * _sigmoid(v[:, C:]) if glu else v

    u_ref[0:H, :] = jnp.where(i > 0, pre(halo_ref[...]), 0.0)
    u_ref[H:H + T, :] = pre(x_ref[...])
    acc = jnp.broadcast_to(b_ref[...], (T, C))
    base = H - (width - 1)
    for k in range(width):
        acc = acc + w_ref[k:k + 1, :] * u_ref[base + k:base + k + T, :]
    if post == "ln_silu":
        mu = jnp.mean(acc, axis=-1, keepdims=True)
        xc = acc - mu
        var = jnp.mean(xc * xc, axis=-1, keepdims=True)
        acc = xc * lax.rsqrt(var + LN_EPS) * lg_ref[...] + lb_ref[...]
    o_ref[...] = acc * _sigmoid(acc)


def causal_conv(x, col_block, w, b, ln=None, *, glu, T):
    S = x.shape[0]
    width, C = w.shape
    cin = 2 * C if glu else C
    H = -(-(width - 1) // 8) * 8
    post = "ln_silu" if ln is not None else "silu"
    in_specs = [pl.BlockSpec((T, cin), lambda i: (i, col_block)),
                pl.BlockSpec((H, cin), lambda i: (jnp.maximum(i * (T // H) - 1, 0), col_block)),
                pl.BlockSpec((width, C), lambda i: (0, 0)),
                pl.BlockSpec((1, C), lambda i: (0, 0))]
    args = [x, x, w, b]
    if ln is not None:
        in_specs += [pl.BlockSpec((1, C), lambda i: (0, 0))] * 2
        args += list(ln)
    return pl.pallas_call(
        functools.partial(_conv_kernel, width=width, glu=glu, post=post),
        out_shape=jax.ShapeDtypeStruct((S, C), F32),
        grid=(S // T,),
        in_specs=in_specs,
        out_specs=pl.BlockSpec((T, C), lambda i: (i, 0)),
        scratch_shapes=[pltpu.VMEM((H + T, C), F32)],
        compiler_params=_cparams(("parallel",)),
        name="causal_conv_glu" if glu else "causal_conv",
    )(*args)


def _log_sigmoid(x):
    return jnp.minimum(x, 0.0) - jnp.log(1.0 + jnp.exp(-jnp.abs(x)))


def _mlstm_kernel(qk_ref, v_ref, o_ref, sm_ref, gt_ref, bcol_ref, brow_ref, gain_ref, out_ref, c_ref, m_ref):
    L = qk_ref.shape[0]
    DH = MLSTM_HEAD_DIM
    W = MLSTM_WIDTH

    @pl.when(pl.program_id(0) == 0)
    def _():
        c_ref[...] = jnp.zeros_like(c_ref)
        m_ref[...] = jnp.zeros_like(m_ref)

    small = sm_ref[...]
    lane = lax.broadcasted_iota(jnp.int32, small.shape, 1)
    t_io = lax.broadcasted_iota(jnp.int32, (L, L), 0)
    s_io = lax.broadcasted_iota(jnp.int32, (L, L), 1)
    causal = s_io <= t_io
    ones_col = (lax.broadcasted_iota(jnp.int32, (L, DH), 1) == 0).astype(BF16)

    def column(c):
        return jnp.sum(jnp.where(lane == c, small, 0.0), axis=1, keepdims=True)

    for h in range(MLSTM_HEADS):
        q = qk_ref[:, h * DH:(h + 1) * DH].astype(BF16)
        k = qk_ref[:, W + h * DH:W + (h + 1) * DH] * (DH ** -0.5)
        v = v_ref[:, h * DH:(h + 1) * DH].astype(BF16)
        vaug = jnp.concatenate([v, ones_col], axis=1)
        i_col = column(SM_I + h) + bcol_ref[h:h + 1, :]
        f_col = column(SM_F + h) + bcol_ref[MLSTM_HEADS + h:MLSTM_HEADS + h + 1, :]
        i_row = gt_ref[h:h + 1, :] + brow_ref[h:h + 1, :]
        f_row = gt_ref[MLSTM_HEADS + h:MLSTM_HEADS + h + 1, :] + brow_ref[MLSTM_HEADS + h:MLSTM_HEADS + h + 1, :]
        lf_col = _log_sigmoid(f_col)
        lf_row = _log_sigmoid(f_row)
        b_col = jnp.sum(jnp.where(causal, lf_row, 0.0), axis=1, keepdims=True)
        b_row = jnp.sum(jnp.where(t_io <= s_io, lf_col, 0.0), axis=0, keepdims=True)
        b_end = jnp.sum(lf_row, axis=1, keepdims=True)
        m0 = m_ref[h:h + 1, 0:1]
        c0 = c_ref[h]

        dmat = jnp.where(causal, b_col - b_row + i_row, -jnp.inf)
        inter = b_col + m0
        m_t = jnp.maximum(inter, jnp.max(dmat, axis=1, keepdims=True))
        qk = lax.dot_general(q, k.astype(BF16), (((1,), (1,)), ((), ())), preferred_element_type=F32)
        smat = qk * jnp.exp(dmat - m_t)
        w_inter = jnp.exp(inter - m_t)
        r = (jnp.dot(smat.astype(BF16), vaug, preferred_element_type=F32)
             + w_inter * jnp.dot(q, c0.astype(BF16), preferred_element_type=F32))
        num = r[:, :DH]
        den = jnp.sum(r[:, DH:], axis=1, keepdims=True)
        hh = num / jnp.maximum(jnp.abs(den), jnp.exp(-m_t))

        a_row = b_end - b_row + i_row
        a_col = b_end - b_col + i_col
        m_new = jnp.maximum(b_end + m0, jnp.max(a_row, axis=1, keepdims=True))
        kw = (k * jnp.exp(a_col - m_new)).astype(BF16)
        c_ref[h] = jnp.exp(b_end + m0 - m_new) * c0 + lax.dot_general(
            kw, vaug, (((0,), (0,)), ((), ())), preferred_element_type=F32)
        m_ref[h:h + 1, :] = jnp.broadcast_to(m_new, (1, m_ref.shape[1]))

        y = _sigmoid(o_ref[:, h * DH:(h + 1) * DH]) * hh
        ms = jnp.mean(y * y, axis=-1, keepdims=True)
        out_ref[:, h * DH:(h + 1) * DH] = y * lax.rsqrt(ms + NORM_EPS) * gain_ref[:, h * DH:(h + 1) * DH]


def mlstm(qk, rm, gates_t, bias_col, bias_row, gain, S):
    L = min(MLSTM_CHUNK, S)
    W = MLSTM_WIDTH
    return pl.pallas_call(
        _mlstm_kernel,
        out_shape=jax.ShapeDtypeStruct((S, W), F32),
        grid=(S // L,),
        in_specs=[pl.BlockSpec((L, 2 * W), lambda c: (c, 0)),
                  pl.BlockSpec((L, W), lambda c: (c, RM_V // W)),
                  pl.BlockSpec((L, W), lambda c: (c, RM_O // W)),
                  pl.BlockSpec((L, 128), lambda c: (c, RM_SMALL // 128)),
                  pl.BlockSpec((2 * MLSTM_HEADS, L), lambda c: (0, c)),
                  pl.BlockSpec((2 * MLSTM_HEADS, 1), lambda c: (0, 0)),
                  pl.BlockSpec((2 * MLSTM_HEADS, 1), lambda c: (0, 0)),
                  pl.BlockSpec((1, W), lambda c: (0, 0))],
        out_specs=pl.BlockSpec((L, W), lambda c: (c, 0)),
        scratch_shapes=[pltpu.VMEM((MLSTM_HEADS, MLSTM_HEAD_DIM, 2 * MLSTM_HEAD_DIM), F32),
                        pltpu.VMEM((8, 128), F32)],
        compiler_params=_cparams(("arbitrary",)),
        name="mlstm",
    )(qk, rm, rm, rm, gates_t, bias_col, bias_row, gain)


def _router_kernel(x_ref, g_ref, wr_ref, br_ref, xn_ref, eid_ref, gate_ref):
    xn = _rms_rows(x_ref[...], g_ref[...])
    xn_ref[...] = xn.astype(xn_ref.dtype)
    logits = lax.dot_general(wr_ref[...], xn, (((1,), (1,)), ((), ())), precision=lax.Precision.HIGHEST,
                             preferred_element_type=F32) + br_ref[...]
    tm = logits.shape[1]
    lg = logits[N_EXPERTS:N_EXPERTS + N_GROUPS, :]
    eg = jnp.exp(lg - jnp.max(lg, axis=0, keepdims=True))
    pg = eg / jnp.sum(eg, axis=0, keepdims=True)
    pg_top = jnp.max(pg, axis=0, keepdims=True)
    g_io = lax.broadcasted_iota(jnp.int32, (N_GROUPS, tm), 0)
    grp = jnp.min(jnp.where(pg == pg_top, g_io, N_GROUPS), axis=0, keepdims=True)
    le = logits[0:EXPERTS_PER_GROUP, :]
    for g in range(1, N_GROUPS):
        le = jnp.where(grp == g, logits[g * EXPERTS_PER_GROUP:(g + 1) * EXPERTS_PER_GROUP, :], le)
    ee = jnp.exp(le - jnp.max(le, axis=0, keepdims=True))
    pe = ee / jnp.sum(ee, axis=0, keepdims=True)
    e_io = lax.broadcasted_iota(jnp.int32, (EXPERTS_PER_GROUP, tm), 0)
    p1 = jnp.max(pe, axis=0, keepdims=True)
    i1 = jnp.min(jnp.where(pe == p1, e_io, EXPERTS_PER_GROUP), axis=0, keepdims=True)
    pe2 = jnp.where(e_io == i1, -1.0, pe)
    p2 = jnp.max(pe2, axis=0, keepdims=True)
    i2 = jnp.min(jnp.where(pe2 == p2, e_io, EXPERTS_PER_GROUP), axis=0, keepdims=True)
    tot = p1 + p2
    eid_ref[...] = jnp.concatenate([grp * EXPERTS_PER_GROUP + i1, grp * EXPERTS_PER_GROUP + i2], axis=0)
    gate_ref[...] = jnp.concatenate([pg_top * p1 / tot, pg_top * p2 / tot], axis=0)


def router(x, g, wr_t, br_t, tm):
    S, D = x.shape
    R = wr_t.shape[0]
    return pl.pallas_call(
        _router_kernel,
        out_shape=(jax.ShapeDtypeStruct((S, D), F32),
                   jax.ShapeDtypeStruct((2, S), jnp.int32),
                   jax.ShapeDtypeStruct((2, S), F32)),
        grid=(S // tm,),
        in_specs=[pl.BlockSpec((tm, D), lambda i: (i, 0)),
                  pl.BlockSpec((1, D), lambda i: (0, 0)),
                  pl.BlockSpec((R, D), lambda i: (0, 0)),
                  pl.BlockSpec((R, 1), lambda i: (0, 0))],
        out_specs=(pl.BlockSpec((tm, D), lambda i: (i, 0)),
                   pl.BlockSpec((2, tm), lambda i: (0, i)),
                   pl.BlockSpec((2, tm), lambda i: (0, i))),
        compiler_params=_cparams(("parallel",)),
        name="moe_router",
    )(x, g, wr_t, br_t)


def _row_gather(idx_ref, idx0, stride, src_hbm, dst, sem, n_rows):
    def body(r, carry):
        i = idx_ref[idx0 + r * stride]
        pltpu.make_async_copy(src_hbm.at[pl.ds(i, 1), :], dst.at[pl.ds(r, 1), :], sem).start()
        return carry

    lax.fori_loop(0, n_rows, body, 0, unroll=8)


def _rows_wait(src_hbm, dst, sem):
    pltpu.make_async_copy(src_hbm.at[pl.ds(0, dst.shape[0]), :], dst, sem).wait()


def _expert_kernel(be_ref, nu_ref, tok_ref, dst_ref, xn_hbm, wg_ref, wu_ref, wd_ref, y_hbm,
                   xbuf0, xbuf1, obuf, gsem, ssem, wg_s, wu_s, wd_s):
    b = pl.program_id(0)
    n_used = nu_ref[0]
    e = be_ref[b]
    prev = be_ref[jnp.maximum(b - 1, 0)]
    spare0 = y_hbm.shape[0] - MOE_ROWS

    def scatter_wait():
        pltpu.make_async_copy(obuf, y_hbm.at[pl.ds(0, MOE_ROWS), :], ssem.at[0]).wait()

    xbufs = (xbuf0, xbuf1)

    @pl.when(b == 0)
    def _():
        _row_gather(tok_ref, 0, 1, xn_hbm, xbuf0, gsem.at[0], MOE_ROWS)
        obuf[...] = jnp.zeros_like(obuf)
        pltpu.make_async_copy(obuf, y_hbm.at[pl.ds(spare0, MOE_ROWS), :], ssem.at[0]).start()

    @pl.when((b == 0) | (e != prev))
    def _():
        wg_s[...] = wg_ref[0, 0].astype(BF16)
        wu_s[...] = wu_ref[0, 0].astype(BF16)
        wd_s[...] = wd_ref[0, 0].astype(BF16)

    half = MOE_ROWS // 2

    def run_block(slot):
        cur, oth = xbufs[slot], xbufs[1 - slot]
        _rows_wait(xn_hbm, cur, gsem.at[slot])
        nxt = jnp.minimum(b + 1, n_used - 1) * MOE_ROWS

        def ffn(r0):
            x = cur[r0:r0 + half, :].astype(BF16)
            hg = jnp.dot(x, wg_s[...], preferred_element_type=F32)
            hu = jnp.dot(x, wu_s[...], preferred_element_type=F32)
            hb = (hg * _sigmoid(hg) * hu).astype(BF16)
            return jnp.dot(hb, wd_s[...], preferred_element_type=F32)

        def gather_next(r0):
            for r in range(r0, r0 + half):
                pltpu.make_async_copy(xn_hbm.at[pl.ds(tok_ref[nxt + r], 1), :], oth.at[pl.ds(r, 1), :],
                                      gsem.at[1 - slot]).start()

        def scatter(r0):
            for r in range(r0, r0 + half):
                pltpu.make_async_copy(obuf.at[pl.ds(r, 1), :],
                                      y_hbm.at[pl.ds(dst_ref[b * MOE_ROWS + r], 1), :], ssem.at[0]).start()

        gather_next(0)
        y0 = ffn(0)
        scatter_wait()
        obuf[0:half, :] = y0
        scatter(0)
        gather_next(half)
        obuf[half:MOE_ROWS, :] = ffn(half)
        scatter(half)

    for slot in range(2):
        @pl.when((b < n_used) & (b % 2 == slot))
        def _(slot=slot):
            run_block(slot)

    @pl.when(b == n_used - 1)
    def _():
        scatter_wait()
        for slot in range(2):
            @pl.when((b + 1) % 2 == slot)
            def _(slot=slot):
                _rows_wait(xn_hbm, xbufs[slot], gsem.at[slot])


def expert_ffn(blk_e, n_used, row_tok, row_dst, xn, w_gate, w_up, w_down, layer, n_out):
    P = row_tok.shape[0]
    D = xn.shape[1]
    n_blocks = P // MOE_ROWS
    Hd = w_gate.shape[3]
    grid_spec = pltpu.PrefetchScalarGridSpec(
        num_scalar_prefetch=4,
        grid=(n_blocks,),
        in_specs=[pl.BlockSpec(memory_space=pl.ANY),
                  pl.BlockSpec((1, 1, D, Hd), lambda b, be, nu, tk, ds: (layer, be[b], 0, 0)),
                  pl.BlockSpec((1, 1, D, Hd), lambda b, be, nu, tk, ds: (layer, be[b], 0, 0)),
                  pl.BlockSpec((1, 1, Hd, D), lambda b, be, nu, tk, ds: (layer, be[b], 0, 0))],
        out_specs=pl.BlockSpec(memory_space=pl.ANY),
        scratch_shapes=[pltpu.VMEM((MOE_ROWS, D), F32), pltpu.VMEM((MOE_ROWS, D), F32),
                        pltpu.VMEM((MOE_ROWS, D), F32),
                        pltpu.SemaphoreType.DMA((2,)), pltpu.SemaphoreType.DMA((1,)),
                        pltpu.VMEM((D, Hd), BF16), pltpu.VMEM((D, Hd), BF16), pltpu.VMEM((Hd, D), BF16)],
    )
    return pl.pallas_call(
        _expert_kernel,
        out_shape=jax.ShapeDtypeStruct((n_out + MOE_ROWS, D), F32),
        grid_spec=grid_spec,
        compiler_params=_cparams(("arbitrary",)),
        name="moe_experts",
    )(blk_e, n_used, row_tok, row_dst, xn, w_gate, w_up, w_down)


def _combine_kernel(x_ref, gate_ref, y0_ref, y1_ref, o_ref):
    gate = gate_ref[...]
    o_ref[...] = x_ref[...] + gate[:, 0:1] * y0_ref[...] + gate[:, 1:2] * y1_ref[...]


def moe_combine(x, gate, y, tt):
    S, D = x.shape
    return pl.pallas_call(
        _combine_kernel,
        out_shape=jax.ShapeDtypeStruct((S, D), F32),
        grid=(S // tt,),
        in_specs=[pl.BlockSpec((tt, D), lambda i: (i, 0)),
                  pl.BlockSpec((tt, 2), lambda i: (i, 0)),
                  pl.BlockSpec((tt, D), lambda i: (i, 0)),
                  pl.BlockSpec((tt, D), lambda i: (S // tt + i, 0))],
        out_specs=pl.BlockSpec((tt, D), lambda i: (i, 0)),
        compiler_params=_cparams(("parallel",)),
        name="moe_combine",
    )(x, gate, y, y)


def _final_norm_kernel(x_ref, g_ref, o_ref):
    o_ref[...] = _rms_rows(x_ref[...], g_ref[...])


def final_norm(x, g, tm):
    S, D = x.shape
    return pl.pallas_call(
        _final_norm_kernel,
        out_shape=jax.ShapeDtypeStruct((S, D), F32),
        grid=(S // tm,),
        in_specs=[pl.BlockSpec((tm, D), lambda i: (i, 0)), pl.BlockSpec((1, D), lambda i: (0, 0))],
        out_specs=pl.BlockSpec((tm, D), lambda i: (i, 0)),
        compiler_params=_cparams(("parallel",)),
        name="final_norm",
    )(x, g)


LOG2E = float(np.log2(np.e))


def _alibi_lanes():
    sl = 2.0 ** (-8.0 * np.arange(1, NSA_HEADS + 1) / NSA_HEADS) * LOG2E
    sl = np.repeat(sl.reshape(NSA_KV_HEADS, HPG, 1), Q_BLOCK, axis=2).reshape(NSA_KV_HEADS, 1, QL)
    sl = jnp.asarray(sl, F32)
    s1 = sl.astype(BF16).astype(F32)
    s2 = (sl - s1).astype(BF16).astype(F32)
    s3 = (sl - s1 - s2).astype(BF16).astype(F32)
    zero = jnp.zeros_like(sl)
    return jnp.concatenate([s1, s2, s3, s1, s2, s3, zero, zero], axis=1)


def _active_tiles(cnt, S):
    G, n_qb, n_blk = cnt.shape
    nt = S // KT
    act = (cnt > 0.5).reshape(G, n_qb, nt, KT // SEL_BLOCK).any(-1)
    tile = jnp.arange(nt, dtype=jnp.int32)
    act = act & (tile[None, None, :] < (jnp.arange(n_qb) * (Q_BLOCK // KT))[None, :, None])
    rank = jnp.cumsum(act.astype(jnp.int32), axis=-1) - 1
    n_act = rank[..., -1] + 1
    hit = act[..., None, :] & (rank[..., None, :] == tile[None, None, :, None])
    ids = jnp.sum(jnp.where(hit, tile[None, None, None, :], 0), axis=-1)
    ids = jnp.where(tile[None, None, :] < n_act[..., None], ids, nt).astype(jnp.int32)
    return ids.reshape(-1), ((n_act + SUP - 1) // SUP).astype(jnp.int32).reshape(-1)


def _overlap_t(n_cmp_pad, n_cmp, n_blk):
    cs = np.arange(n_cmp_pad) * CMP_STRIDE
    ss = np.arange(n_blk) * SEL_BLOCK
    ov = np.minimum(cs[None, :] + CMP_BLOCK, ss[:, None] + SEL_BLOCK) - np.maximum(cs[None, :], ss[:, None])
    ov = np.clip(ov, 0, None) / CMP_STRIDE
    ov[:, n_cmp:] = 0.0
    return jnp.asarray(ov, BF16)


def mixer(x, p, S):
    G = NSA_KV_HEADS
    tm = min(1024, S)
    w_in = p["w_in"]
    c = np.cumsum((0, NSA_WIDTH) + (NSA_KV_WIDTH,) * 6 + (3 * NSA_HEADS,) + (MLSTM_WIDTH,) * 4
                  + (MLSTM_HEADS,) * 2 + (2 * CONV_CHANNELS,))
    (q0, kc0, vc0, ks0, vs0, kw0, vw0, gt0, mq0, mk0, mv0, mo0, mi0, mf0, cu0, end) = [int(v) for v in c]
    w_rm = jnp.concatenate(
        [w_in[:, mq0:mv0], w_in[:, cu0:end], w_in[:, mv0:mi0], w_in[:, kc0:ks0], w_in[:, gt0:mq0],
         w_in[:, mi0:cu0], jnp.zeros((D_MODEL, RM_WIDTH - RM_SMALL - 56), F32)], axis=1).astype(BF16)
    w_ft = jnp.concatenate([w_in[:, q0:kc0], w_in[:, ks0:gt0]], axis=1).T.astype(BF16)
    g = p["attn_norm_g"][None, :]
    rm = norm_matmul(x, g, w_rm, tm, 768)
    q_scale = jnp.where(jnp.arange(FT_WIDTH) < FT_KS, HEAD_DIM ** -0.5 * LOG2E, 1.0).astype(F32)[:, None]
    feat_t = norm_matmul_t(x, g, w_ft, q_scale, tm, 512)

    ncp = S // CMP_STRIDE
    n_cmp = (S - CMP_BLOCK) // CMP_STRIDE + 1
    n_blk = S // SEL_BLOCK
    n_qb = S // Q_BLOCK
    kv = jnp.stack([rm[:, RM_KC:RM_KC + NSA_KV_WIDTH], rm[:, RM_VC:RM_VC + NSA_KV_WIDTH]])
    rows = kv.reshape(2, ncp, CMP_STRIDE, G, HEAD_DIM).transpose(0, 3, 1, 2, 4).reshape(2, G, ncp, -1)
    nxt = jnp.concatenate([rows[:, :, 1:], jnp.zeros_like(rows[:, :, :1])], axis=2)
    blk = jnp.concatenate([rows, nxt], axis=-1)
    pos = jnp.stack([p["cmp_pos_k"].reshape(1, -1), p["cmp_pos_v"].reshape(1, -1)])
    w1 = jnp.stack([p["cmp_w1_k"], p["cmp_w1_v"]]).astype(BF16)
    w2t = jnp.stack([p["cmp_w2_k"].T, p["cmp_w2_v"].T]).astype(BF16)
    kvc_t = compress(blk, pos, w1, w2t)
    slope_rows = _alibi_lanes()
    oc_t, selb, cnt = cmp_attention(feat_t, kvc_t[0], kvc_t[1], slope_rows, _overlap_t(ncp, n_cmp, n_blk), S)
    tile_ids, n_sup = _active_tiles(cnt[:, :, 0, :], S)
    small = rm[:, RM_SMALL:RM_SMALL + 128]
    gates_t = small[:, SM_GATES:SM_GATES + 3 * NSA_HEADS].reshape(n_qb, Q_BLOCK, G, HPG, 3)
    gates_t = gates_t.transpose(2, 0, 4, 3, 1).reshape(G, n_qb, 3, QL)
    gain_t = jnp.broadcast_to(p["nsa_out_g"].reshape(G, HPG, HEAD_DIM, 1).transpose(0, 2, 1, 3),
                              (G, HEAD_DIM, HPG, Q_BLOCK)).reshape(G, HEAD_DIM, QL)
    a_out = sel_win_attention(tile_ids, n_sup, feat_t, selb, oc_t, slope_rows, gates_t, gain_t, S)

    tc = min(512, S)
    qk = causal_conv(rm, RM_QK // (2 * MLSTM_WIDTH), p["mlstm_conv_w"], p["mlstm_conv_b"][None, :],
                     glu=False, T=tc)
    gates_m = small[:, SM_I:SM_I + 2 * MLSTM_HEADS].T
    bias = jnp.concatenate([p["mlstm_i_bias"], p["mlstm_f_bias"]])
    hm = mlstm(qk, rm, gates_m, bias[:, None], bias[:, None], p["mlstm_out_g"][None, :], S)

    cv = causal_conv(rm, RM_CU // (2 * CONV_CHANNELS), p["conv_w"], p["conv_b"][None, :],
                     (p["conv_ln_g"][None, :], p["conv_ln_b"][None, :]), glu=True, T=tc)

    return out_proj(a_out, hm, cv, x, p["w_out"].astype(BF16), min(512, S), 1024)


def moe(x, p, stacked_w, layer, S):
    tm = min(512, S)
    wr_t = jnp.concatenate([p["router_w_expert"].T, p["router_w_group"].T,
                            jnp.zeros((12, D_MODEL), F32)], axis=0)
    br_t = jnp.concatenate([p["router_b_expert"].reshape(-1), p["router_b_group"], jnp.zeros((12,), F32)])[:, None]
    xn, eid, gate = router(x, p["ffn_norm_g"][None, :], wr_t, br_t, tm)

    flat_e = eid.T.reshape(-1)
    A = flat_e.shape[0]
    onehot = (flat_e[:, None] == jnp.arange(N_EXPERTS)[None, :]).astype(jnp.int32)
    rank = jnp.take_along_axis(jnp.cumsum(onehot, axis=0), flat_e[:, None], axis=1)[:, 0] - 1
    counts = jnp.sum(onehot, axis=0)
    padded = (counts + MOE_ROWS - 1) // MOE_ROWS * MOE_ROWS
    pends = jnp.cumsum(padded)
    dest = (pends - padded)[flat_e] + rank
    n_blocks = A // MOE_ROWS + N_EXPERTS
    P = n_blocks * MOE_ROWS
    row_a = jnp.full((P,), -1, jnp.int32).at[dest].set(jnp.arange(A, dtype=jnp.int32), unique_indices=True)
    row_tok = jnp.maximum(row_a, 0) // 2
    spare = 2 * S + jnp.arange(P, dtype=jnp.int32) % MOE_ROWS
    row_dst = jnp.where(row_a >= 0, (row_a % 2) * S + row_a // 2, spare)
    n_used = (pends[-1] // MOE_ROWS).astype(jnp.int32)
    blk_start = jnp.minimum(jnp.arange(n_blocks), n_used - 1) * MOE_ROWS
    blk_e = jnp.minimum(jnp.sum(pends[None, :] <= blk_start[:, None], axis=1), N_EXPERTS - 1).astype(jnp.int32)

    y = expert_ffn(blk_e, n_used[None], row_tok, row_dst, xn, stacked_w["expert_w_gate"],
                   stacked_w["expert_w_up"], stacked_w["expert_w_down"], layer, 2 * S)
    return moe_combine(x, gate.T, y, min(512, S))


_LAYER_KEYS = ("attn_norm_g", "w_in", "cmp_pos_k", "cmp_w1_k", "cmp_w2_k", "cmp_pos_v", "cmp_w1_v", "cmp_w2_v",
               "nsa_out_g", "mlstm_conv_w", "mlstm_conv_b", "mlstm_i_bias", "mlstm_f_bias", "mlstm_out_g",
               "conv_w", "conv_b", "conv_ln_g", "conv_ln_b", "w_out", "ffn_norm_g", "router_w_group",
               "router_b_group", "router_w_expert", "router_b_expert", "expert_w_gate", "expert_w_up",
               "expert_w_down")


def kernel(x, attn_norm_g, w_in, cmp_pos_k, cmp_w1_k, cmp_w2_k, cmp_pos_v, cmp_w1_v, cmp_w2_v, nsa_out_g, mlstm_conv_w, mlstm_conv_b, mlstm_i_bias, mlstm_f_bias, mlstm_out_g, conv_w, conv_b, conv_ln_g, conv_ln_b, w_out, ffn_norm_g, router_w_group, router_b_group, router_w_expert, router_b_expert, expert_w_gate, expert_w_up, expert_w_down, final_norm_g):
    stacked = (attn_norm_g, w_in, cmp_pos_k, cmp_w1_k, cmp_w2_k, cmp_pos_v, cmp_w1_v, cmp_w2_v, nsa_out_g,
               mlstm_conv_w, mlstm_conv_b, mlstm_i_bias, mlstm_f_bias, mlstm_out_g, conv_w, conv_b, conv_ln_g,
               conv_ln_b, w_out, ffn_norm_g, router_w_group, router_b_group, router_w_expert, router_b_expert,
               expert_w_gate, expert_w_up, expert_w_down)
    B, S, D = x.shape
    assert B == 1 and D == D_MODEL and S % 1024 == 0
    h = x.reshape(S, D)
    stacked = dict(zip(_LAYER_KEYS, stacked))
    for l in range(attn_norm_g.shape[0]):
        p = {k: v[l] for k, v in stacked.items() if not k.startswith("expert_w")}
        h = mixer(h, p, S)
        h = moe(h, p, stacked, l, S)
    return final_norm(h, final_norm_g[None, :], min(512, S)).reshape(B, S, D)
```

```python
import functools

import numpy as np
import jax
import jax.numpy as jnp
from jax import lax
from jax.experimental import pallas as pl
from jax.experimental.pallas import tpu as pltpu

F32 = jnp.float32
BF16 = jnp.bfloat16

D_MODEL = 2048
HEAD_DIM = 64
NSA_WIDTH = D_MODEL // 2
NSA_HEADS = NSA_WIDTH // HEAD_DIM
NSA_KV_HEADS = NSA_HEADS // 4
HPG = NSA_HEADS // NSA_KV_HEADS
NSA_KV_WIDTH = NSA_KV_HEADS * HEAD_DIM
CMP_BLOCK = 32
CMP_STRIDE = 16
CMP_HIDDEN = 4 * HEAD_DIM
SEL_BLOCK = 64
SEL_TOPK = 16
WINDOW = 512
Q_BLOCK = 128
SEL_FORCE = 1.0e4
MLSTM_WIDTH = D_MODEL // 4
MLSTM_HEADS = 4
MLSTM_HEAD_DIM = MLSTM_WIDTH // MLSTM_HEADS
MLSTM_CONV = 4
CONV_CHANNELS = D_MODEL // 4
CONV_WIDTH = 31
N_GROUPS = 4
EXPERTS_PER_GROUP = 8
N_EXPERTS = N_GROUPS * EXPERTS_PER_GROUP
EXPERT_HIDDEN = D_MODEL // 4
NORM_EPS = 1e-6
LN_EPS = 1e-5

NEG = -1.0e30
QL = HPG * Q_BLOCK
MLSTM_CHUNK = 256
MOE_ROWS = 256
VMEM_LIMIT = 52 * 1024 * 1024

RM_QK, RM_CU, RM_V, RM_O, RM_KC, RM_VC, RM_SMALL = 0, 1024, 2048, 2560, 3072, 3328, 3584
RM_WIDTH = 3840
SM_GATES, SM_I, SM_F = 0, 48, 52
FT_Q, FT_KS, FT_VS, FT_KW, FT_VW = 0, 1024, 1280, 1536, 1792
FT_WIDTH = 2048


def _cparams(sem, vmem=VMEM_LIMIT):
    return pltpu.CompilerParams(dimension_semantics=sem, vmem_limit_bytes=vmem)


def _sigmoid(x):
    return 1.0 / (1.0 + jnp.exp(-x))


def _rms_rows(x, g):
    ms = jnp.mean(x * x, axis=-1, keepdims=True)
    return x * lax.rsqrt(ms + NORM_EPS) * g


def _norm_mm_kernel(x_ref, g_ref, w_ref, o_ref, h_ref):
    @pl.when(pl.program_id(1) == 0)
    def _():
        h_ref[...] = _rms_rows(x_ref[...], g_ref[...]).astype(BF16)

    o_ref[...] = jnp.dot(h_ref[...], w_ref[...], preferred_element_type=F32).astype(o_ref.dtype)


def norm_matmul(x, g, w, tm, tn):
    M, K = x.shape
    N = w.shape[1]
    return pl.pallas_call(
        _norm_mm_kernel,
        out_shape=jax.ShapeDtypeStruct((M, N), F32),
        grid=(M // tm, N // tn),
        in_specs=[pl.BlockSpec((tm, K), lambda i, j: (i, 0)),
                  pl.BlockSpec((1, K), lambda i, j: (0, 0)),
                  pl.BlockSpec((K, tn), lambda i, j: (0, j))],
        out_specs=pl.BlockSpec((tm, tn), lambda i, j: (i, j)),
        scratch_shapes=[pltpu.VMEM((tm, K), BF16)],
        compiler_params=_cparams(("parallel", "arbitrary")),
        name="norm_matmul",
    )(x, g, w)


def _norm_mm_t_kernel(x_ref, g_ref, wt_ref, sc_ref, o_ref, h_ref):
    @pl.when(pl.program_id(1) == 0)
    def _():
        h_ref[...] = _rms_rows(x_ref[...], g_ref[...]).astype(BF16)

    o = lax.dot_general(wt_ref[...], h_ref[...], (((1,), (1,)), ((), ())), preferred_element_type=F32)
    o_ref[...] = (o * sc_ref[...]).astype(o_ref.dtype)


def norm_matmul_t(x, g, wt, row_scale, tm, tn):
    M, K = x.shape
    N = wt.shape[0]
    return pl.pallas_call(
        _norm_mm_t_kernel,
        out_shape=jax.ShapeDtypeStruct((N, M), BF16),
        grid=(M // tm, N // tn),
        in_specs=[pl.BlockSpec((tm, K), lambda i, j: (i, 0)),
                  pl.BlockSpec((1, K), lambda i, j: (0, 0)),
                  pl.BlockSpec((tn, K), lambda i, j: (j, 0)),
                  pl.BlockSpec((tn, 1), lambda i, j: (j, 0))],
        out_specs=pl.BlockSpec((tn, tm), lambda i, j: (j, i)),
        scratch_shapes=[pltpu.VMEM((tm, K), BF16)],
        compiler_params=_cparams(("parallel", "arbitrary")),
        name="norm_matmul_t",
    )(x, g, wt, row_scale)


def _out_mm_kernel(a_ref, m_ref, c_ref, x_ref, w_ref, o_ref):
    h = jnp.concatenate([a_ref[...], m_ref[...], c_ref[...]], axis=-1).astype(BF16)
    o_ref[...] = x_ref[...] + jnp.dot(h, w_ref[...], preferred_element_type=F32)


def out_proj(a, m, c, x, w, tm, tn):
    M = x.shape[0]
    N = w.shape[1]
    return pl.pallas_call(
        _out_mm_kernel,
        out_shape=jax.ShapeDtypeStruct((M, N), F32),
        grid=(N // tn, M // tm),
        in_specs=[pl.BlockSpec((tm, a.shape[1]), lambda j, i: (i, 0)),
                  pl.BlockSpec((tm, m.shape[1]), lambda j, i: (i, 0)),
                  pl.BlockSpec((tm, c.shape[1]), lambda j, i: (i, 0)),
                  pl.BlockSpec((tm, tn), lambda j, i: (i, j)),
                  pl.BlockSpec((w.shape[0], tn), lambda j, i: (0, j))],
        out_specs=pl.BlockSpec((tm, tn), lambda j, i: (i, j)),
        compiler_params=_cparams(("parallel", "parallel")),
        name="out_proj",
    )(a, m, c, x, w)


def _compress_kernel(blk_ref, pos_ref, w1_ref, w2t_ref, o_ref):
    x = (blk_ref[0, 0] + pos_ref[0]).astype(BF16)
    hid = jnp.dot(x, w1_ref[0], preferred_element_type=F32)
    hid = hid * _sigmoid(hid)
    o_ref[0, 0] = lax.dot_general(w2t_ref[0], hid.astype(BF16), (((1,), (1,)), ((), ())),
                                  preferred_element_type=F32).astype(o_ref.dtype)


def compress(blk, pos, w1, w2t):
    _, G, NCP, LD = blk.shape
    return pl.pallas_call(
        _compress_kernel,
        out_shape=jax.ShapeDtypeStruct((2, G, HEAD_DIM, NCP), BF16),
        grid=(2, G),
        in_specs=[pl.BlockSpec((1, 1, NCP, LD), lambda a, g: (a, g, 0, 0)),
                  pl.BlockSpec((1, 1, LD), lambda a, g: (a, 0, 0)),
                  pl.BlockSpec((1, LD, CMP_HIDDEN), lambda a, g: (a, 0, 0)),
                  pl.BlockSpec((1, HEAD_DIM, CMP_HIDDEN), lambda a, g: (a, 0, 0))],
        out_specs=pl.BlockSpec((1, 1, HEAD_DIM, NCP), lambda a, g: (a, g, 0, 0)),
        compiler_params=_cparams(("parallel", "parallel")),
        name="nsa_compress",
    )(blk, pos, w1, w2t)


def _tdot(a, b):
    return lax.dot_general(a, b, (((0,), (0,)), ((), ())), preferred_element_type=F32)


def _cmp_attn_kernel(q_ref, kct_ref, vct_ref, sl_ref, ov_ref, oc_ref, sel_ref, cnt_ref, imp_ref, *, n_sel, cch):
    qb = pl.program_id(1)
    gp = kct_ref.shape[0]
    q_augs = []
    for j in range(gp):
        q = q_ref[j * HPG * HEAD_DIM:(j + 1) * HPG * HEAD_DIM, :]
        qt = jnp.concatenate([q[h * HEAD_DIM:(h + 1) * HEAD_DIM, :] for h in range(HPG)], axis=1)
        aug = jnp.concatenate([sl_ref[j], jnp.zeros((8, QL), F32)], axis=0).astype(BF16)
        q_augs.append(jnp.concatenate([qt, aug], axis=0))
    ncp = kct_ref.shape[2]
    n_chunks = ncp // cch
    per_qb = Q_BLOCK // CMP_STRIDE
    last_end = CMP_BLOCK - 1
    need = jnp.minimum((qb * per_qb + per_qb - 2) // cch + 1, n_chunks)
    qoff = lax.broadcasted_iota(jnp.int32, (1, QL), 1) & (Q_BLOCK - 1)
    col_ok = (qb * Q_BLOCK + qoff) >= last_end

    for c in range(1, n_chunks + 1):
        @pl.when(need == c)
        def _(c=c):
            R = c * cch
            r8 = lax.broadcasted_iota(jnp.int32, (8, R), 0)
            n8 = lax.broadcasted_iota(jnp.int32, (8, R), 1)
            hi = ((n8 // per_qb - qb) * Q_BLOCK).astype(F32)
            lo = ((n8 % per_qb) * CMP_STRIDE + last_end).astype(F32)
            pos_rows = jnp.where(r8 < 3, hi, jnp.where(r8 < 6, lo, 0.0))
            k_rows = jnp.concatenate([pos_rows, jnp.zeros((8, R), F32)], axis=0).astype(BF16)
            t0 = max(R - 2 * cch, 0)
            n_io = t0 + lax.broadcasted_iota(jnp.int32, (R - t0, QL), 0)
            visible = n_io * CMP_STRIDE + last_end - qb * Q_BLOCK <= qoff
            ov = ov_ref[:, 0:R]
            for j in range(gp):
                s = _tdot(jnp.concatenate([kct_ref[j, :, 0:R], k_rows], axis=0), q_augs[j])
                tail = jnp.where(visible, s[t0:R], NEG)
                s = tail if t0 == 0 else jnp.concatenate([s[0:t0], tail], axis=0)
                m = jnp.max(s, axis=0, keepdims=True)
                p = jnp.exp2(s - m)
                l = jnp.maximum(jnp.sum(p, axis=0, keepdims=True), 1e-30)
                p = p * jnp.where(col_ok, 1.0 / l, 0.0)
                oc_ref[j, 0] = jnp.dot(vct_ref[j, :, 0:R], p.astype(BF16), preferred_element_type=F32)
                ps = p[:, 0:Q_BLOCK]
                for h in range(1, HPG):
                    ps = ps + p[:, h * Q_BLOCK:(h + 1) * Q_BLOCK]
                hi_p = ps.astype(BF16)
                r1 = ps - hi_p.astype(F32)
                mid_p = r1.astype(BF16)
                lo_p = (r1 - mid_p.astype(F32)).astype(BF16)
                imp_ref[j] = (jnp.dot(ov, hi_p, preferred_element_type=F32)
                              + jnp.dot(ov, mid_p, preferred_element_type=F32)
                              + jnp.dot(ov, lo_p, preferred_element_type=F32))

    n_blk = imp_ref.shape[1]
    j_io = lax.broadcasted_iota(jnp.int32, (n_blk, Q_BLOCK), 0)
    tq = qb * Q_BLOCK + lax.broadcasted_iota(jnp.int32, (n_blk, Q_BLOCK), 1)
    cur = tq // SEL_BLOCK
    forced = (j_io == 0) | (j_io == cur) | (j_io == cur - 1)
    for j in range(gp):
        v = jnp.where(forced, -jnp.inf, jnp.where(j_io <= cur, imp_ref[j], -SEL_FORCE))
        sel = jnp.where(forced, 1.0, 0.0)
        for _ in range(n_sel - 3):
            mx = jnp.max(v, axis=0, keepdims=True)
            idx = jnp.min(jnp.where(v == mx, j_io, n_blk), axis=0, keepdims=True)
            pick = j_io == idx
            v = jnp.where(pick, -jnp.inf, v)
            sel = jnp.where(pick, 1.0, sel)
        live = (sel > 0.5) & (j_io <= cur)
        sel_ref[j, 0, 0:n_blk, :] = jnp.where(live, 0.0, NEG)
        sel_ref[j, 0, n_blk:n_blk + 8, :] = jnp.full((8, Q_BLOCK), NEG, F32)
        cnt_ref[j, 0] = lax.dot_general(jnp.ones((8, Q_BLOCK), BF16), jnp.where(live, 1.0, 0.0).astype(BF16),
                                        (((1,), (1,)), ((), ())), preferred_element_type=F32)


def cmp_attention(feat_t, kc_t, vc_t, slope_rows, ov_t, S):
    G = NSA_KV_HEADS
    n_qb = S // Q_BLOCK
    n_blk = S // SEL_BLOCK
    ncp = kc_t.shape[2]
    n_sel = min(SEL_TOPK, n_blk)
    gp = G
    return pl.pallas_call(
        functools.partial(_cmp_attn_kernel, n_sel=n_sel, cch=min(128, ncp)),
        out_shape=(jax.ShapeDtypeStruct((G, n_qb, HEAD_DIM, QL), F32),
                   jax.ShapeDtypeStruct((G, n_qb, n_blk + 8, Q_BLOCK), F32),
                   jax.ShapeDtypeStruct((G, n_qb, 8, n_blk), F32)),
        grid=(G // gp, n_qb),
        in_specs=[pl.BlockSpec((gp * HPG * HEAD_DIM, Q_BLOCK), lambda g, i: (g, i)),
                  pl.BlockSpec((gp, HEAD_DIM, ncp), lambda g, i: (g, 0, 0)),
                  pl.BlockSpec((gp, HEAD_DIM, ncp), lambda g, i: (g, 0, 0)),
                  pl.BlockSpec((gp, 8, QL), lambda g, i: (g, 0, 0)),
                  pl.BlockSpec((n_blk, ncp), lambda g, i: (0, 0))],
        out_specs=(pl.BlockSpec((gp, 1, HEAD_DIM, QL), lambda g, i: (g, i, 0, 0)),
                   pl.BlockSpec((gp, 1, n_blk + 8, Q_BLOCK), lambda g, i: (g, i, 0, 0)),
                   pl.BlockSpec((gp, 1, 8, n_blk), lambda g, i: (g, i, 0, 0))),
        scratch_shapes=[pltpu.VMEM((gp, n_blk, Q_BLOCK), F32)],
        compiler_params=_cparams(("parallel", "parallel")),
        name="nsa_cmp_select",
    )(feat_t, kc_t, vc_t, slope_rows, ov_t)


KT = 128
SUP = 4
AUG = 16


def _sel_win_kernel(ids_ref, nsup_ref, q_ref, ks_ref, vs_ref, kw_ref, vw_ref, selb_ref, oc_ref, sl_ref, gate_ref,
                    gain_ref, o_ref, *, nt, gp):
    qb = pl.program_id(1)
    n_qb = pl.num_programs(1)

    def k_aug(kt, hi_lane, with_blocks):
        n = kt.shape[1]
        r = lax.broadcasted_iota(jnp.int32, (8, n), 0)
        lane = lax.broadcasted_iota(jnp.int32, (8, n), 1)
        lo = (lane & (KT - 1)).astype(F32)
        pos_rows = jnp.where(r < 3, hi_lane, jnp.where(r < 6, lo, 0.0))
        if with_blocks:
            blk_rows = jnp.where((lane // SEL_BLOCK) == r, 1.0, 0.0)
        else:
            blk_rows = jnp.zeros((8, n), F32)
        return jnp.concatenate([kt, jnp.concatenate([pos_rows, blk_rows], axis=0).astype(BF16)], axis=0)

    def v_aug(vt):
        n = vt.shape[1]
        ones = jnp.where(lax.broadcasted_iota(jnp.int32, (AUG, n), 0) == 0, 1.0, 0.0).astype(BF16)
        return jnp.concatenate([vt, ones], axis=0)

    def tdot(a, b):
        return lax.dot_general(a, b, (((0,), (0,)), ((), ())), preferred_element_type=F32)

    row_io = lax.broadcasted_iota(jnp.int32, (KT, QL), 0)
    qoff = lax.broadcasted_iota(jnp.int32, (KT, QL), 1) & (Q_BLOCK - 1)
    lane_s = lax.broadcasted_iota(jnp.int32, (1, SUP * KT), 1)
    nw = WINDOW // KT + 1
    tiles_w = [qb - (nw - 1) + i for i in range(nw)]
    k0w = [pl.multiple_of(jnp.maximum(t, 0) * KT, KT) for t in tiles_w]
    hi_w = ((lax.broadcasted_iota(jnp.int32, (1, nw * KT), 1) // KT - (nw - 1)) * KT).astype(F32)
    k0d = pl.multiple_of(qb * KT, KT)
    zero_hi = jnp.zeros((1, KT), F32)

    def make_group(j):
        g = pl.program_id(0) * gp + j
        rows = slice(j * HEAD_DIM, (j + 1) * HEAD_DIM)
        q = q_ref[j * HPG * HEAD_DIM:(j + 1) * HPG * HEAD_DIM, :]
        qt = jnp.concatenate([q[h * HEAD_DIM:(h + 1) * HEAD_DIM, :] for h in range(HPG)], axis=1)
        slope_rows = sl_ref[j]

        def q_aug(mask):
            aug = jnp.concatenate([slope_rows, mask], axis=0).astype(BF16)
            return jnp.concatenate([qt, aug], axis=0)

        def mask_rows(block_ids):
            r = lax.broadcasted_iota(jnp.int32, (8, Q_BLOCK), 0)
            out = jnp.zeros((8, Q_BLOCK), F32)
            for i, b in enumerate(block_ids):
                row = jnp.broadcast_to(selb_ref[j, 0, pl.ds(b, 1), :], (8, Q_BLOCK))
                out = jnp.where(r == i, row, out)
            return jnp.concatenate([out] * HPG, axis=1)

        ktw = jnp.concatenate([kw_ref[rows, pl.ds(k0, KT)] for k0 in k0w], axis=1)
        vtw = jnp.concatenate([vw_ref[rows, pl.ds(k0, KT)] for k0 in k0w], axis=1)
        s = tdot(k_aug(ktw, hi_w, False), q_aug(jnp.zeros((8, QL), F32)))
        parts = []
        for i in range(nw):
            si = s[i * KT:(i + 1) * KT]
            if i == 0:
                si = jnp.where((row_io > qoff) & (tiles_w[i] >= 0), si, NEG)
            elif i == nw - 1:
                si = jnp.where(row_io <= qoff, si, NEG)
            else:
                si = jnp.where(tiles_w[i] >= 0, si, NEG)
            parts.append(si)
        s = jnp.concatenate(parts, axis=0)
        m_w = jnp.max(s, axis=0, keepdims=True)
        acc_w = jnp.dot(v_aug(vtw), jnp.exp2(s - m_w).astype(BF16), preferred_element_type=F32)

        s = tdot(k_aug(ks_ref[rows, pl.ds(k0d, KT)], zero_hi, True), q_aug(mask_rows([2 * qb, 2 * qb + 1])))
        s = jnp.where(row_io <= qoff, s, NEG)
        m_s = jnp.max(s, axis=0, keepdims=True)
        acc_s = jnp.dot(v_aug(vs_ref[rows, pl.ds(k0d, KT)]), jnp.exp2(s - m_s).astype(BF16),
                        preferred_element_type=F32)

        def update(si, carry):
            m, acc = carry
            base = (g * n_qb + qb) * nt + si * SUP
            tids = [ids_ref[base + i] for i in range(SUP)]
            tcl = [jnp.minimum(t, nt - 1) for t in tids]
            k0s = [pl.multiple_of(t * KT, KT) for t in tcl]
            kt = jnp.concatenate([ks_ref[rows, pl.ds(k0, KT)] for k0 in k0s], axis=1)
            vt = jnp.concatenate([vs_ref[rows, pl.ds(k0, KT)] for k0 in k0s], axis=1)
            hi = (tcl[SUP - 1] - qb) * KT
            for i in range(SUP - 2, -1, -1):
                hi = jnp.where(lane_s < (i + 1) * KT, (tcl[i] - qb) * KT, hi)
            blocks = [2 * t + b for t in tids for b in range(2)]
            s = tdot(k_aug(kt, hi.astype(F32), True), q_aug(mask_rows(blocks)))
            m_new = jnp.maximum(m, jnp.max(s, axis=0, keepdims=True))
            p = jnp.exp2(s - m_new).astype(BF16)
            acc = jnp.exp2(m - m_new) * acc + jnp.dot(v_aug(vt), p, preferred_element_type=F32)
            return m_new, acc

        def finish(carry):
            _, acc = carry
            gts = _sigmoid(gate_ref[j, 0])
            o_s = acc[0:HEAD_DIM] / acc[HEAD_DIM:HEAD_DIM + 1]
            o_w = acc_w[0:HEAD_DIM] / acc_w[HEAD_DIM:HEAD_DIM + 1]
            o = gts[0:1] * oc_ref[j, 0] + gts[1:2] * o_s + gts[2:3] * o_w
            ms = jnp.mean(o * o, axis=0, keepdims=True)
            y = o * lax.rsqrt(ms + NORM_EPS) * gain_ref[j]
            yt = jnp.concatenate([y[:, h * Q_BLOCK:(h + 1) * Q_BLOCK] for h in range(HPG)], axis=0)
            o_ref[:, j * HPG * HEAD_DIM:(j + 1) * HPG * HEAD_DIM] = yt.T

        return (m_s, acc_s), nsup_ref[g * n_qb + qb], update, finish

    groups = [make_group(j) for j in range(gp)]
    n_iter = groups[0][1]
    for grp in groups[1:]:
        n_iter = jnp.maximum(n_iter, grp[1])
    carries = lax.fori_loop(0, n_iter, lambda si, cs: tuple(grp[2](si, c) for grp, c in zip(groups, cs)),
                            tuple(grp[0] for grp in groups))
    for grp, c in zip(groups, carries):
        grp[3](c)


def sel_win_attention(tile_ids, n_sup, feat_t, selb, oc_t, slope_rows, gates_t, gain_t, S):
    G = NSA_KV_HEADS
    gp = 2
    n_qb = S // Q_BLOCK
    n_blk = S // SEL_BLOCK
    nt = S // KT
    kv_rows = gp * HEAD_DIM
    kv_spec = lambda base: pl.BlockSpec((kv_rows, S), lambda g, i, ids, ns: (base // kv_rows + g, 0))
    grid_spec = pltpu.PrefetchScalarGridSpec(
        num_scalar_prefetch=2,
        grid=(G // gp, n_qb),
        in_specs=[pl.BlockSpec((gp * HPG * HEAD_DIM, Q_BLOCK), lambda g, i, ids, ns: (g, i)),
                  kv_spec(FT_KS), kv_spec(FT_VS), kv_spec(FT_KW), kv_spec(FT_VW),
                  pl.BlockSpec((gp, 1, n_blk + 8, Q_BLOCK), lambda g, i, ids, ns: (g, i, 0, 0)),
                  pl.BlockSpec((gp, 1, HEAD_DIM, QL), lambda g, i, ids, ns: (g, i, 0, 0)),
                  pl.BlockSpec((gp, 8, QL), lambda g, i, ids, ns: (g, 0, 0)),
                  pl.BlockSpec((gp, 1, 3, QL), lambda g, i, ids, ns: (g, i, 0, 0)),
                  pl.BlockSpec((gp, HEAD_DIM, QL), lambda g, i, ids, ns: (g, 0, 0))],
        out_specs=pl.BlockSpec((Q_BLOCK, gp * HPG * HEAD_DIM), lambda g, i, ids, ns: (i, g)),
    )
    return pl.pallas_call(
        functools.partial(_sel_win_kernel, nt=nt, gp=gp),
        out_shape=jax.ShapeDtypeStruct((S, NSA_WIDTH), F32),
        grid_spec=grid_spec,
        compiler_params=_cparams(("parallel", "parallel")),
        name="nsa_sel_win",
    )(tile_ids, n_sup, feat_t, feat_t, feat_t, feat_t, feat_t, selb, oc_t, slope_rows, gates_t, gain_t)


def _conv_kernel(*refs, width, glu, post):
    if post == "ln_silu":
        x_ref, halo_ref, w_ref, b_ref, lg_ref, lb_ref, o_ref, u_ref = refs
    else:
        x_ref, halo_ref, w_ref, b_ref, o_ref, u_ref = refs
    i = pl.program_id(0)
    T = o_ref.shape[0]
    H = halo_ref.shape[0]
    C = o_ref.shape[1]

    def pre(v):
        return v[:, :C] * _sigmoid(v[:, C:]) if glu else v

    u_ref[0:H, :] = jnp.where(i > 0, pre(halo_ref[...]), 0.0)
    u_ref[H:H + T, :] = pre(x_ref[...])
    acc = jnp.broadcast_to(b_ref[...], (T, C))
    base = H - (width - 1)
    for k in range(width):
        acc = acc + w_ref[k:k + 1, :] * u_ref[base + k:base + k + T, :]
    if post == "ln_silu":
        mu = jnp.mean(acc, axis=-1, keepdims=True)
        xc = acc - mu
        var = jnp.mean(xc * xc, axis=-1, keepdims=True)
        acc = xc * lax.rsqrt(var + LN_EPS) * lg_ref[...] + lb_ref[...]
    o_ref[...] = acc * _sigmoid(acc)


def causal_conv(x, col_block, w, b, ln=None, *, glu, T):
    S = x.shape[0]
    width, C = w.shape
    cin = 2 * C if glu else C
    H = -(-(width - 1) // 8) * 8
    post = "ln_silu" if ln is not None else "silu"
    in_specs = [pl.BlockSpec((T, cin), lambda i: (i, col_block)),
                pl.BlockSpec((H, cin), lambda i: (jnp.maximum(i * (T // H) - 1, 0), col_block)),
                pl.BlockSpec((width, C), lambda i: (0, 0)),
                pl.BlockSpec((1, C), lambda i: (0, 0))]
    args = [x, x, w, b]
    if ln is not None:
        in_specs += [pl.BlockSpec((1, C), lambda i: (0, 0))] * 2
        args += list(ln)
    return pl.pallas_call(
        functools.partial(_conv_kernel, width=width, glu=glu, post=post),
        out_shape=jax.ShapeDtypeStruct((S, C), F32),
        grid=(S // T,),
        in_specs=in_specs,
        out_specs=pl.BlockSpec((T, C), lambda i: (i, 0)),
        scratch_shapes=[pltpu.VMEM((H + T, C), F32)],
        compiler_params=_cparams(("parallel",)),
        name="causal_conv_glu" if glu else "causal_conv",
    )(*args)


def _log_sigmoid(x):
    return jnp.minimum(x, 0.0) - jnp.log(1.0 + jnp.exp(-jnp.abs(x)))


def _mlstm_kernel(qk_ref, v_ref, o_ref, sm_ref, gt_ref, bcol_ref, brow_ref, gain_ref, out_ref, c_ref, m_ref):
    L = qk_ref.shape[0]
    DH = MLSTM_HEAD_DIM
    W = MLSTM_WIDTH

    @pl.when(pl.program_id(0) == 0)
    def _():
        c_ref[...] = jnp.zeros_like(c_ref)
        m_ref[...] = jnp.zeros_like(m_ref)

    small = sm_ref[...]
    lane = lax.broadcasted_iota(jnp.int32, small.shape, 1)
    t_io = lax.broadcasted_iota(jnp.int32, (L, L), 0)
    s_io = lax.broadcasted_iota(jnp.int32, (L, L), 1)
    causal = s_io <= t_io
    ones_col = (lax.broadcasted_iota(jnp.int32, (L, DH), 1) == 0).astype(BF16)

    def column(c):
        return jnp.sum(jnp.where(lane == c, small, 0.0), axis=1, keepdims=True)

    for h in range(MLSTM_HEADS):
        q = qk_ref[:, h * DH:(h + 1) * DH].astype(BF16)
        k = qk_ref[:, W + h * DH:W + (h + 1) * DH] * (DH ** -0.5)
        v = v_ref[:, h * DH:(h + 1) * DH].astype(BF16)
        vaug = jnp.concatenate([v, ones_col], axis=1)
        i_col = column(SM_I + h) + bcol_ref[h:h + 1, :]
        f_col = column(SM_F + h) + bcol_ref[MLSTM_HEADS + h:MLSTM_HEADS + h + 1, :]
        i_row = gt_ref[h:h + 1, :] + brow_ref[h:h + 1, :]
        f_row = gt_ref[MLSTM_HEADS + h:MLSTM_HEADS + h + 1, :] + brow_ref[MLSTM_HEADS + h:MLSTM_HEADS + h + 1, :]
        lf_col = _log_sigmoid(f_col)
        lf_row = _log_sigmoid(f_row)
        b_col = jnp.sum(jnp.where(causal, lf_row, 0.0), axis=1, keepdims=True)
        b_row = jnp.sum(jnp.where(t_io <= s_io, lf_col, 0.0), axis=0, keepdims=True)
        b_end = jnp.sum(lf_row, axis=1, keepdims=True)
        m0 = m_ref[h:h + 1, 0:1]
        c0 = c_ref[h]

        dmat = jnp.where(causal, b_col - b_row + i_row, -jnp.inf)
        inter = b_col + m0
        m_t = jnp.maximum(inter, jnp.max(dmat, axis=1, keepdims=True))
        qk = lax.dot_general(q, k.astype(BF16), (((1,), (1,)), ((), ())), preferred_element_type=F32)
        smat = qk * jnp.exp(dmat - m_t)
        w_inter = jnp.exp(inter - m_t)
        r = (jnp.dot(smat.astype(BF16), vaug, preferred_element_type=F32)
             + w_inter * jnp.dot(q, c0.astype(BF16), preferred_element_type=F32))
        num = r[:, :DH]
        den = jnp.sum(r[:, DH:], axis=1, keepdims=True)
        hh = num / jnp.maximum(jnp.abs(den), jnp.exp(-m_t))

        a_row = b_end - b_row + i_row
        a_col = b_end - b_col + i_col
        m_new = jnp.maximum(b_end + m0, jnp.max(a_row, axis=1, keepdims=True))
        kw = (k * jnp.exp(a_col - m_new)).astype(BF16)
        c_ref[h] = jnp.exp(b_end + m0 - m_new) * c0 + lax.dot_general(
            kw, vaug, (((0,), (0,)), ((), ())), preferred_element_type=F32)
        m_ref[h:h + 1, :] = jnp.broadcast_to(m_new, (1, m_ref.shape[1]))

        y = _sigmoid(o_ref[:, h * DH:(h + 1) * DH]) * hh
        ms = jnp.mean(y * y, axis=-1, keepdims=True)
        out_ref[:, h * DH:(h + 1) * DH] = y * lax.rsqrt(ms + NORM_EPS) * gain_ref[:, h * DH:(h + 1) * DH]


def mlstm(qk, rm, gates_t, bias_col, bias_row, gain, S):
    L = min(MLSTM_CHUNK, S)
    W = MLSTM_WIDTH
    return pl.pallas_call(
        _mlstm_kernel,
        out_shape=jax.ShapeDtypeStruct((S, W), F32),
        grid=(S // L,),
        in_specs=[pl.BlockSpec((L, 2 * W), lambda c: (c, 0)),
                  pl.BlockSpec((L, W), lambda c: (c, RM_V // W)),
                  pl.BlockSpec((L, W), lambda c: (c, RM_O // W)),
                  pl.BlockSpec((L, 128), lambda c: (c, RM_SMALL // 128)),
                  pl.BlockSpec((2 * MLSTM_HEADS, L), lambda c: (0, c)),
                  pl.BlockSpec((2 * MLSTM_HEADS, 1), lambda c: (0, 0)),
                  pl.BlockSpec((2 * MLSTM_HEADS, 1), lambda c: (0, 0)),
                  pl.BlockSpec((1, W), lambda c: (0, 0))],
        out_specs=pl.BlockSpec((L, W), lambda c: (c, 0)),
        scratch_shapes=[pltpu.VMEM((MLSTM_HEADS, MLSTM_HEAD_DIM, 2 * MLSTM_HEAD_DIM), F32),
                        pltpu.VMEM((8, 128), F32)],
        compiler_params=_cparams(("arbitrary",)),
        name="mlstm",
    )(qk, rm, rm, rm, gates_t, bias_col, bias_row, gain)


def _router_kernel(x_ref, g_ref, wr_ref, br_ref, xn_ref, eid_ref, gate_ref):
    xn = _rms_rows(x_ref[...], g_ref[...])
    xn_ref[:, 0, :] = xn
    logits = lax.dot_general(wr_ref[...], xn, (((1,), (1,)), ((), ())), precision=lax.Precision.HIGHEST,
                             preferred_element_type=F32) + br_ref[...]
    tm = logits.shape[1]
    lg = logits[N_EXPERTS:N_EXPERTS + N_GROUPS, :]
    eg = jnp.exp(lg - jnp.max(lg, axis=0, keepdims=True))
    pg = eg / jnp.sum(eg, axis=0, keepdims=True)
    pg_top = jnp.max(pg, axis=0, keepdims=True)
    g_io = lax.broadcasted_iota(jnp.int32, (N_GROUPS, tm), 0)
    grp = jnp.min(jnp.where(pg == pg_top, g_io, N_GROUPS), axis=0, keepdims=True)
    le = logits[0:EXPERTS_PER_GROUP, :]
    for g in range(1, N_GROUPS):
        le = jnp.where(grp == g, logits[g * EXPERTS_PER_GROUP:(g + 1) * EXPERTS_PER_GROUP, :], le)
    ee = jnp.exp(le - jnp.max(le, axis=0, keepdims=True))
    pe = ee / jnp.sum(ee, axis=0, keepdims=True)
    e_io = lax.broadcasted_iota(jnp.int32, (EXPERTS_PER_GROUP, tm), 0)
    p1 = jnp.max(pe, axis=0, keepdims=True)
    i1 = jnp.min(jnp.where(pe == p1, e_io, EXPERTS_PER_GROUP), axis=0, keepdims=True)
    pe2 = jnp.where(e_io == i1, -1.0, pe)
    p2 = jnp.max(pe2, axis=0, keepdims=True)
    i2 = jnp.min(jnp.where(pe2 == p2, e_io, EXPERTS_PER_GROUP), axis=0, keepdims=True)
    tot = p1 + p2
    eid_ref[...] = jnp.concatenate([grp * EXPERTS_PER_GROUP + i1, grp * EXPERTS_PER_GROUP + i2], axis=0)
    gate_ref[...] = jnp.concatenate([pg_top * p1 / tot, pg_top * p2 / tot], axis=0)


def router(x, g, wr_t, br_t, tm):
    S, D = x.shape
    R = wr_t.shape[0]
    return pl.pallas_call(
        _router_kernel,
        out_shape=(jax.ShapeDtypeStruct((S, 1, D), F32),
                   jax.ShapeDtypeStruct((2, S), jnp.int32),
                   jax.ShapeDtypeStruct((2, S), F32)),
        grid=(S // tm,),
        in_specs=[pl.BlockSpec((tm, D), lambda i: (i, 0)),
                  pl.BlockSpec((1, D), lambda i: (0, 0)),
                  pl.BlockSpec((R, D), lambda i: (0, 0)),
                  pl.BlockSpec((R, 1), lambda i: (0, 0))],
        out_specs=(pl.BlockSpec((tm, 1, D), lambda i: (i, 0, 0)),
                   pl.BlockSpec((2, tm), lambda i: (0, i)),
                   pl.BlockSpec((2, tm), lambda i: (0, i))),
        compiler_params=_cparams(("parallel",)),
        name="moe_router",
    )(x, g, wr_t, br_t)


def _row_gather(idx_ref, idx0, stride, src_hbm, dst, sem, n_rows):
    def body(r, carry):
        i = idx_ref[idx0 + r * stride]
        pltpu.make_async_copy(src_hbm.at[i], dst.at[pl.ds(r, 1), :], sem).start()
        return carry

    lax.fori_loop(0, n_rows, body, 0, unroll=8)


def _rows_wait(src_hbm, dst, sem):
    pltpu.make_async_copy(src_hbm.at[pl.ds(0, dst.shape[0]), 0], dst, sem).wait()


def _expert_kernel(be_ref, nu_ref, tok_ref, dst_ref, xn_hbm, wg_ref, wu_ref, wd_ref, y_hbm,
                   xbuf0, xbuf1, obuf, gsem, ssem, wg_s, wu_s, wd_s):
    b = pl.program_id(0)
    n_used = nu_ref[0]
    e = be_ref[b]
    prev = be_ref[jnp.maximum(b - 1, 0)]
    spare0 = y_hbm.shape[0] - MOE_ROWS

    def scatter_wait():
        pltpu.make_async_copy(obuf, y_hbm.at[pl.ds(0, MOE_ROWS), 0], ssem.at[0]).wait()

    xbufs = (xbuf0, xbuf1)

    @pl.when(b == 0)
    def _():
        _row_gather(tok_ref, 0, 1, xn_hbm, xbuf0, gsem.at[0], MOE_ROWS)
        obuf[...] = jnp.zeros_like(obuf)
        pltpu.make_async_copy(obuf, y_hbm.at[pl.ds(spare0, MOE_ROWS), 0], ssem.at[0]).start()

    @pl.when((b == 0) | (e != prev))
    def _():
        wg_s[...] = wg_ref[0, 0].astype(BF16)
        wu_s[...] = wu_ref[0, 0].astype(BF16)
        wd_s[...] = wd_ref[0, 0].astype(BF16)

    half = MOE_ROWS // 2

    def run_block(slot):
        cur, oth = xbufs[slot], xbufs[1 - slot]
        _rows_wait(xn_hbm, cur, gsem.at[slot])
        nxt = jnp.minimum(b + 1, n_used - 1) * MOE_ROWS

        def ffn(r0):
            x = cur[r0:r0 + half, :].astype(BF16)
            hg = jnp.dot(x, wg_s[...], preferred_element_type=F32)
            hu = jnp.dot(x, wu_s[...], preferred_element_type=F32)
            hb = (hg * _sigmoid(hg) * hu).astype(BF16)
            return jnp.dot(hb, wd_s[...], preferred_element_type=F32)

        def gather_next(r0):
            for r in range(r0, r0 + half):
                pltpu.make_async_copy(xn_hbm.at[tok_ref[nxt + r]], oth.at[pl.ds(r, 1), :],
                                      gsem.at[1 - slot]).start()

        def scatter(r0):
            for r in range(r0, r0 + half):
                pltpu.make_async_copy(obuf.at[pl.ds(r, 1), :], y_hbm.at[dst_ref[b * MOE_ROWS + r]],
                                      ssem.at[0]).start()

        gather_next(0)
        y0 = ffn(0)
        scatter_wait()
        obuf[0:half, :] = y0
        scatter(0)
        gather_next(half)
        obuf[half:MOE_ROWS, :] = ffn(half)
        scatter(half)

    for slot in range(2):
        @pl.when((b < n_used) & (b % 2 == slot))
        def _(slot=slot):
            run_block(slot)

    @pl.when(b == n_used - 1)
    def _():
        scatter_wait()
        for slot in range(2):
            @pl.when((b + 1) % 2 == slot)
            def _(slot=slot):
                _rows_wait(xn_hbm, xbufs[slot], gsem.at[slot])


def expert_ffn(blk_e, n_used, row_tok, row_dst, xn, w_gate, w_up, w_down, layer, n_out):
    P = row_tok.shape[0]
    D = xn.shape[2]
    n_blocks = P // MOE_ROWS
    Hd = w_gate.shape[3]
    grid_spec = pltpu.PrefetchScalarGridSpec(
        num_scalar_prefetch=4,
        grid=(n_blocks,),
        in_specs=[pl.BlockSpec(memory_space=pl.ANY),
                  pl.BlockSpec((1, 1, D, Hd), lambda b, be, nu, tk, ds: (layer, be[b], 0, 0)),
                  pl.BlockSpec((1, 1, D, Hd), lambda b, be, nu, tk, ds: (layer, be[b], 0, 0)),
                  pl.BlockSpec((1, 1, Hd, D), lambda b, be, nu, tk, ds: (layer, be[b], 0, 0))],
        out_specs=pl.BlockSpec(memory_space=pl.ANY),
        scratch_shapes=[pltpu.VMEM((MOE_ROWS, D), F32), pltpu.VMEM((MOE_ROWS, D), F32),
                        pltpu.VMEM((MOE_ROWS, D), F32),
                        pltpu.SemaphoreType.DMA((2,)), pltpu.SemaphoreType.DMA((1,)),
                        pltpu.VMEM((D, Hd), BF16), pltpu.VMEM((D, Hd), BF16), pltpu.VMEM((Hd, D), BF16)],
    )
    return pl.pallas_call(
        _expert_kernel,
        out_shape=jax.ShapeDtypeStruct((n_out + MOE_ROWS, 1, D), F32),
        grid_spec=grid_spec,
        compiler_params=_cparams(("arbitrary",)),
        name="moe_experts",
    )(blk_e, n_used, row_tok, row_dst, xn, w_gate, w_up, w_down)


def _combine_kernel(x_ref, gate_ref, y0_ref, y1_ref, o_ref):
    gate = gate_ref[...]
    o_ref[...] = x_ref[...] + gate[:, 0:1] * y0_ref[:, 0, :] + gate[:, 1:2] * y1_ref[:, 0, :]


def moe_combine(x, gate, y, tt):
    S, D = x.shape
    return pl.pallas_call(
        _combine_kernel,
        out_shape=jax.ShapeDtypeStruct((S, D), F32),
        grid=(S // tt,),
        in_specs=[pl.BlockSpec((tt, D), lambda i: (i, 0)),
                  pl.BlockSpec((tt, 2), lambda i: (i, 0)),
                  pl.BlockSpec((tt, 1, D), lambda i: (i, 0, 0)),
                  pl.BlockSpec((tt, 1, D), lambda i: (S // tt + i, 0, 0))],
        out_specs=pl.BlockSpec((tt, D), lambda i: (i, 0)),
        compiler_params=_cparams(("parallel",)),
        name="moe_combine",
    )(x, gate, y, y)


def _final_norm_kernel(x_ref, g_ref, o_ref):
    o_ref[...] = _rms_rows(x_ref[...], g_ref[...])


def final_norm(x, g, tm):
    S, D = x.shape
    return pl.pallas_call(
        _final_norm_kernel,
        out_shape=jax.ShapeDtypeStruct((S, D), F32),
        grid=(S // tm,),
        in_specs=[pl.BlockSpec((tm, D), lambda i: (i, 0)), pl.BlockSpec((1, D), lambda i: (0, 0))],
        out_specs=pl.BlockSpec((tm, D), lambda i: (i, 0)),
        compiler_params=_cparams(("parallel",)),
        name="final_norm",
    )(x, g)


LOG2E = float(np.log2(np.e))


def _alibi_lanes():
    sl = 2.0 ** (-8.0 * np.arange(1, NSA_HEADS + 1) / NSA_HEADS) * LOG2E
    sl = np.repeat(sl.reshape(NSA_KV_HEADS, HPG, 1), Q_BLOCK, axis=2).reshape(NSA_KV_HEADS, 1, QL)
    sl = jnp.asarray(sl, F32)
    s1 = sl.astype(BF16).astype(F32)
    s2 = (sl - s1).astype(BF16).astype(F32)
    s3 = (sl - s1 - s2).astype(BF16).astype(F32)
    zero = jnp.zeros_like(sl)
    return jnp.concatenate([s1, s2, s3, s1, s2, s3, zero, zero], axis=1)


def _active_tiles(cnt, S):
    G, n_qb, n_blk = cnt.shape
    nt = S // KT
    act = (cnt > 0.5).reshape(G, n_qb, nt, KT // SEL_BLOCK).any(-1)
    tile = jnp.arange(nt, dtype=jnp.int32)
    act = act & (tile[None, None, :] < (jnp.arange(n_qb) * (Q_BLOCK // KT))[None, :, None])
    rank = jnp.cumsum(act.astype(jnp.int32), axis=-1) - 1
    n_act = rank[..., -1] + 1
    hit = act[..., None, :] & (rank[..., None, :] == tile[None, None, :, None])
    ids = jnp.sum(jnp.where(hit, tile[None, None, None, :], 0), axis=-1)
    ids = jnp.where(tile[None, None, :] < n_act[..., None], ids, nt).astype(jnp.int32)
    return ids.reshape(-1), ((n_act + SUP - 1) // SUP).astype(jnp.int32).reshape(-1)


def _overlap_t(n_cmp_pad, n_cmp, n_blk):
    cs = np.arange(n_cmp_pad) * CMP_STRIDE
    ss = np.arange(n_blk) * SEL_BLOCK
    ov = np.minimum(cs[None, :] + CMP_BLOCK, ss[:, None] + SEL_BLOCK) - np.maximum(cs[None, :], ss[:, None])
    ov = np.clip(ov, 0, None) / CMP_STRIDE
    ov[:, n_cmp:] = 0.0
    return jnp.asarray(ov, BF16)


def mixer(x, p, S):
    G = NSA_KV_HEADS
    tm = min(1024, S)
    w_in = p["w_in"]
    c = np.cumsum((0, NSA_WIDTH) + (NSA_KV_WIDTH,) * 6 + (3 * NSA_HEADS,) + (MLSTM_WIDTH,) * 4
                  + (MLSTM_HEADS,) * 2 + (2 * CONV_CHANNELS,))
    (q0, kc0, vc0, ks0, vs0, kw0, vw0, gt0, mq0, mk0, mv0, mo0, mi0, mf0, cu0, end) = [int(v) for v in c]
    w_rm = jnp.concatenate(
        [w_in[:, mq0:mv0], w_in[:, cu0:end], w_in[:, mv0:mi0], w_in[:, kc0:ks0], w_in[:, gt0:mq0],
         w_in[:, mi0:cu0], jnp.zeros((D_MODEL, RM_WIDTH - RM_SMALL - 56), F32)], axis=1).astype(BF16)
    w_ft = jnp.concatenate([w_in[:, q0:kc0], w_in[:, ks0:gt0]], axis=1).T.astype(BF16)
    g = p["attn_norm_g"][None, :]
    rm = norm_matmul(x, g, w_rm, tm, 768)
    q_scale = jnp.where(jnp.arange(FT_WIDTH) < FT_KS, HEAD_DIM ** -0.5 * LOG2E, 1.0).astype(F32)[:, None]
    feat_t = norm_matmul_t(x, g, w_ft, q_scale, tm, 512)

    ncp = S // CMP_STRIDE
    n_cmp = (S - CMP_BLOCK) // CMP_STRIDE + 1
    n_blk = S // SEL_BLOCK
    n_qb = S // Q_BLOCK
    kv = jnp.stack([rm[:, RM_KC:RM_KC + NSA_KV_WIDTH], rm[:, RM_VC:RM_VC + NSA_KV_WIDTH]])
    rows = kv.reshape(2, ncp, CMP_STRIDE, G, HEAD_DIM).transpose(0, 3, 1, 2, 4).reshape(2, G, ncp, -1)
    nxt = jnp.concatenate([rows[:, :, 1:], jnp.zeros_like(rows[:, :, :1])], axis=2)
    blk = jnp.concatenate([rows, nxt], axis=-1)
    pos = jnp.stack([p["cmp_pos_k"].reshape(1, -1), p["cmp_pos_v"].reshape(1, -1)])
    w1 = jnp.stack([p["cmp_w1_k"], p["cmp_w1_v"]]).astype(BF16)
    w2t = jnp.stack([p["cmp_w2_k"].T, p["cmp_w2_v"].T]).astype(BF16)
    kvc_t = compress(blk, pos, w1, w2t)
    slope_rows = _alibi_lanes()
    oc_t, selb, cnt = cmp_attention(feat_t, kvc_t[0], kvc_t[1], slope_rows, _overlap_t(ncp, n_cmp, n_blk), S)
    tile_ids, n_sup = _active_tiles(cnt[:, :, 0, :], S)
    small = rm[:, RM_SMALL:RM_SMALL + 128]
    gates_t = small[:, SM_GATES:SM_GATES + 3 * NSA_HEADS].reshape(n_qb, Q_BLOCK, G, HPG, 3)
    gates_t = gates_t.transpose(2, 0, 4, 3, 1).reshape(G, n_qb, 3, QL)
    gain_t = jnp.broadcast_to(p["nsa_out_g"].reshape(G, HPG, HEAD_DIM, 1).transpose(0, 2, 1, 3),
                              (G, HEAD_DIM, HPG, Q_BLOCK)).reshape(G, HEAD_DIM, QL)
    a_out = sel_win_attention(tile_ids, n_sup, feat_t, selb, oc_t, slope_rows, gates_t, gain_t, S)

    tc = min(512, S)
    qk = causal_conv(rm, RM_QK // (2 * MLSTM_WIDTH), p["mlstm_conv_w"], p["mlstm_conv_b"][None, :],
                     glu=False, T=tc)
    gates_m = small[:, SM_I:SM_I + 2 * MLSTM_HEADS].T
    bias = jnp.concatenate([p["mlstm_i_bias"], p["mlstm_f_bias"]])
    hm = mlstm(qk, rm, gates_m, bias[:, None], bias[:, None], p["mlstm_out_g"][None, :], S)

    cv = causal_conv(rm, RM_CU // (2 * CONV_CHANNELS), p["conv_w"], p["conv_b"][None, :],
                     (p["conv_ln_g"][None, :], p["conv_ln_b"][None, :]), glu=True, T=tc)

    return out_proj(a_out, hm, cv, x, p["w_out"].astype(BF16), min(512, S), 1024)


def moe(x, p, stacked_w, layer, S):
    tm = min(512, S)
    wr_t = jnp.concatenate([p["router_w_expert"].T, p["router_w_group"].T,
                            jnp.zeros((12, D_MODEL), F32)], axis=0)
    br_t = jnp.concatenate([p["router_b_expert"].reshape(-1), p["router_b_group"], jnp.zeros((12,), F32)])[:, None]
    xn, eid, gate = router(x, p["ffn_norm_g"][None, :], wr_t, br_t, tm)

    flat_e = eid.T.reshape(-1)
    A = flat_e.shape[0]
    onehot = (flat_e[:, None] == jnp.arange(N_EXPERTS)[None, :]).astype(jnp.int32)
    rank = jnp.take_along_axis(jnp.cumsum(onehot, axis=0), flat_e[:, None], axis=1)[:, 0] - 1
    counts = jnp.sum(onehot, axis=0)
    padded = (counts + MOE_ROWS - 1) // MOE_ROWS * MOE_ROWS
    pends = jnp.cumsum(padded)
    dest = (pends - padded)[flat_e] + rank
    n_blocks = A // MOE_ROWS + N_EXPERTS
    P = n_blocks * MOE_ROWS
    row_a = jnp.full((P,), -1, jnp.int32).at[dest].set(jnp.arange(A, dtype=jnp.int32), unique_indices=True)
    row_tok = jnp.maximum(row_a, 0) // 2
    spare = 2 * S + jnp.arange(P, dtype=jnp.int32) % MOE_ROWS
    row_dst = jnp.where(row_a >= 0, (row_a % 2) * S + row_a // 2, spare)
    n_used = (pends[-1] // MOE_ROWS).astype(jnp.int32)
    blk_start = jnp.minimum(jnp.arange(n_blocks), n_used - 1) * MOE_ROWS
    blk_e = jnp.minimum(jnp.sum(pends[None, :] <= blk_start[:, None], axis=1), N_EXPERTS - 1).astype(jnp.int32)

    y = expert_ffn(blk_e, n_used[None], row_tok, row_dst, xn, stacked_w["expert_w_gate"],
                   stacked_w["expert_w_up"], stacked_w["expert_w_down"], layer, 2 * S)
    return moe_combine(x, gate.T, y, min(512, S))


_LAYER_KEYS = ("attn_norm_g", "w_in", "cmp_pos_k", "cmp_w1_k", "cmp_w2_k", "cmp_pos_v", "cmp_w1_v", "cmp_w2_v",
               "nsa_out_g", "mlstm_conv_w", "mlstm_conv_b", "mlstm_i_bias", "mlstm_f_bias", "mlstm_out_g",
               "conv_w", "conv_b", "conv_ln_g", "conv_ln_b", "w_out", "ffn_norm_g", "router_w_group",
               "router_b_group", "router_w_expert", "router_b_expert", "expert_w_gate", "expert_w_up",
               "expert_w_down")


def kernel(x, attn_norm_g, w_in, cmp_pos_k, cmp_w1_k, cmp_w2_k, cmp_pos_v, cmp_w1_v, cmp_w2_v, nsa_out_g, mlstm_conv_w, mlstm_conv_b, mlstm_i_bias, mlstm_f_bias, mlstm_out_g, conv_w, conv_b, conv_ln_g, conv_ln_b, w_out, ffn_norm_g, router_w_group, router_b_group, router_w_expert, router_b_expert, expert_w_gate, expert_w_up, expert_w_down, final_norm_g):
    stacked = (attn_norm_g, w_in, cmp_pos_k, cmp_w1_k, cmp_w2_k, cmp_pos_v, cmp_w1_v, cmp_w2_v, nsa_out_g,
               mlstm_conv_w, mlstm_conv_b, mlstm_i_bias, mlstm_f_bias, mlstm_out_g, conv_w, conv_b, conv_ln_g,
               conv_ln_b, w_out, ffn_norm_g, router_w_group, router_b_group, router_w_expert, router_b_expert,
               expert_w_gate, expert_w_up, expert_w_down)
    B, S, D = x.shape
    assert B == 1 and D == D_MODEL and S % 1024 == 0
    h = x.reshape(S, D)
    stacked = dict(zip(_LAYER_KEYS, stacked))
    for l in range(attn_norm_g.shape[0]):
        p = {k: v[l] for k, v in stacked.items() if not k.startswith("expert_w")}
        h = mixer(h, p, S)
        h = moe(h, p, stacked, l, S)
    return final_norm(h, final_norm_g[None, :], min(512, S)).reshape(B, S, D)
```

```python
import functools

import numpy as np
import jax
import jax.numpy as jnp
from jax import lax
from jax.experimental import pallas as pl
from jax.experimental.pallas import tpu as pltpu

F32 = jnp.float32
BF16 = jnp.bfloat16

D_MODEL = 2048
HEAD_DIM = 64
NSA_WIDTH = D_MODEL // 2
NSA_HEADS = NSA_WIDTH // HEAD_DIM
NSA_KV_HEADS = NSA_HEADS // 4
HPG = NSA_HEADS // NSA_KV_HEADS
NSA_KV_WIDTH = NSA_KV_HEADS * HEAD_DIM
CMP_BLOCK = 32
CMP_STRIDE = 16
CMP_HIDDEN = 4 * HEAD_DIM
SEL_BLOCK = 64
SEL_TOPK = 16
WINDOW = 512
Q_BLOCK = 128
SEL_FORCE = 1.0e4
MLSTM_WIDTH = D_MODEL // 4
MLSTM_HEADS = 4
MLSTM_HEAD_DIM = MLSTM_WIDTH // MLSTM_HEADS
MLSTM_CONV = 4
CONV_CHANNELS = D_MODEL // 4
CONV_WIDTH = 31
N_GROUPS = 4
EXPERTS_PER_GROUP = 8
N_EXPERTS = N_GROUPS * EXPERTS_PER_GROUP
EXPERT_HIDDEN = D_MODEL // 4
NORM_EPS = 1e-6
LN_EPS = 1e-5

NEG = -1.0e30
QL = HPG * Q_BLOCK
MLSTM_CHUNK = 256
MOE_ROWS = 256
VMEM_LIMIT = 52 * 1024 * 1024

RM_QK, RM_CU, RM_V, RM_O, RM_KC, RM_VC, RM_SMALL = 0, 1024, 2048, 2560, 3072, 3328, 3584
RM_WIDTH = 3840
SM_GATES, SM_I, SM_F = 0, 48, 52
FT_Q, FT_KS, FT_VS, FT_KW, FT_VW = 0, 1024, 1280, 1536, 1792
FT_WIDTH = 2048


def _cparams(sem, vmem=VMEM_LIMIT):
    return pltpu.CompilerParams(dimension_semantics=sem, vmem_limit_bytes=vmem)


def _sigmoid(x):
    return 1.0 / (1.0 + jnp.exp(-x))


def _rms_rows(x, g):
    ms = jnp.mean(x * x, axis=-1, keepdims=True)
    return x * lax.rsqrt(ms + NORM_EPS) * g


def _norm_mm_kernel(x_ref, g_ref, w_ref, o_ref, h_ref):
    @pl.when(pl.program_id(1) == 0)
    def _():
        h_ref[...] = _rms_rows(x_ref[...], g_ref[...]).astype(BF16)

    o_ref[...] = jnp.dot(h_ref[...], w_ref[...], preferred_element_type=F32).astype(o_ref.dtype)


def norm_matmul(x, g, w, tm, tn):
    M, K = x.shape
    N = w.shape[1]
    return pl.pallas_call(
        _norm_mm_kernel,
        out_shape=jax.ShapeDtypeStruct((M, N), F32),
        grid=(M // tm, N // tn),
        in_specs=[pl.BlockSpec((tm, K), lambda i, j: (i, 0)),
                  pl.BlockSpec((1, K), lambda i, j: (0, 0)),
                  pl.BlockSpec((K, tn), lambda i, j: (0, j))],
        out_specs=pl.BlockSpec((tm, tn), lambda i, j: (i, j)),
        scratch_shapes=[pltpu.VMEM((tm, K), BF16)],
        compiler_params=_cparams(("parallel", "arbitrary")),
        name="norm_matmul",
    )(x, g, w)


def _norm_mm_t_kernel(x_ref, g_ref, wt_ref, sc_ref, o_ref, h_ref):
    @pl.when(pl.program_id(1) == 0)
    def _():
        h_ref[...] = _rms_rows(x_ref[...], g_ref[...]).astype(BF16)

    o = lax.dot_general(wt_ref[...], h_ref[...], (((1,), (1,)), ((), ())), preferred_element_type=F32)
    o_ref[...] = (o * sc_ref[...]).astype(o_ref.dtype)


def norm_matmul_t(x, g, wt, row_scale, tm, tn):
    M, K = x.shape
    N = wt.shape[0]
    return pl.pallas_call(
        _norm_mm_t_kernel,
        out_shape=jax.ShapeDtypeStruct((N, M), BF16),
        grid=(M // tm, N // tn),
        in_specs=[pl.BlockSpec((tm, K), lambda i, j: (i, 0)),
                  pl.BlockSpec((1, K), lambda i, j: (0, 0)),
                  pl.BlockSpec((tn, K), lambda i, j: (j, 0)),
                  pl.BlockSpec((tn, 1), lambda i, j: (j, 0))],
        out_specs=pl.BlockSpec((tn, tm), lambda i, j: (j, i)),
        scratch_shapes=[pltpu.VMEM((tm, K), BF16)],
        compiler_params=_cparams(("parallel", "arbitrary")),
        name="norm_matmul_t",
    )(x, g, wt, row_scale)


def _out_mm_kernel(a_ref, m_ref, c_ref, x_ref, w_ref, o_ref):
    h = jnp.concatenate([a_ref[...], m_ref[...], c_ref[...]], axis=-1).astype(BF16)
    o_ref[...] = x_ref[...] + jnp.dot(h, w_ref[...], preferred_element_type=F32)


def out_proj(a, m, c, x, w, tm, tn):
    M = x.shape[0]
    N = w.shape[1]
    return pl.pallas_call(
        _out_mm_kernel,
        out_shape=jax.ShapeDtypeStruct((M, N), F32),
        grid=(N // tn, M // tm),
        in_specs=[pl.BlockSpec((tm, a.shape[1]), lambda j, i: (i, 0)),
                  pl.BlockSpec((tm, m.shape[1]), lambda j, i: (i, 0)),
                  pl.BlockSpec((tm, c.shape[1]), lambda j, i: (i, 0)),
                  pl.BlockSpec((tm, tn), lambda j, i: (i, j)),
                  pl.BlockSpec((w.shape[0], tn), lambda j, i: (0, j))],
        out_specs=pl.BlockSpec((tm, tn), lambda j, i: (i, j)),
        compiler_params=_cparams(("parallel", "parallel")),
        name="out_proj",
    )(a, m, c, x, w)


def _compress_kernel(blk_ref, pos_ref, w1_ref, w2t_ref, o_ref):
    x = (blk_ref[0, 0] + pos_ref[0]).astype(BF16)
    hid = jnp.dot(x, w1_ref[0], preferred_element_type=F32)
    hid = hid * _sigmoid(hid)
    o_ref[0, 0] = lax.dot_general(w2t_ref[0], hid.astype(BF16), (((1,), (1,)), ((), ())),
                                  preferred_element_type=F32).astype(o_ref.dtype)


def compress(blk, pos, w1, w2t):
    _, G, NCP, LD = blk.shape
    return pl.pallas_call(
        _compress_kernel,
        out_shape=jax.ShapeDtypeStruct((2, G, HEAD_DIM, NCP), BF16),
        grid=(2, G),
        in_specs=[pl.BlockSpec((1, 1, NCP, LD), lambda a, g: (a, g, 0, 0)),
                  pl.BlockSpec((1, 1, LD), lambda a, g: (a, 0, 0)),
                  pl.BlockSpec((1, LD, CMP_HIDDEN), lambda a, g: (a, 0, 0)),
                  pl.BlockSpec((1, HEAD_DIM, CMP_HIDDEN), lambda a, g: (a, 0, 0))],
        out_specs=pl.BlockSpec((1, 1, HEAD_DIM, NCP), lambda a, g: (a, g, 0, 0)),
        compiler_params=_cparams(("parallel", "parallel")),
        name="nsa_compress",
    )(blk, pos, w1, w2t)


def _tdot(a, b):
    return lax.dot_general(a, b, (((0,), (0,)), ((), ())), preferred_element_type=F32)


def _cmp_attn_kernel(q_ref, kct_ref, vct_ref, sl_ref, ov_ref, oc_ref, sel_ref, cnt_ref, imp_ref, *, n_sel, cch):
    qb = pl.program_id(1)
    gp = kct_ref.shape[0]
    q_augs = []
    for j in range(gp):
        q = q_ref[j * HPG * HEAD_DIM:(j + 1) * HPG * HEAD_DIM, :]
        qt = jnp.concatenate([q[h * HEAD_DIM:(h + 1) * HEAD_DIM, :] for h in range(HPG)], axis=1)
        aug = jnp.concatenate([sl_ref[j], jnp.zeros((8, QL), F32)], axis=0).astype(BF16)
        q_augs.append(jnp.concatenate([qt, aug], axis=0))
    ncp = kct_ref.shape[2]
    n_chunks = ncp // cch
    per_qb = Q_BLOCK // CMP_STRIDE
    last_end = CMP_BLOCK - 1
    need = jnp.minimum((qb * per_qb + per_qb - 2) // cch + 1, n_chunks)
    qoff = lax.broadcasted_iota(jnp.int32, (1, QL), 1) & (Q_BLOCK - 1)
    col_ok = (qb * Q_BLOCK + qoff) >= last_end

    for c in range(1, n_chunks + 1):
        @pl.when(need == c)
        def _(c=c):
            R = c * cch
            r8 = lax.broadcasted_iota(jnp.int32, (8, R), 0)
            n8 = lax.broadcasted_iota(jnp.int32, (8, R), 1)
            hi = ((n8 // per_qb - qb) * Q_BLOCK).astype(F32)
            lo = ((n8 % per_qb) * CMP_STRIDE + last_end).astype(F32)
            pos_rows = jnp.where(r8 < 3, hi, jnp.where(r8 < 6, lo, 0.0))
            k_rows = jnp.concatenate([pos_rows, jnp.zeros((8, R), F32)], axis=0).astype(BF16)
            t0 = max(R - 2 * cch, 0)
            n_io = t0 + lax.broadcasted_iota(jnp.int32, (R - t0, QL), 0)
            visible = n_io * CMP_STRIDE + last_end - qb * Q_BLOCK <= qoff
            ov = ov_ref[:, 0:R]
            for j in range(gp):
                s = _tdot(jnp.concatenate([kct_ref[j, :, 0:R], k_rows], axis=0), q_augs[j])
                tail = jnp.where(visible, s[t0:R], NEG)
                s = tail if t0 == 0 else jnp.concatenate([s[0:t0], tail], axis=0)
                m = jnp.max(s, axis=0, keepdims=True)
                p = jnp.exp2(s - m)
                l = jnp.maximum(jnp.sum(p, axis=0, keepdims=True), 1e-30)
                p = p * jnp.where(col_ok, 1.0 / l, 0.0)
                oc_ref[j, 0] = jnp.dot(vct_ref[j, :, 0:R], p.astype(BF16), preferred_element_type=F32)
                ps = p[:, 0:Q_BLOCK]
                for h in range(1, HPG):
                    ps = ps + p[:, h * Q_BLOCK:(h + 1) * Q_BLOCK]
                hi_p = ps.astype(BF16)
                r1 = ps - hi_p.astype(F32)
                mid_p = r1.astype(BF16)
                lo_p = (r1 - mid_p.astype(F32)).astype(BF16)
                imp_ref[j] = (jnp.dot(ov, hi_p, preferred_element_type=F32)
                              + jnp.dot(ov, mid_p, preferred_element_type=F32)
                              + jnp.dot(ov, lo_p, preferred_element_type=F32))

    n_blk = imp_ref.shape[1]
    j_io = lax.broadcasted_iota(jnp.int32, (n_blk, Q_BLOCK), 0)
    tq = qb * Q_BLOCK + lax.broadcasted_iota(jnp.int32, (n_blk, Q_BLOCK), 1)
    cur = tq // SEL_BLOCK
    forced = (j_io == 0) | (j_io == cur) | (j_io == cur - 1)
    for j in range(gp):
        v = jnp.where(forced, -jnp.inf, jnp.where(j_io <= cur, imp_ref[j], -SEL_FORCE))
        sel = jnp.where(forced, 1.0, 0.0)
        for _ in range(n_sel - 3):
            mx = jnp.max(v, axis=0, keepdims=True)
            idx = jnp.min(jnp.where(v == mx, j_io, n_blk), axis=0, keepdims=True)
            pick = j_io == idx
            v = jnp.where(pick, -jnp.inf, v)
            sel = jnp.where(pick, 1.0, sel)
        live = (sel > 0.5) & (j_io <= cur)
        sel_ref[j, 0, 0:n_blk, :] = jnp.where(live, 0.0, NEG)
        sel_ref[j, 0, n_blk:n_blk + 8, :] = jnp.full((8, Q_BLOCK), NEG, F32)
        cnt_ref[j, 0] = lax.dot_general(jnp.ones((8, Q_BLOCK), BF16), jnp.where(live, 1.0, 0.0).astype(BF16),
                                        (((1,), (1,)), ((), ())), preferred_element_type=F32)


def cmp_attention(feat_t, kc_t, vc_t, slope_rows, ov_t, S):
    G = NSA_KV_HEADS
    n_qb = S // Q_BLOCK
    n_blk = S // SEL_BLOCK
    ncp = kc_t.shape[2]
    n_sel = min(SEL_TOPK, n_blk)
    gp = G
    return pl.pallas_call(
        functools.partial(_cmp_attn_kernel, n_sel=n_sel, cch=min(128, ncp)),
        out_shape=(jax.ShapeDtypeStruct((G, n_qb, HEAD_DIM, QL), F32),
                   jax.ShapeDtypeStruct((G, n_qb, n_blk + 8, Q_BLOCK), F32),
                   jax.ShapeDtypeStruct((G, n_qb, 8, n_blk), F32)),
        grid=(G // gp, n_qb),
        in_specs=[pl.BlockSpec((gp * HPG * HEAD_DIM, Q_BLOCK), lambda g, i: (g, i)),
                  pl.BlockSpec((gp, HEAD_DIM, ncp), lambda g, i: (g, 0, 0)),
                  pl.BlockSpec((gp, HEAD_DIM, ncp), lambda g, i: (g, 0, 0)),
                  pl.BlockSpec((gp, 8, QL), lambda g, i: (g, 0, 0)),
                  pl.BlockSpec((n_blk, ncp), lambda g, i: (0, 0))],
        out_specs=(pl.BlockSpec((gp, 1, HEAD_DIM, QL), lambda g, i: (g, i, 0, 0)),
                   pl.BlockSpec((gp, 1, n_blk + 8, Q_BLOCK), lambda g, i: (g, i, 0, 0)),
                   pl.BlockSpec((gp, 1, 8, n_blk), lambda g, i: (g, i, 0, 0))),
        scratch_shapes=[pltpu.VMEM((gp, n_blk, Q_BLOCK), F32)],
        compiler_params=_cparams(("parallel", "parallel")),
        name="nsa_cmp_select",
    )(feat_t, kc_t, vc_t, slope_rows, ov_t)


KT = 128
SUP = 4
AUG = 16


def _sel_win_kernel(ids_ref, nsup_ref, q_ref, ks_ref, vs_ref, kw_ref, vw_ref, selb_ref, oc_ref, sl_ref, gate_ref,
                    gain_ref, o_ref, *, nt, gp):
    qb = pl.program_id(1)
    n_qb = pl.num_programs(1)

    def k_aug(kt, hi_lane, with_blocks):
        n = kt.shape[1]
        r = lax.broadcasted_iota(jnp.int32, (8, n), 0)
        lane = lax.broadcasted_iota(jnp.int32, (8, n), 1)
        lo = (lane & (KT - 1)).astype(F32)
        pos_rows = jnp.where(r < 3, hi_lane, jnp.where(r < 6, lo, 0.0))
        if with_blocks:
            blk_rows = jnp.where((lane // SEL_BLOCK) == r, 1.0, 0.0)
        else:
            blk_rows = jnp.zeros((8, n), F32)
        return jnp.concatenate([kt, jnp.concatenate([pos_rows, blk_rows], axis=0).astype(BF16)], axis=0)

    def v_aug(vt):
        n = vt.shape[1]
        ones = jnp.where(lax.broadcasted_iota(jnp.int32, (AUG, n), 0) == 0, 1.0, 0.0).astype(BF16)
        return jnp.concatenate([vt, ones], axis=0)

    def tdot(a, b):
        return lax.dot_general(a, b, (((0,), (0,)), ((), ())), preferred_element_type=F32)

    row_io = lax.broadcasted_iota(jnp.int32, (KT, QL), 0)
    qoff = lax.broadcasted_iota(jnp.int32, (KT, QL), 1) & (Q_BLOCK - 1)
    lane_s = lax.broadcasted_iota(jnp.int32, (1, SUP * KT), 1)
    nw = WINDOW // KT + 1
    tiles_w = [qb - (nw - 1) + i for i in range(nw)]
    k0w = [pl.multiple_of(jnp.maximum(t, 0) * KT, KT) for t in tiles_w]
    hi_w = ((lax.broadcasted_iota(jnp.int32, (1, nw * KT), 1) // KT - (nw - 1)) * KT).astype(F32)
    k0d = pl.multiple_of(qb * KT, KT)
    zero_hi = jnp.zeros((1, KT), F32)

    def make_group(j):
        g = pl.program_id(0) * gp + j
        rows = slice(j * HEAD_DIM, (j + 1) * HEAD_DIM)
        q = q_ref[j * HPG * HEAD_DIM:(j + 1) * HPG * HEAD_DIM, :]
        qt = jnp.concatenate([q[h * HEAD_DIM:(h + 1) * HEAD_DIM, :] for h in range(HPG)], axis=1)
        slope_rows = sl_ref[j]

        def q_aug(mask):
            aug = jnp.concatenate([slope_rows, mask], axis=0).astype(BF16)
            return jnp.concatenate([qt, aug], axis=0)

        def mask_rows(block_ids):
            r = lax.broadcasted_iota(jnp.int32, (8, Q_BLOCK), 0)
            out = jnp.zeros((8, Q_BLOCK), F32)
            for i, b in enumerate(block_ids):
                row = jnp.broadcast_to(selb_ref[j, 0, pl.ds(b, 1), :], (8, Q_BLOCK))
                out = jnp.where(r == i, row, out)
            return jnp.concatenate([out] * HPG, axis=1)

        ktw = jnp.concatenate([kw_ref[rows, pl.ds(k0, KT)] for k0 in k0w], axis=1)
        vtw = jnp.concatenate([vw_ref[rows, pl.ds(k0, KT)] for k0 in k0w], axis=1)
        s = tdot(k_aug(ktw, hi_w, False), q_aug(jnp.zeros((8, QL), F32)))
        parts = []
        for i in range(nw):
            si = s[i * KT:(i + 1) * KT]
            if i == 0:
                si = jnp.where((row_io > qoff) & (tiles_w[i] >= 0), si, NEG)
            elif i == nw - 1:
                si = jnp.where(row_io <= qoff, si, NEG)
            else:
                si = jnp.where(tiles_w[i] >= 0, si, NEG)
            parts.append(si)
        s = jnp.concatenate(parts, axis=0)
        m_w = jnp.max(s, axis=0, keepdims=True)
        acc_w = jnp.dot(v_aug(vtw), jnp.exp2(s - m_w).astype(BF16), preferred_element_type=F32)

        s = tdot(k_aug(ks_ref[rows, pl.ds(k0d, KT)], zero_hi, True), q_aug(mask_rows([2 * qb, 2 * qb + 1])))
        s = jnp.where(row_io <= qoff, s, NEG)
        m_s = jnp.max(s, axis=0, keepdims=True)
        acc_s = jnp.dot(v_aug(vs_ref[rows, pl.ds(k0d, KT)]), jnp.exp2(s - m_s).astype(BF16),
                        preferred_element_type=F32)

        def update(si, carry):
            m, acc = carry
            base = (g * n_qb + qb) * nt + si * SUP
            tids = [ids_ref[base + i] for i in range(SUP)]
            tcl = [jnp.minimum(t, nt - 1) for t in tids]
            k0s = [pl.multiple_of(t * KT, KT) for t in tcl]
            kt = jnp.concatenate([ks_ref[rows, pl.ds(k0, KT)] for k0 in k0s], axis=1)
            vt = jnp.concatenate([vs_ref[rows, pl.ds(k0, KT)] for k0 in k0s], axis=1)
            hi = (tcl[SUP - 1] - qb) * KT
            for i in range(SUP - 2, -1, -1):
                hi = jnp.where(lane_s < (i + 1) * KT, (tcl[i] - qb) * KT, hi)
            blocks = [2 * t + b for t in tids for b in range(2)]
            s = tdot(k_aug(kt, hi.astype(F32), True), q_aug(mask_rows(blocks)))
            m_new = jnp.maximum(m, jnp.max(s, axis=0, keepdims=True))
            p = jnp.exp2(s - m_new).astype(BF16)
            acc = jnp.exp2(m - m_new) * acc + jnp.dot(v_aug(vt), p, preferred_element_type=F32)
            return m_new, acc

        def finish(carry):
            _, acc = carry
            gts = _sigmoid(gate_ref[j, 0])
            o_s = acc[0:HEAD_DIM] / acc[HEAD_DIM:HEAD_DIM + 1]
            o_w = acc_w[0:HEAD_DIM] / acc_w[HEAD_DIM:HEAD_DIM + 1]
            o = gts[0:1] * oc_ref[j, 0] + gts[1:2] * o_s + gts[2:3] * o_w
            ms = jnp.mean(o * o, axis=0, keepdims=True)
            y = o * lax.rsqrt(ms + NORM_EPS) * gain_ref[j]
            yt = jnp.concatenate([y[:, h * Q_BLOCK:(h + 1) * Q_BLOCK] for h in range(HPG)], axis=0)
            o_ref[:, j * HPG * HEAD_DIM:(j + 1) * HPG * HEAD_DIM] = yt.T

        return (m_s, acc_s), nsup_ref[g * n_qb + qb], update, finish

    groups = [make_group(j) for j in range(gp)]
    n_iter = groups[0][1]
    for grp in groups[1:]:
        n_iter = jnp.maximum(n_iter, grp[1])
    carries = lax.fori_loop(0, n_iter, lambda si, cs: tuple(grp[2](si, c) for grp, c in zip(groups, cs)),
                            tuple(grp[0] for grp in groups))
    for grp, c in zip(groups, carries):
        grp[3](c)


def sel_win_attention(tile_ids, n_sup, feat_t, selb, oc_t, slope_rows, gates_t, gain_t, S):
    G = NSA_KV_HEADS
    gp = 2
    n_qb = S // Q_BLOCK
    n_blk = S // SEL_BLOCK
    nt = S // KT
    kv_rows = gp * HEAD_DIM
    kv_spec = lambda base: pl.BlockSpec((kv_rows, S), lambda g, i, ids, ns: (base // kv_rows + g, 0))
    grid_spec = pltpu.PrefetchScalarGridSpec(
        num_scalar_prefetch=2,
        grid=(G // gp, n_qb),
        in_specs=[pl.BlockSpec((gp * HPG * HEAD_DIM, Q_BLOCK), lambda g, i, ids, ns: (g, i)),
                  kv_spec(FT_KS), kv_spec(FT_VS), kv_spec(FT_KW), kv_spec(FT_VW),
                  pl.BlockSpec((gp, 1, n_blk + 8, Q_BLOCK), lambda g, i, ids, ns: (g, i, 0, 0)),
                  pl.BlockSpec((gp, 1, HEAD_DIM, QL), lambda g, i, ids, ns: (g, i, 0, 0)),
                  pl.BlockSpec((gp, 8, QL), lambda g, i, ids, ns: (g, 0, 0)),
                  pl.BlockSpec((gp, 1, 3, QL), lambda g, i, ids, ns: (g, i, 0, 0)),
                  pl.BlockSpec((gp, HEAD_DIM, QL), lambda g, i, ids, ns: (g, 0, 0))],
        out_specs=pl.BlockSpec((Q_BLOCK, gp * HPG * HEAD_DIM), lambda g, i, ids, ns: (i, g)),
    )
    return pl.pallas_call(
        functools.partial(_sel_win_kernel, nt=nt, gp=gp),
        out_shape=jax.ShapeDtypeStruct((S, NSA_WIDTH), F32),
        grid_spec=grid_spec,
        compiler_params=_cparams(("parallel", "parallel")),
        name="nsa_sel_win",
    )(tile_ids, n_sup, feat_t, feat_t, feat_t, feat_t, feat_t, selb, oc_t, slope_rows, gates_t, gain_t)


def _conv_kernel(*refs, width, glu, post):
    if post == "ln_silu":
        x_ref, halo_ref, w_ref, b_ref, lg_ref, lb_ref, o_ref, u_ref = refs
    else:
        x_ref, halo_ref, w_ref, b_ref, o_ref, u_ref = refs
    i = pl.program_id(0)
    T = o_ref.shape[0]
    H = halo_ref.shape[0]
    C = o_ref.shape[1]

    def pre(v):
        return v[:, :C] * _sigmoid(v[:, C:]) if glu else v

    u_ref[0:H, :] = jnp.where(i > 0, pre(halo_ref[...]), 0.0)
    u_ref[H:H + T, :] = pre(x_ref[...])
    acc = jnp.broadcast_to(b_ref[...], (T, C))
    base = H - (width - 1)
    for k in range(width):
        acc = acc + w_ref[k:k + 1, :] * u_ref[base + k:base + k + T, :]
    if post == "ln_silu":
        mu = jnp.mean(acc, axis=-1, keepdims=True)
        xc = acc - mu
        var = jnp.mean(xc * xc, axis=-1, keepdims=True)
        acc = xc * lax.rsqrt(var + LN_EPS) * lg_ref[...] + lb_ref[...]
    o_ref[...] = acc * _sigmoid(acc)


def causal_conv(x, col_block, w, b, ln=None, *, glu, T):
    S = x.shape[0]
    width, C = w.shape
    cin = 2 * C if glu else C
    H = -(-(width - 1) // 8) * 8
    post = "ln_silu" if ln is not None else "silu"
    in_specs = [pl.BlockSpec((T, cin), lambda i: (i, col_block)),
                pl.BlockSpec((H, cin), lambda i: (jnp.maximum(i * (T // H) - 1, 0), col_block)),
                pl.BlockSpec((width, C), lambda i: (0, 0)),
                pl.BlockSpec((1, C), lambda i: (0, 0))]
    args = [x, x, w, b]
    if ln is not None:
        in_specs += [pl.BlockSpec((1, C), lambda i: (0, 0))] * 2
        args += list(ln)
    return pl.pallas_call(
        functools.partial(_conv_kernel, width=width, glu=glu, post=post),
        out_shape=jax.ShapeDtypeStruct((S, C), F32),
        grid=(S // T,),
        in_specs=in_specs,
        out_specs=pl.BlockSpec((T, C), lambda i: (i, 0)),
        scratch_shapes=[pltpu.VMEM((H + T, C), F32)],
        compiler_params=_cparams(("parallel",)),
        name="causal_conv_glu" if glu else "causal_conv",
    )(*args)


def _log_sigmoid(x):
    return jnp.minimum(x, 0.0) - jnp.log(1.0 + jnp.exp(-jnp.abs(x)))


def _split3(x):
    hi = x.astype(BF16).astype(F32)
    mid = (x - hi).astype(BF16).astype(F32)
    lo = (x - hi - mid).astype(BF16).astype(F32)
    return hi, mid, lo


def _lane_scan(x, op, fill):
    n = x.shape[1]
    lane = lax.broadcasted_iota(jnp.int32, x.shape, 1)
    sh = 1
    while sh < n:
        x = op(x, jnp.where(lane >= sh, pltpu.roll(x, sh, axis=1), fill))
        sh *= 2
    return x


def _mlstm_kernel(qk_ref, v_ref, o_ref, gt_ref, bias_ref, gain_ref, out_ref, c_ref, m_ref):
    L = qk_ref.shape[0]
    DH = MLSTM_HEAD_DIM
    W = MLSTM_WIDTH
    H = MLSTM_HEADS

    @pl.when(pl.program_id(0) == 0)
    def _():
        c_ref[...] = jnp.zeros_like(c_ref)
        m_ref[...] = jnp.zeros_like(m_ref)

    g8 = gt_ref[...] + bias_ref[...]
    b8 = pltpu.roll(_lane_scan(_log_sigmoid(g8), jnp.add, 0.0), H, axis=0)
    e8 = g8 - b8
    m0 = m_ref[:, 0:1]
    u8 = jnp.maximum(_lane_scan(e8, jnp.maximum, -jnp.inf), m0)
    b_end = jnp.min(b8, axis=1, keepdims=True)
    a8 = b_end - b8 + g8
    m_new = jnp.maximum(b_end + m0, jnp.max(a8, axis=1, keepdims=True))
    sp8 = jnp.exp(b_end + m0 - m_new)
    m_ref[...] = jnp.broadcast_to(m_new, m_ref.shape)

    ones_rows = jnp.ones((3, L), F32)
    ones_rep = jnp.ones((8, DH), BF16)
    ones_sq = jnp.ones((DH, DH), BF16)
    causal = lax.broadcasted_iota(jnp.int32, (L, L), 1) <= lax.broadcasted_iota(jnp.int32, (L, L), 0)

    def rows8(*parts):
        n = sum(p.shape[0] for p in parts)
        return jnp.concatenate(list(parts) + [jnp.zeros((8 - n, L), F32)], axis=0).astype(BF16)

    def rep(row):
        return _tdot(rows8(*_split3(row)), ones_rep)

    for h in range(H):
        q = qk_ref[:, h * DH:(h + 1) * DH].astype(BF16)
        k = qk_ref[:, W + h * DH:W + (h + 1) * DH] * (DH ** -0.5)
        v = v_ref[:, h * DH:(h + 1) * DH].astype(BF16)
        vaug = jnp.concatenate([v, jnp.ones((L, DH), BF16)], axis=1)
        c0 = c_ref[h]
        e_row, u_row = e8[h:h + 1], u8[h:h + 1]
        x = _tdot(rows8(*_split3(-u_row), ones_rows), rows8(ones_rows, *_split3(e_row)))
        decay = jnp.where(causal, jnp.exp(x), 0.0)
        qk = lax.dot_general(q, k.astype(BF16), (((1,), (1,)), ((), ())), preferred_element_type=F32)
        w_inter = jnp.exp(rep(m0[h:h + 1] - u_row))
        r = (jnp.dot((qk * decay).astype(BF16), vaug, preferred_element_type=F32)
             + jnp.concatenate([w_inter, w_inter], axis=1)
             * jnp.dot(q, c0.astype(BF16), preferred_element_type=F32))
        m_t = rep(b8[h:h + 1] + u_row)
        hh = r[:, :DH] / jnp.maximum(jnp.abs(r[:, DH:]), jnp.exp(-m_t))

        kw = (k * jnp.exp(rep(a8[h:h + 1] - m_new[h:h + 1]))).astype(BF16)
        c_ref[h] = sp8[h:h + 1] * c0 + _tdot(kw, vaug)

        y = _sigmoid(o_ref[:, h * DH:(h + 1) * DH]) * hh
        y2 = y * y
        y2_hi = y2.astype(BF16)
        y2_lo = (y2 - y2_hi.astype(F32)).astype(BF16)
        ms = (jnp.dot(y2_hi, ones_sq, preferred_element_type=F32)
              + jnp.dot(y2_lo, ones_sq, preferred_element_type=F32)) * (1.0 / DH)
        out_ref[:, h * DH:(h + 1) * DH] = y * lax.rsqrt(ms + NORM_EPS) * gain_ref[:, h * DH:(h + 1) * DH]


def mlstm(qk, rm, gates_t, bias_col, gain, S):
    L = min(MLSTM_CHUNK, S)
    W = MLSTM_WIDTH
    return pl.pallas_call(
        _mlstm_kernel,
        out_shape=jax.ShapeDtypeStruct((S, W), F32),
        grid=(S // L,),
        in_specs=[pl.BlockSpec((L, 2 * W), lambda c: (c, 0)),
                  pl.BlockSpec((L, W), lambda c: (c, RM_V // W)),
                  pl.BlockSpec((L, W), lambda c: (c, RM_O // W)),
                  pl.BlockSpec((2 * MLSTM_HEADS, L), lambda c: (0, c)),
                  pl.BlockSpec((2 * MLSTM_HEADS, 1), lambda c: (0, 0)),
                  pl.BlockSpec((1, W), lambda c: (0, 0))],
        out_specs=pl.BlockSpec((L, W), lambda c: (c, 0)),
        scratch_shapes=[pltpu.VMEM((MLSTM_HEADS, MLSTM_HEAD_DIM, 2 * MLSTM_HEAD_DIM), F32),
                        pltpu.VMEM((8, 128), F32)],
        compiler_params=_cparams(("arbitrary",)),
        name="mlstm",
    )(qk, rm, rm, gates_t, bias_col, gain)


def _router_kernel(x_ref, g_ref, wr_ref, br_ref, xn_ref, eid_ref, gate_ref):
    xn = _rms_rows(x_ref[...], g_ref[...])
    xn_ref[:, 0, :] = xn
    x_hi = xn.astype(BF16)
    x_lo = (xn - x_hi.astype(F32)).astype(BF16)
    w_hi, w_lo = wr_ref[0], wr_ref[1]
    logits = (jnp.dot(x_hi, w_hi, preferred_element_type=F32)
              + (jnp.dot(x_hi, w_lo, preferred_element_type=F32)
                 + jnp.dot(x_lo, w_hi, preferred_element_type=F32)))
    logits = logits.T[0:br_ref.shape[0], :] + br_ref[...]
    tm = logits.shape[1]
    lg = logits[N_EXPERTS:N_EXPERTS + N_GROUPS, :]
    eg = jnp.exp(lg - jnp.max(lg, axis=0, keepdims=True))
    pg = eg / jnp.sum(eg, axis=0, keepdims=True)
    pg_top = jnp.max(pg, axis=0, keepdims=True)
    g_io = lax.broadcasted_iota(jnp.int32, (N_GROUPS, tm), 0)
    grp = jnp.min(jnp.where(pg == pg_top, g_io, N_GROUPS), axis=0, keepdims=True)
    le = logits[0:EXPERTS_PER_GROUP, :]
    for g in range(1, N_GROUPS):
        le = jnp.where(grp == g, logits[g * EXPERTS_PER_GROUP:(g + 1) * EXPERTS_PER_GROUP, :], le)
    ee = jnp.exp(le - jnp.max(le, axis=0, keepdims=True))
    pe = ee / jnp.sum(ee, axis=0, keepdims=True)
    e_io = lax.broadcasted_iota(jnp.int32, (EXPERTS_PER_GROUP, tm), 0)
    p1 = jnp.max(pe, axis=0, keepdims=True)
    i1 = jnp.min(jnp.where(pe == p1, e_io, EXPERTS_PER_GROUP), axis=0, keepdims=True)
    pe2 = jnp.where(e_io == i1, -1.0, pe)
    p2 = jnp.max(pe2, axis=0, keepdims=True)
    i2 = jnp.min(jnp.where(pe2 == p2, e_io, EXPERTS_PER_GROUP), axis=0, keepdims=True)
    tot = p1 + p2
    eid_ref[...] = jnp.concatenate([grp * EXPERTS_PER_GROUP + i1, grp * EXPERTS_PER_GROUP + i2], axis=0)
    gate_ref[...] = jnp.concatenate([pg_top * p1 / tot, pg_top * p2 / tot], axis=0)


def router(x, g, wr_t, br_t, tm):
    S, D = x.shape
    R = br_t.shape[0]
    return pl.pallas_call(
        _router_kernel,
        out_shape=(jax.ShapeDtypeStruct((S, 1, D), F32),
                   jax.ShapeDtypeStruct((2, S), jnp.int32),
                   jax.ShapeDtypeStruct((2, S), F32)),
        grid=(S // tm,),
        in_specs=[pl.BlockSpec((tm, D), lambda i: (i, 0)),
                  pl.BlockSpec((1, D), lambda i: (0, 0)),
                  pl.BlockSpec((2, D, 128), lambda i: (0, 0, 0)),
                  pl.BlockSpec((R, 1), lambda i: (0, 0))],
        out_specs=(pl.BlockSpec((tm, 1, D), lambda i: (i, 0, 0)),
                   pl.BlockSpec((2, tm), lambda i: (0, i)),
                   pl.BlockSpec((2, tm), lambda i: (0, i))),
        compiler_params=_cparams(("parallel",)),
        name="moe_router",
    )(x, g, wr_t, br_t)


def _row_gather(idx_ref, idx0, stride, src_hbm, dst, sem, n_rows):
    def body(r, carry):
        i = idx_ref[idx0 + r * stride]
        pltpu.make_async_copy(src_hbm.at[i], dst.at[pl.ds(r, 1), :], sem).start()
        return carry

    lax.fori_loop(0, n_rows, body, 0, unroll=8)


def _rows_wait(src_hbm, dst, sem):
    pltpu.make_async_copy(src_hbm.at[pl.ds(0, dst.shape[0]), 0], dst, sem).wait()


def _expert_kernel(be_ref, nu_ref, tok_ref, dst_ref, xn_hbm, wg_ref, wu_ref, wd_ref, y_hbm,
                   xbuf0, xbuf1, obuf, gsem, ssem, wg_s, wu_s, wd_s):
    b = pl.program_id(0)
    n_used = nu_ref[0]
    e = be_ref[b]
    prev = be_ref[jnp.maximum(b - 1, 0)]
    spare0 = y_hbm.shape[0] - MOE_ROWS

    def scatter_wait():
        pltpu.make_async_copy(obuf, y_hbm.at[pl.ds(0, MOE_ROWS), 0], ssem.at[0]).wait()

    xbufs = (xbuf0, xbuf1)

    @pl.when(b == 0)
    def _():
        _row_gather(tok_ref, 0, 1, xn_hbm, xbuf0, gsem.at[0], MOE_ROWS)
        obuf[...] = jnp.zeros_like(obuf)
        pltpu.make_async_copy(obuf, y_hbm.at[pl.ds(spare0, MOE_ROWS), 0], ssem.at[0]).start()

    @pl.when((b == 0) | (e != prev))
    def _():
        wg_s[...] = wg_ref[0, 0].astype(BF16)
        wu_s[...] = wu_ref[0, 0].astype(BF16)
        wd_s[...] = wd_ref[0, 0].astype(BF16)

    half = MOE_ROWS // 2

    def run_block(slot):
        cur, oth = xbufs[slot], xbufs[1 - slot]
        _rows_wait(xn_hbm, cur, gsem.at[slot])
        nxt = jnp.minimum(b + 1, n_used - 1) * MOE_ROWS

        def ffn(r0):
            x = cur[r0:r0 + half, :].astype(BF16)
            hg = jnp.dot(x, wg_s[...], preferred_element_type=F32)
            hu = jnp.dot(x, wu_s[...], preferred_element_type=F32)
            hb = (hg * _sigmoid(hg) * hu).astype(BF16)
            return jnp.dot(hb, wd_s[...], preferred_element_type=F32)

        def gather_next(r0):
            for r in range(r0, r0 + half):
                pltpu.make_async_copy(xn_hbm.at[tok_ref[nxt + r]], oth.at[pl.ds(r, 1), :],
                                      gsem.at[1 - slot]).start()

        def scatter(r0):
            for r in range(r0, r0 + half):
                pltpu.make_async_copy(obuf.at[pl.ds(r, 1), :], y_hbm.at[dst_ref[b * MOE_ROWS + r]],
                                      ssem.at[0]).start()

        gather_next(0)
        y0 = ffn(0)
        scatter_wait()
        obuf[0:half, :] = y0
        scatter(0)
        gather_next(half)
        obuf[half:MOE_ROWS, :] = ffn(half)
        scatter(half)

    for slot in range(2):
        @pl.when((b < n_used) & (b % 2 == slot))
        def _(slot=slot):
            run_block(slot)

    @pl.when(b == n_used - 1)
    def _():
        scatter_wait()
        for slot in range(2):
            @pl.when((b + 1) % 2 == slot)
            def _(slot=slot):
                _rows_wait(xn_hbm, xbufs[slot], gsem.at[slot])


def expert_ffn(blk_e, n_used, row_tok, row_dst, xn, w_gate, w_up, w_down, layer, n_out):
    P = row_tok.shape[0]
    D = xn.shape[2]
    n_blocks = P // MOE_ROWS
    Hd = w_gate.shape[3]
    grid_spec = pltpu.PrefetchScalarGridSpec(
        num_scalar_prefetch=4,
        grid=(n_blocks,),
        in_specs=[pl.BlockSpec(memory_space=pl.ANY),
                  pl.BlockSpec((1, 1, D, Hd), lambda b, be, nu, tk, ds: (layer, be[b], 0, 0)),
                  pl.BlockSpec((1, 1, D, Hd), lambda b, be, nu, tk, ds: (layer, be[b], 0, 0)),
                  pl.BlockSpec((1, 1, Hd, D), lambda b, be, nu, tk, ds: (layer, be[b], 0, 0))],
        out_specs=pl.BlockSpec(memory_space=pl.ANY),
        scratch_shapes=[pltpu.VMEM((MOE_ROWS, D), F32), pltpu.VMEM((MOE_ROWS, D), F32),
                        pltpu.VMEM((MOE_ROWS, D), F32),
                        pltpu.SemaphoreType.DMA((2,)), pltpu.SemaphoreType.DMA((1,)),
                        pltpu.VMEM((D, Hd), BF16), pltpu.VMEM((D, Hd), BF16), pltpu.VMEM((Hd, D), BF16)],
    )
    return pl.pallas_call(
        _expert_kernel,
        out_shape=jax.ShapeDtypeStruct((n_out + MOE_ROWS, 1, D), F32),
        grid_spec=grid_spec,
        compiler_params=_cparams(("arbitrary",)),
        name="moe_experts",
    )(blk_e, n_used, row_tok, row_dst, xn, w_gate, w_up, w_down)


def _combine_kernel(x_ref, gate_ref, y0_ref, y1_ref, o_ref):
    gate = gate_ref[...]
    o_ref[...] = x_ref[...] + gate[:, 0:1] * y0_ref[:, 0, :] + gate[:, 1:2] * y1_ref[:, 0, :]


def moe_combine(x, gate, y, tt):
    S, D = x.shape
    return pl.pallas_call(
        _combine_kernel,
        out_shape=jax.ShapeDtypeStruct((S, D), F32),
        grid=(S // tt,),
        in_specs=[pl.BlockSpec((tt, D), lambda i: (i, 0)),
                  pl.BlockSpec((tt, 2), lambda i: (i, 0)),
                  pl.BlockSpec((tt, 1, D), lambda i: (i, 0, 0)),
                  pl.BlockSpec((tt, 1, D), lambda i: (S // tt + i, 0, 0))],
        out_specs=pl.BlockSpec((tt, D), lambda i: (i, 0)),
        compiler_params=_cparams(("parallel",)),
        name="moe_combine",
    )(x, gate, y, y)


def _final_norm_kernel(x_ref, g_ref, o_ref):
    o_ref[...] = _rms_rows(x_ref[...], g_ref[...])


def final_norm(x, g, tm):
    S, D = x.shape
    return pl.pallas_call(
        _final_norm_kernel,
        out_shape=jax.ShapeDtypeStruct((S, D), F32),
        grid=(S // tm,),
        in_specs=[pl.BlockSpec((tm, D), lambda i: (i, 0)), pl.BlockSpec((1, D), lambda i: (0, 0))],
        out_specs=pl.BlockSpec((tm, D), lambda i: (i, 0)),
        compiler_params=_cparams(("parallel",)),
        name="final_norm",
    )(x, g)


LOG2E = float(np.log2(np.e))


def _alibi_lanes():
    sl = 2.0 ** (-8.0 * np.arange(1, NSA_HEADS + 1) / NSA_HEADS) * LOG2E
    sl = np.repeat(sl.reshape(NSA_KV_HEADS, HPG, 1), Q_BLOCK, axis=2).reshape(NSA_KV_HEADS, 1, QL)
    sl = jnp.asarray(sl, F32)
    s1 = sl.astype(BF16).astype(F32)
    s2 = (sl - s1).astype(BF16).astype(F32)
    s3 = (sl - s1 - s2).astype(BF16).astype(F32)
    zero = jnp.zeros_like(sl)
    return jnp.concatenate([s1, s2, s3, s1, s2, s3, zero, zero], axis=1)


def _active_tiles(cnt, S):
    G, n_qb, n_blk = cnt.shape
    nt = S // KT
    act = (cnt > 0.5).reshape(G, n_qb, nt, KT // SEL_BLOCK).any(-1)
    tile = jnp.arange(nt, dtype=jnp.int32)
    act = act & (tile[None, None, :] < (jnp.arange(n_qb) * (Q_BLOCK // KT))[None, :, None])
    rank = jnp.cumsum(act.astype(jnp.int32), axis=-1) - 1
    n_act = rank[..., -1] + 1
    hit = act[..., None, :] & (rank[..., None, :] == tile[None, None, :, None])
    ids = jnp.sum(jnp.where(hit, tile[None, None, None, :], 0), axis=-1)
    ids = jnp.where(tile[None, None, :] < n_act[..., None], ids, nt).astype(jnp.int32)
    return ids.reshape(-1), ((n_act + SUP - 1) // SUP).astype(jnp.int32).reshape(-1)


def _overlap_t(n_cmp_pad, n_cmp, n_blk):
    cs = np.arange(n_cmp_pad) * CMP_STRIDE
    ss = np.arange(n_blk) * SEL_BLOCK
    ov = np.minimum(cs[None, :] + CMP_BLOCK, ss[:, None] + SEL_BLOCK) - np.maximum(cs[None, :], ss[:, None])
    ov = np.clip(ov, 0, None) / CMP_STRIDE
    ov[:, n_cmp:] = 0.0
    return jnp.asarray(ov, BF16)


def mixer(x, p, S):
    G = NSA_KV_HEADS
    tm = min(1024, S)
    w_in = p["w_in"]
    c = np.cumsum((0, NSA_WIDTH) + (NSA_KV_WIDTH,) * 6 + (3 * NSA_HEADS,) + (MLSTM_WIDTH,) * 4
                  + (MLSTM_HEADS,) * 2 + (2 * CONV_CHANNELS,))
    (q0, kc0, vc0, ks0, vs0, kw0, vw0, gt0, mq0, mk0, mv0, mo0, mi0, mf0, cu0, end) = [int(v) for v in c]
    w_rm = jnp.concatenate(
        [w_in[:, mq0:mv0], w_in[:, cu0:end], w_in[:, mv0:mi0], w_in[:, kc0:ks0], w_in[:, gt0:mq0],
         w_in[:, mi0:cu0], jnp.zeros((D_MODEL, RM_WIDTH - RM_SMALL - 56), F32)], axis=1).astype(BF16)
    w_ft = jnp.concatenate([w_in[:, q0:kc0], w_in[:, ks0:gt0]], axis=1).T.astype(BF16)
    g = p["attn_norm_g"][None, :]
    rm = norm_matmul(x, g, w_rm, tm, 768)
    q_scale = jnp.where(jnp.arange(FT_WIDTH) < FT_KS, HEAD_DIM ** -0.5 * LOG2E, 1.0).astype(F32)[:, None]
    feat_t = norm_matmul_t(x, g, w_ft, q_scale, tm, 512)

    ncp = S // CMP_STRIDE
    n_cmp = (S - CMP_BLOCK) // CMP_STRIDE + 1
    n_blk = S // SEL_BLOCK
    n_qb = S // Q_BLOCK
    kv = jnp.stack([rm[:, RM_KC:RM_KC + NSA_KV_WIDTH], rm[:, RM_VC:RM_VC + NSA_KV_WIDTH]])
    rows = kv.reshape(2, ncp, CMP_STRIDE, G, HEAD_DIM).transpose(0, 3, 1, 2, 4).reshape(2, G, ncp, -1)
    nxt = jnp.concatenate([rows[:, :, 1:], jnp.zeros_like(rows[:, :, :1])], axis=2)
    blk = jnp.concatenate([rows, nxt], axis=-1)
    pos = jnp.stack([p["cmp_pos_k"].reshape(1, -1), p["cmp_pos_v"].reshape(1, -1)])
    w1 = jnp.stack([p["cmp_w1_k"], p["cmp_w1_v"]]).astype(BF16)
    w2t = jnp.stack([p["cmp_w2_k"].T, p["cmp_w2_v"].T]).astype(BF16)
    kvc_t = compress(blk, pos, w1, w2t)
    slope_rows = _alibi_lanes()
    oc_t, selb, cnt = cmp_attention(feat_t, kvc_t[0], kvc_t[1], slope_rows, _overlap_t(ncp, n_cmp, n_blk), S)
    tile_ids, n_sup = _active_tiles(cnt[:, :, 0, :], S)
    small = rm[:, RM_SMALL:RM_SMALL + 128]
    gates_t = small[:, SM_GATES:SM_GATES + 3 * NSA_HEADS].reshape(n_qb, Q_BLOCK, G, HPG, 3)
    gates_t = gates_t.transpose(2, 0, 4, 3, 1).reshape(G, n_qb, 3, QL)
    gain_t = jnp.broadcast_to(p["nsa_out_g"].reshape(G, HPG, HEAD_DIM, 1).transpose(0, 2, 1, 3),
                              (G, HEAD_DIM, HPG, Q_BLOCK)).reshape(G, HEAD_DIM, QL)
    a_out = sel_win_attention(tile_ids, n_sup, feat_t, selb, oc_t, slope_rows, gates_t, gain_t, S)

    tc = min(512, S)
    qk = causal_conv(rm, RM_QK // (2 * MLSTM_WIDTH), p["mlstm_conv_w"], p["mlstm_conv_b"][None, :],
                     glu=False, T=tc)
    gates_m = small[:, SM_I:SM_I + 2 * MLSTM_HEADS].T
    bias = jnp.concatenate([p["mlstm_i_bias"], p["mlstm_f_bias"]])
    hm = mlstm(qk, rm, gates_m, bias[:, None], p["mlstm_out_g"][None, :], S)

    cv = causal_conv(rm, RM_CU // (2 * CONV_CHANNELS), p["conv_w"], p["conv_b"][None, :],
                     (p["conv_ln_g"][None, :], p["conv_ln_b"][None, :]), glu=True, T=tc)

    return out_proj(a_out, hm, cv, x, p["w_out"].astype(BF16), min(512, S), 1024)


def moe(x, p, stacked_w, layer, S):
    tm = min(512, S)
    wr_t = jnp.concatenate([p["router_w_expert"], p["router_w_group"],
                            jnp.zeros((D_MODEL, 128 - N_EXPERTS - N_GROUPS), F32)], axis=1)
    wr_hi = wr_t.astype(BF16)
    wr_t = jnp.stack([wr_hi, (wr_t - wr_hi.astype(F32)).astype(BF16)])
    br_t = jnp.concatenate([p["router_b_expert"].reshape(-1), p["router_b_group"], jnp.zeros((12,), F32)])[:, None]
    xn, eid, gate = router(x, p["ffn_norm_g"][None, :], wr_t, br_t, tm)

    flat_e = eid.T.reshape(-1)
    A = flat_e.shape[0]
    onehot = (flat_e[:, None] == jnp.arange(N_EXPERTS)[None, :]).astype(jnp.int32)
    rank = jnp.take_along_axis(jnp.cumsum(onehot, axis=0), flat_e[:, None], axis=1)[:, 0] - 1
    counts = jnp.sum(onehot, axis=0)
    padded = (counts + MOE_ROWS - 1) // MOE_ROWS * MOE_ROWS
    pends = jnp.cumsum(padded)
    dest = (pends - padded)[flat_e] + rank
    n_blocks = A // MOE_ROWS + N_EXPERTS
    P = n_blocks * MOE_ROWS
    row_a = jnp.full((P,), -1, jnp.int32).at[dest].set(jnp.arange(A, dtype=jnp.int32), unique_indices=True)
    row_tok = jnp.maximum(row_a, 0) // 2
    spare = 2 * S + jnp.arange(P, dtype=jnp.int32) % MOE_ROWS
    row_dst = jnp.where(row_a >= 0, (row_a % 2) * S + row_a // 2, spare)
    n_used = (pends[-1] // MOE_ROWS).astype(jnp.int32)
    blk_start = jnp.minimum(jnp.arange(n_blocks), n_used - 1) * MOE_ROWS
    blk_e = jnp.minimum(jnp.sum(pends[None, :] <= blk_start[:, None], axis=1), N_EXPERTS - 1).astype(jnp.int32)

    y = expert_ffn(blk_e, n_used[None], row_tok, row_dst, xn, stacked_w["expert_w_gate"],
                   stacked_w["expert_w_up"], stacked_w["expert_w_down"], layer, 2 * S)
    return moe_combine(x, gate.T, y, min(512, S))


_LAYER_KEYS = ("attn_norm_g", "w_in", "cmp_pos_k", "cmp_w1_k", "cmp_w2_k", "cmp_pos_v", "cmp_w1_v", "cmp_w2_v",
               "nsa_out_g", "mlstm_conv_w", "mlstm_conv_b", "mlstm_i_bias", "mlstm_f_bias", "mlstm_out_g",
               "conv_w", "conv_b", "conv_ln_g", "conv_ln_b", "w_out", "ffn_norm_g", "router_w_group",
               "router_b_group", "router_w_expert", "router_b_expert", "expert_w_gate", "expert_w_up",
               "expert_w_down")


def kernel(x, attn_norm_g, w_in, cmp_pos_k, cmp_w1_k, cmp_w2_k, cmp_pos_v, cmp_w1_v, cmp_w2_v, nsa_out_g, mlstm_conv_w, mlstm_conv_b, mlstm_i_bias, mlstm_f_bias, mlstm_out_g, conv_w, conv_b, conv_ln_g, conv_ln_b, w_out, ffn_norm_g, router_w_group, router_b_group, router_w_expert, router_b_expert, expert_w_gate, expert_w_up, expert_w_down, final_norm_g):
    stacked = (attn_norm_g, w_in, cmp_pos_k, cmp_w1_k, cmp_w2_k, cmp_pos_v, cmp_w1_v, cmp_w2_v, nsa_out_g,
               mlstm_conv_w, mlstm_conv_b, mlstm_i_bias, mlstm_f_bias, mlstm_out_g, conv_w, conv_b, conv_ln_g,
               conv_ln_b, w_out, ffn_norm_g, router_w_group, router_b_group, router_w_expert, router_b_expert,
               expert_w_gate, expert_w_up, expert_w_down)
    B, S, D = x.shape
    assert B == 1 and D == D_MODEL and S % 1024 == 0
    h = x.reshape(S, D)
    stacked = dict(zip(_LAYER_KEYS, stacked))
    for l in range(attn_norm_g.shape[0]):
        p = {k: v[l] for k, v in stacked.items() if not k.startswith("expert_w")}
        h = mixer(h, p, S)
        h = moe(h, p, stacked, l, S)
    return final_norm(h, final_norm_g[None, :], min(512, S)).reshape(B, S, D)
```

```python
import functools

import numpy as np
import jax
import jax.numpy as jnp
from jax import lax
from jax.experimental import pallas as pl
from jax.experimental.pallas import tpu as pltpu

F32 = jnp.float32
BF16 = jnp.bfloat16

D_MODEL = 2048
HEAD_DIM = 64
NSA_WIDTH = D_MODEL // 2
NSA_HEADS = NSA_WIDTH // HEAD_DIM
NSA_KV_HEADS = NSA_HEADS // 4
HPG = NSA_HEADS // NSA_KV_HEADS
NSA_KV_WIDTH = NSA_KV_HEADS * HEAD_DIM
CMP_BLOCK = 32
CMP_STRIDE = 16
CMP_HIDDEN = 4 * HEAD_DIM
SEL_BLOCK = 64
SEL_TOPK = 16
WINDOW = 512
Q_BLOCK = 128
SEL_FORCE = 1.0e4
MLSTM_WIDTH = D_MODEL // 4
MLSTM_HEADS = 4
MLSTM_HEAD_DIM = MLSTM_WIDTH // MLSTM_HEADS
MLSTM_CONV = 4
CONV_CHANNELS = D_MODEL // 4
CONV_WIDTH = 31
N_GROUPS = 4
EXPERTS_PER_GROUP = 8
N_EXPERTS = N_GROUPS * EXPERTS_PER_GROUP
EXPERT_HIDDEN = D_MODEL // 4
NORM_EPS = 1e-6
LN_EPS = 1e-5

NEG = -1.0e30
QL = HPG * Q_BLOCK
MLSTM_CHUNK = 256
MOE_ROWS = 256
VMEM_LIMIT = 52 * 1024 * 1024

RM_QK, RM_CU, RM_V, RM_O, RM_KC, RM_VC, RM_SMALL = 0, 1024, 2048, 2560, 3072, 3328, 3584
RM_WIDTH = 3840
SM_GATES, SM_I, SM_F = 0, 48, 52
FT_Q, FT_KS, FT_VS, FT_KW, FT_VW = 0, 1024, 1280, 1536, 1792
FT_WIDTH = 2048


def _cparams(sem, vmem=VMEM_LIMIT):
    return pltpu.CompilerParams(dimension_semantics=sem, vmem_limit_bytes=vmem)


def _sigmoid(x):
    return 1.0 / (1.0 + jnp.exp(-x))


def _rms_rows(x, g):
    ms = jnp.mean(x * x, axis=-1, keepdims=True)
    return x * lax.rsqrt(ms + NORM_EPS) * g


def _norm_mm_kernel(x_ref, g_ref, w_ref, o_ref, h_ref):
    @pl.when(pl.program_id(1) == 0)
    def _():
        h_ref[...] = _rms_rows(x_ref[...], g_ref[...]).astype(BF16)

    o_ref[...] = jnp.dot(h_ref[...], w_ref[...], preferred_element_type=F32).astype(o_ref.dtype)


def norm_matmul(x, g, w, tm, tn):
    M, K = x.shape
    N = w.shape[1]
    return pl.pallas_call(
        _norm_mm_kernel,
        out_shape=jax.ShapeDtypeStruct((M, N), F32),
        grid=(M // tm, N // tn),
        in_specs=[pl.BlockSpec((tm, K), lambda i, j: (i, 0)),
                  pl.BlockSpec((1, K), lambda i, j: (0, 0)),
                  pl.BlockSpec((K, tn), lambda i, j: (0, j))],
        out_specs=pl.BlockSpec((tm, tn), lambda i, j: (i, j)),
        scratch_shapes=[pltpu.VMEM((tm, K), BF16)],
        compiler_params=_cparams(("parallel", "arbitrary")),
        name="norm_matmul",
    )(x, g, w)


def _norm_mm_t_kernel(x_ref, g_ref, wt_ref, sc_ref, o_ref, h_ref):
    @pl.when(pl.program_id(1) == 0)
    def _():
        h_ref[...] = _rms_rows(x_ref[...], g_ref[...]).astype(BF16)

    o = lax.dot_general(wt_ref[...], h_ref[...], (((1,), (1,)), ((), ())), preferred_element_type=F32)
    o_ref[...] = (o * sc_ref[...]).astype(o_ref.dtype)


def norm_matmul_t(x, g, wt, row_scale, tm, tn):
    M, K = x.shape
    N = wt.shape[0]
    return pl.pallas_call(
        _norm_mm_t_kernel,
        out_shape=jax.ShapeDtypeStruct((N, M), BF16),
        grid=(M // tm, N // tn),
        in_specs=[pl.BlockSpec((tm, K), lambda i, j: (i, 0)),
                  pl.BlockSpec((1, K), lambda i, j: (0, 0)),
                  pl.BlockSpec((tn, K), lambda i, j: (j, 0)),
                  pl.BlockSpec((tn, 1), lambda i, j: (j, 0))],
        out_specs=pl.BlockSpec((tn, tm), lambda i, j: (j, i)),
        scratch_shapes=[pltpu.VMEM((tm, K), BF16)],
        compiler_params=_cparams(("parallel", "arbitrary")),
        name="norm_matmul_t",
    )(x, g, wt, row_scale)


def _out_mm_kernel(a_ref, m_ref, c_ref, x_ref, w_ref, o_ref):
    h = jnp.concatenate([a_ref[...], m_ref[...], c_ref[...]], axis=-1).astype(BF16)
    o_ref[...] = x_ref[...] + jnp.dot(h, w_ref[...], preferred_element_type=F32)


def out_proj(a, m, c, x, w, tm, tn):
    M = x.shape[0]
    N = w.shape[1]
    return pl.pallas_call(
        _out_mm_kernel,
        out_shape=jax.ShapeDtypeStruct((M, N), F32),
        grid=(N // tn, M // tm),
        in_specs=[pl.BlockSpec((tm, a.shape[1]), lambda j, i: (i, 0)),
                  pl.BlockSpec((tm, m.shape[1]), lambda j, i: (i, 0)),
                  pl.BlockSpec((tm, c.shape[1]), lambda j, i: (i, 0)),
                  pl.BlockSpec((tm, tn), lambda j, i: (i, j)),
                  pl.BlockSpec((w.shape[0], tn), lambda j, i: (0, j))],
        out_specs=pl.BlockSpec((tm, tn), lambda j, i: (i, j)),
        compiler_params=_cparams(("parallel", "parallel")),
        name="out_proj",
    )(a, m, c, x, w)


def _compress_kernel(blk_ref, pos_ref, w1_ref, w2t_ref, o_ref):
    x = (blk_ref[0, 0] + pos_ref[0]).astype(BF16)
    hid = jnp.dot(x, w1_ref[0], preferred_element_type=F32)
    hid = hid * _sigmoid(hid)
    o_ref[0, 0] = lax.dot_general(w2t_ref[0], hid.astype(BF16), (((1,), (1,)), ((), ())),
                                  preferred_element_type=F32).astype(o_ref.dtype)


def compress(blk, pos, w1, w2t):
    _, G, NCP, LD = blk.shape
    return pl.pallas_call(
        _compress_kernel,
        out_shape=jax.ShapeDtypeStruct((2, G, HEAD_DIM, NCP), BF16),
        grid=(2, G),
        in_specs=[pl.BlockSpec((1, 1, NCP, LD), lambda a, g: (a, g, 0, 0)),
                  pl.BlockSpec((1, 1, LD), lambda a, g: (a, 0, 0)),
                  pl.BlockSpec((1, LD, CMP_HIDDEN), lambda a, g: (a, 0, 0)),
                  pl.BlockSpec((1, HEAD_DIM, CMP_HIDDEN), lambda a, g: (a, 0, 0))],
        out_specs=pl.BlockSpec((1, 1, HEAD_DIM, NCP), lambda a, g: (a, g, 0, 0)),
        compiler_params=_cparams(("parallel", "parallel")),
        name="nsa_compress",
    )(blk, pos, w1, w2t)


def _tdot(a, b):
    return lax.dot_general(a, b, (((0,), (0,)), ((), ())), preferred_element_type=F32)


def _cmp_attn_kernel(q_ref, kct_ref, vct_ref, sl_ref, ov_ref, oc_ref, sel_ref, cnt_ref, imp_ref, *, n_sel, cch):
    qb = pl.program_id(1)
    gp = kct_ref.shape[0]
    q_augs = []
    for j in range(gp):
        q = q_ref[j * HPG * HEAD_DIM:(j + 1) * HPG * HEAD_DIM, :]
        qt = jnp.concatenate([q[h * HEAD_DIM:(h + 1) * HEAD_DIM, :] for h in range(HPG)], axis=1)
        aug = jnp.concatenate([sl_ref[j], jnp.zeros((8, QL), F32)], axis=0).astype(BF16)
        q_augs.append(jnp.concatenate([qt, aug], axis=0))
    ncp = kct_ref.shape[2]
    n_chunks = ncp // cch
    per_qb = Q_BLOCK // CMP_STRIDE
    last_end = CMP_BLOCK - 1
    need = jnp.minimum((qb * per_qb + per_qb - 2) // cch + 1, n_chunks)
    qoff = lax.broadcasted_iota(jnp.int32, (1, QL), 1) & (Q_BLOCK - 1)
    col_ok = (qb * Q_BLOCK + qoff) >= last_end

    for c in range(1, n_chunks + 1):
        @pl.when(need == c)
        def _(c=c):
            R = c * cch
            r8 = lax.broadcasted_iota(jnp.int32, (8, R), 0)
            n8 = lax.broadcasted_iota(jnp.int32, (8, R), 1)
            hi = ((n8 // per_qb - qb) * Q_BLOCK).astype(F32)
            lo = ((n8 % per_qb) * CMP_STRIDE + last_end).astype(F32)
            pos_rows = jnp.where(r8 < 3, hi, jnp.where(r8 < 6, lo, 0.0))
            k_rows = jnp.concatenate([pos_rows, jnp.zeros((8, R), F32)], axis=0).astype(BF16)
            t0 = max(R - 2 * cch, 0)
            n_io = t0 + lax.broadcasted_iota(jnp.int32, (R - t0, QL), 0)
            visible = n_io * CMP_STRIDE + last_end - qb * Q_BLOCK <= qoff
            ov = ov_ref[:, 0:R]
            for j in range(gp):
                s = _tdot(jnp.concatenate([kct_ref[j, :, 0:R], k_rows], axis=0), q_augs[j])
                tail = jnp.where(visible, s[t0:R], NEG)
                s = tail if t0 == 0 else jnp.concatenate([s[0:t0], tail], axis=0)
                m = jnp.max(s, axis=0, keepdims=True)
                p = jnp.exp2(s - m)
                l = jnp.maximum(jnp.sum(p, axis=0, keepdims=True), 1e-30)
                p = p * jnp.where(col_ok, 1.0 / l, 0.0)
                oc_ref[j, 0] = jnp.dot(vct_ref[j, :, 0:R], p.astype(BF16), preferred_element_type=F32)
                ps = p[:, 0:Q_BLOCK]
                for h in range(1, HPG):
                    ps = ps + p[:, h * Q_BLOCK:(h + 1) * Q_BLOCK]
                hi_p = ps.astype(BF16)
                r1 = ps - hi_p.astype(F32)
                mid_p = r1.astype(BF16)
                lo_p = (r1 - mid_p.astype(F32)).astype(BF16)
                imp_ref[j] = (jnp.dot(ov, hi_p, preferred_element_type=F32)
                              + jnp.dot(ov, mid_p, preferred_element_type=F32)
                              + jnp.dot(ov, lo_p, preferred_element_type=F32))

    n_blk = imp_ref.shape[1]
    j_io = lax.broadcasted_iota(jnp.int32, (n_blk, Q_BLOCK), 0)
    tq = qb * Q_BLOCK + lax.broadcasted_iota(jnp.int32, (n_blk, Q_BLOCK), 1)
    cur = tq // SEL_BLOCK
    forced = (j_io == 0) | (j_io == cur) | (j_io == cur - 1)
    for j in range(gp):
        v = jnp.where(forced, -jnp.inf, jnp.where(j_io <= cur, imp_ref[j], -SEL_FORCE))
        sel = jnp.where(forced, 1.0, 0.0)
        for _ in range(n_sel - 3):
            mx = jnp.max(v, axis=0, keepdims=True)
            idx = jnp.min(jnp.where(v == mx, j_io, n_blk), axis=0, keepdims=True)
            pick = j_io == idx
            v = jnp.where(pick, -jnp.inf, v)
            sel = jnp.where(pick, 1.0, sel)
        live = (sel > 0.5) & (j_io <= cur)
        sel_ref[j, 0, 0:n_blk, :] = jnp.where(live, 0.0, NEG)
        sel_ref[j, 0, n_blk:n_blk + 8, :] = jnp.full((8, Q_BLOCK), NEG, F32)
        cnt_ref[j, 0] = lax.dot_general(jnp.ones((8, Q_BLOCK), BF16), jnp.where(live, 1.0, 0.0).astype(BF16),
                                        (((1,), (1,)), ((), ())), preferred_element_type=F32)


def cmp_attention(feat_t, kc_t, vc_t, slope_rows, ov_t, S):
    G = NSA_KV_HEADS
    n_qb = S // Q_BLOCK
    n_blk = S // SEL_BLOCK
    ncp = kc_t.shape[2]
    n_sel = min(SEL_TOPK, n_blk)
    gp = G
    return pl.pallas_call(
        functools.partial(_cmp_attn_kernel, n_sel=n_sel, cch=min(128, ncp)),
        out_shape=(jax.ShapeDtypeStruct((G, n_qb, HEAD_DIM, QL), F32),
                   jax.ShapeDtypeStruct((G, n_qb, n_blk + 8, Q_BLOCK), F32),
                   jax.ShapeDtypeStruct((G, n_qb, 8, n_blk), F32)),
        grid=(G // gp, n_qb),
        in_specs=[pl.BlockSpec((gp * HPG * HEAD_DIM, Q_BLOCK), lambda g, i: (g, i)),
                  pl.BlockSpec((gp, HEAD_DIM, ncp), lambda g, i: (g, 0, 0)),
                  pl.BlockSpec((gp, HEAD_DIM, ncp), lambda g, i: (g, 0, 0)),
                  pl.BlockSpec((gp, 8, QL), lambda g, i: (g, 0, 0)),
                  pl.BlockSpec((n_blk, ncp), lambda g, i: (0, 0))],
        out_specs=(pl.BlockSpec((gp, 1, HEAD_DIM, QL), lambda g, i: (g, i, 0, 0)),
                   pl.BlockSpec((gp, 1, n_blk + 8, Q_BLOCK), lambda g, i: (g, i, 0, 0)),
                   pl.BlockSpec((gp, 1, 8, n_blk), lambda g, i: (g, i, 0, 0))),
        scratch_shapes=[pltpu.VMEM((gp, n_blk, Q_BLOCK), F32)],
        compiler_params=_cparams(("parallel", "parallel")),
        name="nsa_cmp_select",
    )(feat_t, kc_t, vc_t, slope_rows, ov_t)


KT = 128
SUP = 4
AUG = 16


def _sel_win_kernel(ids_ref, nsup_ref, q_ref, ks_ref, vs_ref, kw_ref, vw_ref, selb_ref, oc_ref, sl_ref, gate_ref,
                    gain_ref, o_ref, *, nt, gp):
    qb = pl.program_id(1)
    n_qb = pl.num_programs(1)

    def k_aug(kt, hi_lane, with_blocks):
        n = kt.shape[1]
        r = lax.broadcasted_iota(jnp.int32, (8, n), 0)
        lane = lax.broadcasted_iota(jnp.int32, (8, n), 1)
        lo = (lane & (KT - 1)).astype(F32)
        pos_rows = jnp.where(r < 3, hi_lane, jnp.where(r < 6, lo, 0.0))
        if with_blocks:
            blk_rows = jnp.where((lane // SEL_BLOCK) == r, 1.0, 0.0)
        else:
            blk_rows = jnp.zeros((8, n), F32)
        return jnp.concatenate([kt, jnp.concatenate([pos_rows, blk_rows], axis=0).astype(BF16)], axis=0)

    def v_aug(vt):
        n = vt.shape[1]
        ones = jnp.where(lax.broadcasted_iota(jnp.int32, (AUG, n), 0) == 0, 1.0, 0.0).astype(BF16)
        return jnp.concatenate([vt, ones], axis=0)

    def tdot(a, b):
        return lax.dot_general(a, b, (((0,), (0,)), ((), ())), preferred_element_type=F32)

    row_io = lax.broadcasted_iota(jnp.int32, (KT, QL), 0)
    qoff = lax.broadcasted_iota(jnp.int32, (KT, QL), 1) & (Q_BLOCK - 1)
    lane_s = lax.broadcasted_iota(jnp.int32, (1, SUP * KT), 1)
    nw = WINDOW // KT + 1
    tiles_w = [qb - (nw - 1) + i for i in range(nw)]
    k0w = [pl.multiple_of(jnp.maximum(t, 0) * KT, KT) for t in tiles_w]
    hi_w = ((lax.broadcasted_iota(jnp.int32, (1, nw * KT), 1) // KT - (nw - 1)) * KT).astype(F32)
    k0d = pl.multiple_of(qb * KT, KT)
    zero_hi = jnp.zeros((1, KT), F32)

    def make_group(j):
        g = pl.program_id(0) * gp + j
        rows = slice(j * HEAD_DIM, (j + 1) * HEAD_DIM)
        q = q_ref[j * HPG * HEAD_DIM:(j + 1) * HPG * HEAD_DIM, :]
        qt = jnp.concatenate([q[h * HEAD_DIM:(h + 1) * HEAD_DIM, :] for h in range(HPG)], axis=1)
        slope_rows = sl_ref[j]

        def q_aug(mask):
            aug = jnp.concatenate([slope_rows, mask], axis=0).astype(BF16)
            return jnp.concatenate([qt, aug], axis=0)

        def mask_rows(block_ids):
            r = lax.broadcasted_iota(jnp.int32, (8, Q_BLOCK), 0)
            out = jnp.zeros((8, Q_BLOCK), F32)
            for i, b in enumerate(block_ids):
                row = jnp.broadcast_to(selb_ref[j, 0, pl.ds(b, 1), :], (8, Q_BLOCK))
                out = jnp.where(r == i, row, out)
            return jnp.concatenate([out] * HPG, axis=1)

        ktw = jnp.concatenate([kw_ref[rows, pl.ds(k0, KT)] for k0 in k0w], axis=1)
        vtw = jnp.concatenate([vw_ref[rows, pl.ds(k0, KT)] for k0 in k0w], axis=1)
        s = tdot(k_aug(ktw, hi_w, False), q_aug(jnp.zeros((8, QL), F32)))
        parts = []
        for i in range(nw):
            si = s[i * KT:(i + 1) * KT]
            if i == 0:
                si = jnp.where((row_io > qoff) & (tiles_w[i] >= 0), si, NEG)
            elif i == nw - 1:
                si = jnp.where(row_io <= qoff, si, NEG)
            else:
                si = jnp.where(tiles_w[i] >= 0, si, NEG)
            parts.append(si)
        s = jnp.concatenate(parts, axis=0)
        m_w = jnp.max(s, axis=0, keepdims=True)
        acc_w = jnp.dot(v_aug(vtw), jnp.exp2(s - m_w).astype(BF16), preferred_element_type=F32)

        s = tdot(k_aug(ks_ref[rows, pl.ds(k0d, KT)], zero_hi, True), q_aug(mask_rows([2 * qb, 2 * qb + 1])))
        s = jnp.where(row_io <= qoff, s, NEG)
        m_s = jnp.max(s, axis=0, keepdims=True)
        acc_s = jnp.dot(v_aug(vs_ref[rows, pl.ds(k0d, KT)]), jnp.exp2(s - m_s).astype(BF16),
                        preferred_element_type=F32)

        def update(si, carry):
            m, acc = carry
            base = (g * n_qb + qb) * nt + si * SUP
            tids = [ids_ref[base + i] for i in range(SUP)]
            tcl = [jnp.minimum(t, nt - 1) for t in tids]
            k0s = [pl.multiple_of(t * KT, KT) for t in tcl]
            kt = jnp.concatenate([ks_ref[rows, pl.ds(k0, KT)] for k0 in k0s], axis=1)
            vt = jnp.concatenate([vs_ref[rows, pl.ds(k0, KT)] for k0 in k0s], axis=1)
            hi = (tcl[SUP - 1] - qb) * KT
            for i in range(SUP - 2, -1, -1):
                hi = jnp.where(lane_s < (i + 1) * KT, (tcl[i] - qb) * KT, hi)
            blocks = [2 * t + b for t in tids for b in range(2)]
            s = tdot(k_aug(kt, hi.astype(F32), True), q_aug(mask_rows(blocks)))
            m_new = jnp.maximum(m, jnp.max(s, axis=0, keepdims=True))
            p = jnp.exp2(s - m_new).astype(BF16)
            acc = jnp.exp2(m - m_new) * acc + jnp.dot(v_aug(vt), p, preferred_element_type=F32)
            return m_new, acc

        def finish(carry):
            _, acc = carry
            gts = _sigmoid(gate_ref[j, 0])
            o_s = acc[0:HEAD_DIM] / acc[HEAD_DIM:HEAD_DIM + 1]
            o_w = acc_w[0:HEAD_DIM] / acc_w[HEAD_DIM:HEAD_DIM + 1]
            o = gts[0:1] * oc_ref[j, 0] + gts[1:2] * o_s + gts[2:3] * o_w
            ms = jnp.mean(o * o, axis=0, keepdims=True)
            y = o * lax.rsqrt(ms + NORM_EPS) * gain_ref[j]
            yt = jnp.concatenate([y[:, h * Q_BLOCK:(h + 1) * Q_BLOCK] for h in range(HPG)], axis=0)
            o_ref[:, j * HPG * HEAD_DIM:(j + 1) * HPG * HEAD_DIM] = yt.T

        return (m_s, acc_s), nsup_ref[g * n_qb + qb], update, finish

    groups = [make_group(j) for j in range(gp)]
    n_iter = groups[0][1]
    for grp in groups[1:]:
        n_iter = jnp.maximum(n_iter, grp[1])
    carries = lax.fori_loop(0, n_iter, lambda si, cs: tuple(grp[2](si, c) for grp, c in zip(groups, cs)),
                            tuple(grp[0] for grp in groups))
    for grp, c in zip(groups, carries):
        grp[3](c)


def sel_win_attention(tile_ids, n_sup, feat_t, selb, oc_t, slope_rows, gates_t, gain_t, S):
    G = NSA_KV_HEADS
    gp = 2
    n_qb = S // Q_BLOCK
    n_blk = S // SEL_BLOCK
    nt = S // KT
    kv_rows = gp * HEAD_DIM
    kv_spec = lambda base: pl.BlockSpec((kv_rows, S), lambda g, i, ids, ns: (base // kv_rows + g, 0))
    grid_spec = pltpu.PrefetchScalarGridSpec(
        num_scalar_prefetch=2,
        grid=(G // gp, n_qb),
        in_specs=[pl.BlockSpec((gp * HPG * HEAD_DIM, Q_BLOCK), lambda g, i, ids, ns: (g, i)),
                  kv_spec(FT_KS), kv_spec(FT_VS), kv_spec(FT_KW), kv_spec(FT_VW),
                  pl.BlockSpec((gp, 1, n_blk + 8, Q_BLOCK), lambda g, i, ids, ns: (g, i, 0, 0)),
                  pl.BlockSpec((gp, 1, HEAD_DIM, QL), lambda g, i, ids, ns: (g, i, 0, 0)),
                  pl.BlockSpec((gp, 8, QL), lambda g, i, ids, ns: (g, 0, 0)),
                  pl.BlockSpec((gp, 1, 3, QL), lambda g, i, ids, ns: (g, i, 0, 0)),
                  pl.BlockSpec((gp, HEAD_DIM, QL), lambda g, i, ids, ns: (g, 0, 0))],
        out_specs=pl.BlockSpec((Q_BLOCK, gp * HPG * HEAD_DIM), lambda g, i, ids, ns: (i, g)),
    )
    return pl.pallas_call(
        functools.partial(_sel_win_kernel, nt=nt, gp=gp),
        out_shape=jax.ShapeDtypeStruct((S, NSA_WIDTH), F32),
        grid_spec=grid_spec,
        compiler_params=_cparams(("parallel", "parallel")),
        name="nsa_sel_win",
    )(tile_ids, n_sup, feat_t, feat_t, feat_t, feat_t, feat_t, selb, oc_t, slope_rows, gates_t, gain_t)


def _conv_kernel(*refs, width, glu, post):
    if post == "ln_silu":
        x_ref, halo_ref, w_ref, b_ref, lg_ref, lb_ref, o_ref, u_ref, s_ref = refs
    else:
        x_ref, halo_ref, w_ref, b_ref, o_ref, u_ref, s_ref = refs
    i = pl.program_id(0)
    T = o_ref.shape[0]
    H = halo_ref.shape[0]
    C = o_ref.shape[1]

    def pre(v):
        return v[:, :C] * _sigmoid(v[:, C:]) if glu else v

    u_ref[0:H, :] = jnp.where(i > 0, pre(halo_ref[...]), 0.0)
    u_ref[H:H + T, :] = pre(x_ref[...])
    acc = jnp.broadcast_to(b_ref[...], (T, C))
    base = H - (width - 1)
    phases = {}
    for k in range(width):
        phases.setdefault((base + k) % 8, []).append((k, (base + k) // 8))
    for ph, taps in phases.items():
        n_rows = 8 * max(a for _, a in taps) + T
        if ph == 0 or len(taps) == 1:
            src, off = u_ref, ph
        else:
            s_ref[0:n_rows, :] = u_ref[ph:ph + n_rows, :]
            src, off = s_ref, 0
        for k, a in taps:
            acc = acc + w_ref[k:k + 1, :] * src[off + 8 * a:off + 8 * a + T, :]
    if post == "ln_silu":
        mu = jnp.mean(acc, axis=-1, keepdims=True)
        xc = acc - mu
        var = jnp.mean(xc * xc, axis=-1, keepdims=True)
        acc = xc * lax.rsqrt(var + LN_EPS) * lg_ref[...] + lb_ref[...]
    o_ref[...] = acc * _sigmoid(acc)


def causal_conv(x, col_block, w, b, ln=None, *, glu, T):
    S = x.shape[0]
    width, C = w.shape
    cin = 2 * C if glu else C
    H = -(-(width - 1) // 8) * 8
    post = "ln_silu" if ln is not None else "silu"
    in_specs = [pl.BlockSpec((T, cin), lambda i: (i, col_block)),
                pl.BlockSpec((H, cin), lambda i: (jnp.maximum(i * (T // H) - 1, 0), col_block)),
                pl.BlockSpec((width, C), lambda i: (0, 0)),
                pl.BlockSpec((1, C), lambda i: (0, 0))]
    args = [x, x, w, b]
    if ln is not None:
        in_specs += [pl.BlockSpec((1, C), lambda i: (0, 0))] * 2
        args += list(ln)
    return pl.pallas_call(
        functools.partial(_conv_kernel, width=width, glu=glu, post=post),
        out_shape=jax.ShapeDtypeStruct((S, C), F32),
        grid=(S // T,),
        in_specs=in_specs,
        out_specs=pl.BlockSpec((T, C), lambda i: (i, 0)),
        scratch_shapes=[pltpu.VMEM((H + T, C), F32), pltpu.VMEM((H + T, C), F32)],
        compiler_params=_cparams(("parallel",)),
        name="causal_conv_glu" if glu else "causal_conv",
    )(*args)


def _log_sigmoid(x):
    return jnp.minimum(x, 0.0) - jnp.log(1.0 + jnp.exp(-jnp.abs(x)))


def _split3(x):
    hi = x.astype(BF16).astype(F32)
    mid = (x - hi).astype(BF16).astype(F32)
    lo = (x - hi - mid).astype(BF16).astype(F32)
    return hi, mid, lo


def _lane_scan(x, op, fill):
    n = x.shape[1]
    lane = lax.broadcasted_iota(jnp.int32, x.shape, 1)
    sh = 1
    while sh < n:
        x = op(x, jnp.where(lane >= sh, pltpu.roll(x, sh, axis=1), fill))
        sh *= 2
    return x


def _mlstm_kernel(qk_ref, v_ref, o_ref, gt_ref, bias_ref, gain_ref, out_ref, c_ref, m_ref):
    L = qk_ref.shape[0]
    DH = MLSTM_HEAD_DIM
    W = MLSTM_WIDTH
    H = MLSTM_HEADS

    @pl.when(pl.program_id(0) == 0)
    def _():
        c_ref[...] = jnp.zeros_like(c_ref)
        m_ref[...] = jnp.zeros_like(m_ref)

    g8 = gt_ref[...] + bias_ref[...]
    b8 = pltpu.roll(_lane_scan(_log_sigmoid(g8), jnp.add, 0.0), H, axis=0)
    e8 = g8 - b8
    m0 = m_ref[:, 0:1]
    u8 = jnp.maximum(_lane_scan(e8, jnp.maximum, -jnp.inf), m0)
    b_end = jnp.min(b8, axis=1, keepdims=True)
    a8 = b_end - b8 + g8
    m_new = jnp.maximum(b_end + m0, jnp.max(a8, axis=1, keepdims=True))
    sp8 = jnp.exp(b_end + m0 - m_new)
    m_ref[...] = jnp.broadcast_to(m_new, m_ref.shape)

    ones_rows = jnp.ones((3, L), F32)
    ones_rep = jnp.ones((8, DH), BF16)
    ones_sq = jnp.ones((DH, DH), BF16)
    causal = lax.broadcasted_iota(jnp.int32, (L, L), 1) <= lax.broadcasted_iota(jnp.int32, (L, L), 0)

    def rows8(*parts):
        n = sum(p.shape[0] for p in parts)
        return jnp.concatenate(list(parts) + [jnp.zeros((8 - n, L), F32)], axis=0).astype(BF16)

    def rep(row):
        return _tdot(rows8(*_split3(row)), ones_rep)

    for h in range(H):
        q = qk_ref[:, h * DH:(h + 1) * DH].astype(BF16)
        k = qk_ref[:, W + h * DH:W + (h + 1) * DH] * (DH ** -0.5)
        v = v_ref[:, h * DH:(h + 1) * DH].astype(BF16)
        vaug = jnp.concatenate([v, jnp.ones((L, DH), BF16)], axis=1)
        c0 = c_ref[h]
        e_row, u_row = e8[h:h + 1], u8[h:h + 1]
        x = _tdot(rows8(*_split3(-u_row), ones_rows), rows8(ones_rows, *_split3(e_row)))
        decay = jnp.where(causal, jnp.exp(x), 0.0)
        qk = lax.dot_general(q, k.astype(BF16), (((1,), (1,)), ((), ())), preferred_element_type=F32)
        w_inter = jnp.exp(rep(m0[h:h + 1] - u_row))
        r = (jnp.dot((qk * decay).astype(BF16), vaug, preferred_element_type=F32)
             + jnp.concatenate([w_inter, w_inter], axis=1)
             * jnp.dot(q, c0.astype(BF16), preferred_element_type=F32))
        m_t = rep(b8[h:h + 1] + u_row)
        hh = r[:, :DH] / jnp.maximum(jnp.abs(r[:, DH:]), jnp.exp(-m_t))

        kw = (k * jnp.exp(rep(a8[h:h + 1] - m_new[h:h + 1]))).astype(BF16)
        c_ref[h] = sp8[h:h + 1] * c0 + _tdot(kw, vaug)

        y = _sigmoid(o_ref[:, h * DH:(h + 1) * DH]) * hh
        y2 = y * y
        y2_hi = y2.astype(BF16)
        y2_lo = (y2 - y2_hi.astype(F32)).astype(BF16)
        ms = (jnp.dot(y2_hi, ones_sq, preferred_element_type=F32)
              + jnp.dot(y2_lo, ones_sq, preferred_element_type=F32)) * (1.0 / DH)
        out_ref[:, h * DH:(h + 1) * DH] = y * lax.rsqrt(ms + NORM_EPS) * gain_ref[:, h * DH:(h + 1) * DH]


def mlstm(qk, rm, gates_t, bias_col, gain, S):
    L = min(MLSTM_CHUNK, S)
    W = MLSTM_WIDTH
    return pl.pallas_call(
        _mlstm_kernel,
        out_shape=jax.ShapeDtypeStruct((S, W), F32),
        grid=(S // L,),
        in_specs=[pl.BlockSpec((L, 2 * W), lambda c: (c, 0)),
                  pl.BlockSpec((L, W), lambda c: (c, RM_V // W)),
                  pl.BlockSpec((L, W), lambda c: (c, RM_O // W)),
                  pl.BlockSpec((2 * MLSTM_HEADS, L), lambda c: (0, c)),
                  pl.BlockSpec((2 * MLSTM_HEADS, 1), lambda c: (0, 0)),
                  pl.BlockSpec((1, W), lambda c: (0, 0))],
        out_specs=pl.BlockSpec((L, W), lambda c: (c, 0)),
        scratch_shapes=[pltpu.VMEM((MLSTM_HEADS, MLSTM_HEAD_DIM, 2 * MLSTM_HEAD_DIM), F32),
                        pltpu.VMEM((8, 128), F32)],
        compiler_params=_cparams(("arbitrary",)),
        name="mlstm",
    )(qk, rm, rm, gates_t, bias_col, gain)


def _router_kernel(x_ref, g_ref, wr_ref, br_ref, xn_ref, eid_ref, gate_ref):
    xn = _rms_rows(x_ref[...], g_ref[...])
    xn_ref[:, 0, :] = xn
    x_hi = xn.astype(BF16)
    x_lo = (xn - x_hi.astype(F32)).astype(BF16)
    w_hi, w_lo = wr_ref[0], wr_ref[1]
    logits = (jnp.dot(x_hi, w_hi, preferred_element_type=F32)
              + (jnp.dot(x_hi, w_lo, preferred_element_type=F32)
                 + jnp.dot(x_lo, w_hi, preferred_element_type=F32)))
    logits = logits.T[0:br_ref.shape[0], :] + br_ref[...]
    tm = logits.shape[1]
    lg = logits[N_EXPERTS:N_EXPERTS + N_GROUPS, :]
    eg = jnp.exp(lg - jnp.max(lg, axis=0, keepdims=True))
    pg = eg / jnp.sum(eg, axis=0, keepdims=True)
    pg_top = jnp.max(pg, axis=0, keepdims=True)
    g_io = lax.broadcasted_iota(jnp.int32, (N_GROUPS, tm), 0)
    grp = jnp.min(jnp.where(pg == pg_top, g_io, N_GROUPS), axis=0, keepdims=True)
    le = logits[0:EXPERTS_PER_GROUP, :]
    for g in range(1, N_GROUPS):
        le = jnp.where(grp == g, logits[g * EXPERTS_PER_GROUP:(g + 1) * EXPERTS_PER_GROUP, :], le)
    ee = jnp.exp(le - jnp.max(le, axis=0, keepdims=True))
    pe = ee / jnp.sum(ee, axis=0, keepdims=True)
    e_io = lax.broadcasted_iota(jnp.int32, (EXPERTS_PER_GROUP, tm), 0)
    p1 = jnp.max(pe, axis=0, keepdims=True)
    i1 = jnp.min(jnp.where(pe == p1, e_io, EXPERTS_PER_GROUP), axis=0, keepdims=True)
    pe2 = jnp.where(e_io == i1, -1.0, pe)
    p2 = jnp.max(pe2, axis=0, keepdims=True)
    i2 = jnp.min(jnp.where(pe2 == p2, e_io, EXPERTS_PER_GROUP), axis=0, keepdims=True)
    tot = p1 + p2
    eid_ref[...] = jnp.concatenate([grp * EXPERTS_PER_GROUP + i1, grp * EXPERTS_PER_GROUP + i2], axis=0)
    gate_ref[...] = jnp.concatenate([pg_top * p1 / tot, pg_top * p2 / tot], axis=0)


def router(x, g, wr_t, br_t, tm):
    S, D = x.shape
    R = br_t.shape[0]
    return pl.pallas_call(
        _router_kernel,
        out_shape=(jax.ShapeDtypeStruct((S, 1, D), F32),
                   jax.ShapeDtypeStruct((2, S), jnp.int32),
                   jax.ShapeDtypeStruct((2, S), F32)),
        grid=(S // tm,),
        in_specs=[pl.BlockSpec((tm, D), lambda i: (i, 0)),
                  pl.BlockSpec((1, D), lambda i: (0, 0)),
                  pl.BlockSpec((2, D, 128), lambda i: (0, 0, 0)),
                  pl.BlockSpec((R, 1), lambda i: (0, 0))],
        out_specs=(pl.BlockSpec((tm, 1, D), lambda i: (i, 0, 0)),
                   pl.BlockSpec((2, tm), lambda i: (0, i)),
                   pl.BlockSpec((2, tm), lambda i: (0, i))),
        compiler_params=_cparams(("parallel",)),
        name="moe_router",
    )(x, g, wr_t, br_t)


def _row_gather(idx_ref, idx0, stride, src_hbm, dst, sem, n_rows):
    def body(r, carry):
        i = idx_ref[idx0 + r * stride]
        pltpu.make_async_copy(src_hbm.at[i], dst.at[pl.ds(r, 1), :], sem).start()
        return carry

    lax.fori_loop(0, n_rows, body, 0, unroll=8)


def _rows_wait(src_hbm, dst, sem):
    pltpu.make_async_copy(src_hbm.at[pl.ds(0, dst.shape[0]), 0], dst, sem).wait()


def _expert_kernel(be_ref, nu_ref, tok_ref, dst_ref, xn_hbm, wg_ref, wu_ref, wd_ref, y_hbm,
                   xbuf0, xbuf1, obuf, gsem, ssem, wg_s, wu_s, wd_s):
    b = pl.program_id(0)
    n_used = nu_ref[0]
    e = be_ref[b]
    prev = be_ref[jnp.maximum(b - 1, 0)]
    spare0 = y_hbm.shape[0] - MOE_ROWS

    def scatter_wait():
        pltpu.make_async_copy(obuf, y_hbm.at[pl.ds(0, MOE_ROWS), 0], ssem.at[0]).wait()

    xbufs = (xbuf0, xbuf1)

    @pl.when(b == 0)
    def _():
        _row_gather(tok_ref, 0, 1, xn_hbm, xbuf0, gsem.at[0], MOE_ROWS)
        obuf[...] = jnp.zeros_like(obuf)
        pltpu.make_async_copy(obuf, y_hbm.at[pl.ds(spare0, MOE_ROWS), 0], ssem.at[0]).start()

    @pl.when((b == 0) | (e != prev))
    def _():
        wg_s[...] = wg_ref[0, 0].astype(BF16)
        wu_s[...] = wu_ref[0, 0].astype(BF16)
        wd_s[...] = wd_ref[0, 0].astype(BF16)

    half = MOE_ROWS // 2

    def run_block(slot):
        cur, oth = xbufs[slot], xbufs[1 - slot]
        _rows_wait(xn_hbm, cur, gsem.at[slot])
        nxt = jnp.minimum(b + 1, n_used - 1) * MOE_ROWS

        def ffn(r0):
            x = cur[r0:r0 + half, :].astype(BF16)
            hg = jnp.dot(x, wg_s[...], preferred_element_type=F32)
            hu = jnp.dot(x, wu_s[...], preferred_element_type=F32)
            hb = (hg * _sigmoid(hg) * hu).astype(BF16)
            return jnp.dot(hb, wd_s[...], preferred_element_type=F32)

        def gather_next(r0):
            for r in range(r0, r0 + half):
                pltpu.make_async_copy(xn_hbm.at[tok_ref[nxt + r]], oth.at[pl.ds(r, 1), :],
                                      gsem.at[1 - slot]).start()

        def scatter(r0):
            for r in range(r0, r0 + half):
                pltpu.make_async_copy(obuf.at[pl.ds(r, 1), :], y_hbm.at[dst_ref[b * MOE_ROWS + r]],
                                      ssem.at[0]).start()

        gather_next(0)
        y0 = ffn(0)
        scatter_wait()
        obuf[0:half, :] = y0
        scatter(0)
        gather_next(half)
        obuf[half:MOE_ROWS, :] = ffn(half)
        scatter(half)

    for slot in range(2):
        @pl.when((b < n_used) & (b % 2 == slot))
        def _(slot=slot):
            run_block(slot)

    @pl.when(b == n_used - 1)
    def _():
        scatter_wait()
        for slot in range(2):
            @pl.when((b + 1) % 2 == slot)
            def _(slot=slot):
                _rows_wait(xn_hbm, xbufs[slot], gsem.at[slot])


def expert_ffn(blk_e, n_used, row_tok, row_dst, xn, w_gate, w_up, w_down, layer, n_out):
    P = row_tok.shape[0]
    D = xn.shape[2]
    n_blocks = P // MOE_ROWS
    Hd = w_gate.shape[3]
    grid_spec = pltpu.PrefetchScalarGridSpec(
        num_scalar_prefetch=4,
        grid=(n_blocks,),
        in_specs=[pl.BlockSpec(memory_space=pl.ANY),
                  pl.BlockSpec((1, 1, D, Hd), lambda b, be, nu, tk, ds: (layer, be[b], 0, 0)),
                  pl.BlockSpec((1, 1, D, Hd), lambda b, be, nu, tk, ds: (layer, be[b], 0, 0)),
                  pl.BlockSpec((1, 1, Hd, D), lambda b, be, nu, tk, ds: (layer, be[b], 0, 0))],
        out_specs=pl.BlockSpec(memory_space=pl.ANY),
        scratch_shapes=[pltpu.VMEM((MOE_ROWS, D), F32), pltpu.VMEM((MOE_ROWS, D), F32),
                        pltpu.VMEM((MOE_ROWS, D), F32),
                        pltpu.SemaphoreType.DMA((2,)), pltpu.SemaphoreType.DMA((1,)),
                        pltpu.VMEM((D, Hd), BF16), pltpu.VMEM((D, Hd), BF16), pltpu.VMEM((Hd, D), BF16)],
    )
    return pl.pallas_call(
        _expert_kernel,
        out_shape=jax.ShapeDtypeStruct((n_out + MOE_ROWS, 1, D), F32),
        grid_spec=grid_spec,
        compiler_params=_cparams(("arbitrary",)),
        name="moe_experts",
    )(blk_e, n_used, row_tok, row_dst, xn, w_gate, w_up, w_down)


def _combine_kernel(x_ref, gate_ref, y0_ref, y1_ref, o_ref):
    gate = gate_ref[...]
    o_ref[...] = x_ref[...] + gate[:, 0:1] * y0_ref[:, 0, :] + gate[:, 1:2] * y1_ref[:, 0, :]


def moe_combine(x, gate, y, tt):
    S, D = x.shape
    return pl.pallas_call(
        _combine_kernel,
        out_shape=jax.ShapeDtypeStruct((S, D), F32),
        grid=(S // tt,),
        in_specs=[pl.BlockSpec((tt, D), lambda i: (i, 0)),
                  pl.BlockSpec((tt, 2), lambda i: (i, 0)),
                  pl.BlockSpec((tt, 1, D), lambda i: (i, 0, 0)),
                  pl.BlockSpec((tt, 1, D), lambda i: (S // tt + i, 0, 0))],
        out_specs=pl.BlockSpec((tt, D), lambda i: (i, 0)),
        compiler_params=_cparams(("parallel",)),
        name="moe_combine",
    )(x, gate, y, y)


def _final_norm_kernel(x_ref, g_ref, o_ref):
    o_ref[...] = _rms_rows(x_ref[...], g_ref[...])


def final_norm(x, g, tm):
    S, D = x.shape
    return pl.pallas_call(
        _final_norm_kernel,
        out_shape=jax.ShapeDtypeStruct((S, D), F32),
        grid=(S // tm,),
        in_specs=[pl.BlockSpec((tm, D), lambda i: (i, 0)), pl.BlockSpec((1, D), lambda i: (0, 0))],
        out_specs=pl.BlockSpec((tm, D), lambda i: (i, 0)),
        compiler_params=_cparams(("parallel",)),
        name="final_norm",
    )(x, g)


LOG2E = float(np.log2(np.e))


def _alibi_lanes():
    sl = 2.0 ** (-8.0 * np.arange(1, NSA_HEADS + 1) / NSA_HEADS) * LOG2E
    sl = np.repeat(sl.reshape(NSA_KV_HEADS, HPG, 1), Q_BLOCK, axis=2).reshape(NSA_KV_HEADS, 1, QL)
    sl = jnp.asarray(sl, F32)
    s1 = sl.astype(BF16).astype(F32)
    s2 = (sl - s1).astype(BF16).astype(F32)
    s3 = (sl - s1 - s2).astype(BF16).astype(F32)
    zero = jnp.zeros_like(sl)
    return jnp.concatenate([s1, s2, s3, s1, s2, s3, zero, zero], axis=1)


def _active_tiles(cnt, S):
    G, n_qb, n_blk = cnt.shape
    nt = S // KT
    act = (cnt > 0.5).reshape(G, n_qb, nt, KT // SEL_BLOCK).any(-1)
    tile = jnp.arange(nt, dtype=jnp.int32)
    act = act & (tile[None, None, :] < (jnp.arange(n_qb) * (Q_BLOCK // KT))[None, :, None])
    rank = jnp.cumsum(act.astype(jnp.int32), axis=-1) - 1
    n_act = rank[..., -1] + 1
    hit = act[..., None, :] & (rank[..., None, :] == tile[None, None, :, None])
    ids = jnp.sum(jnp.where(hit, tile[None, None, None, :], 0), axis=-1)
    ids = jnp.where(tile[None, None, :] < n_act[..., None], ids, nt).astype(jnp.int32)
    return ids.reshape(-1), ((n_act + SUP - 1) // SUP).astype(jnp.int32).reshape(-1)


def _overlap_t(n_cmp_pad, n_cmp, n_blk):
    cs = np.arange(n_cmp_pad) * CMP_STRIDE
    ss = np.arange(n_blk) * SEL_BLOCK
    ov = np.minimum(cs[None, :] + CMP_BLOCK, ss[:, None] + SEL_BLOCK) - np.maximum(cs[None, :], ss[:, None])
    ov = np.clip(ov, 0, None) / CMP_STRIDE
    ov[:, n_cmp:] = 0.0
    return jnp.asarray(ov, BF16)


def mixer(x, p, S):
    G = NSA_KV_HEADS
    tm = min(1024, S)
    w_in = p["w_in"]
    c = np.cumsum((0, NSA_WIDTH) + (NSA_KV_WIDTH,) * 6 + (3 * NSA_HEADS,) + (MLSTM_WIDTH,) * 4
                  + (MLSTM_HEADS,) * 2 + (2 * CONV_CHANNELS,))
    (q0, kc0, vc0, ks0, vs0, kw0, vw0, gt0, mq0, mk0, mv0, mo0, mi0, mf0, cu0, end) = [int(v) for v in c]
    w_rm = jnp.concatenate(
        [w_in[:, mq0:mv0], w_in[:, cu0:end], w_in[:, mv0:mi0], w_in[:, kc0:ks0], w_in[:, gt0:mq0],
         w_in[:, mi0:cu0], jnp.zeros((D_MODEL, RM_WIDTH - RM_SMALL - 56), F32)], axis=1).astype(BF16)
    w_ft = jnp.concatenate([w_in[:, q0:kc0], w_in[:, ks0:gt0]], axis=1).T.astype(BF16)
    g = p["attn_norm_g"][None, :]
    rm = norm_matmul(x, g, w_rm, tm, 768)
    q_scale = jnp.where(jnp.arange(FT_WIDTH) < FT_KS, HEAD_DIM ** -0.5 * LOG2E, 1.0).astype(F32)[:, None]
    feat_t = norm_matmul_t(x, g, w_ft, q_scale, tm, 512)

    ncp = S // CMP_STRIDE
    n_cmp = (S - CMP_BLOCK) // CMP_STRIDE + 1
    n_blk = S // SEL_BLOCK
    n_qb = S // Q_BLOCK
    kv = jnp.stack([rm[:, RM_KC:RM_KC + NSA_KV_WIDTH], rm[:, RM_VC:RM_VC + NSA_KV_WIDTH]])
    rows = kv.reshape(2, ncp, CMP_STRIDE, G, HEAD_DIM).transpose(0, 3, 1, 2, 4).reshape(2, G, ncp, -1)
    nxt = jnp.concatenate([rows[:, :, 1:], jnp.zeros_like(rows[:, :, :1])], axis=2)
    blk = jnp.concatenate([rows, nxt], axis=-1)
    pos = jnp.stack([p["cmp_pos_k"].reshape(1, -1), p["cmp_pos_v"].reshape(1, -1)])
    w1 = jnp.stack([p["cmp_w1_k"], p["cmp_w1_v"]]).astype(BF16)
    w2t = jnp.stack([p["cmp_w2_k"].T, p["cmp_w2_v"].T]).astype(BF16)
    kvc_t = compress(blk, pos, w1, w2t)
    slope_rows = _alibi_lanes()
    oc_t, selb, cnt = cmp_attention(feat_t, kvc_t[0], kvc_t[1], slope_rows, _overlap_t(ncp, n_cmp, n_blk), S)
    tile_ids, n_sup = _active_tiles(cnt[:, :, 0, :], S)
    small = rm[:, RM_SMALL:RM_SMALL + 128]
    gates_t = small[:, SM_GATES:SM_GATES + 3 * NSA_HEADS].reshape(n_qb, Q_BLOCK, G, HPG, 3)
    gates_t = gates_t.transpose(2, 0, 4, 3, 1).reshape(G, n_qb, 3, QL)
    gain_t = jnp.broadcast_to(p["nsa_out_g"].reshape(G, HPG, HEAD_DIM, 1).transpose(0, 2, 1, 3),
                              (G, HEAD_DIM, HPG, Q_BLOCK)).reshape(G, HEAD_DIM, QL)
    a_out = sel_win_attention(tile_ids, n_sup, feat_t, selb, oc_t, slope_rows, gates_t, gain_t, S)

    tc = min(512, S)
    qk = causal_conv(rm, RM_QK // (2 * MLSTM_WIDTH), p["mlstm_conv_w"], p["mlstm_conv_b"][None, :],
                     glu=False, T=tc)
    gates_m = small[:, SM_I:SM_I + 2 * MLSTM_HEADS].T
    bias = jnp.concatenate([p["mlstm_i_bias"], p["mlstm_f_bias"]])
    hm = mlstm(qk, rm, gates_m, bias[:, None], p["mlstm_out_g"][None, :], S)

    cv = causal_conv(rm, RM_CU // (2 * CONV_CHANNELS), p["conv_w"], p["conv_b"][None, :],
                     (p["conv_ln_g"][None, :], p["conv_ln_b"][None, :]), glu=True, T=tc)

    return out_proj(a_out, hm, cv, x, p["w_out"].astype(BF16), min(512, S), 1024)


def moe(x, p, stacked_w, layer, S):
    tm = min(512, S)
    wr_t = jnp.concatenate([p["router_w_expert"], p["router_w_group"],
                            jnp.zeros((D_MODEL, 128 - N_EXPERTS - N_GROUPS), F32)], axis=1)
    wr_hi = wr_t.astype(BF16)
    wr_t = jnp.stack([wr_hi, (wr_t - wr_hi.astype(F32)).astype(BF16)])
    br_t = jnp.concatenate([p["router_b_expert"].reshape(-1), p["router_b_group"], jnp.zeros((12,), F32)])[:, None]
    xn, eid, gate = router(x, p["ffn_norm_g"][None, :], wr_t, br_t, tm)

    flat_e = eid.T.reshape(-1)
    A = flat_e.shape[0]
    onehot = (flat_e[:, None] == jnp.arange(N_EXPERTS)[None, :]).astype(jnp.int32)
    rank = jnp.take_along_axis(jnp.cumsum(onehot, axis=0), flat_e[:, None], axis=1)[:, 0] - 1
    counts = jnp.sum(onehot, axis=0)
    padded = (counts + MOE_ROWS - 1) // MOE_ROWS * MOE_ROWS
    pends = jnp.cumsum(padded)
    dest = (pends - padded)[flat_e] + rank
    n_blocks = A // MOE_ROWS + N_EXPERTS
    P = n_blocks * MOE_ROWS
    row_a = jnp.full((P,), -1, jnp.int32).at[dest].set(jnp.arange(A, dtype=jnp.int32), unique_indices=True)
    row_tok = jnp.maximum(row_a, 0) // 2
    spare = 2 * S + jnp.arange(P, dtype=jnp.int32) % MOE_ROWS
    row_dst = jnp.where(row_a >= 0, (row_a % 2) * S + row_a // 2, spare)
    n_used = (pends[-1] // MOE_ROWS).astype(jnp.int32)
    blk_start = jnp.minimum(jnp.arange(n_blocks), n_used - 1) * MOE_ROWS
    blk_e = jnp.minimum(jnp.sum(pends[None, :] <= blk_start[:, None], axis=1), N_EXPERTS - 1).astype(jnp.int32)

    y = expert_ffn(blk_e, n_used[None], row_tok, row_dst, xn, stacked_w["expert_w_gate"],
                   stacked_w["expert_w_up"], stacked_w["expert_w_down"], layer, 2 * S)
    return moe_combine(x, gate.T, y, min(512, S))


_LAYER_KEYS = ("attn_norm_g", "w_in", "cmp_pos_k", "cmp_w1_k", "cmp_w2_k", "cmp_pos_v", "cmp_w1_v", "cmp_w2_v",
               "nsa_out_g", "mlstm_conv_w", "mlstm_conv_b", "mlstm_i_bias", "mlstm_f_bias", "mlstm_out_g",
               "conv_w", "conv_b", "conv_ln_g", "conv_ln_b", "w_out", "ffn_norm_g", "router_w_group",
               "router_b_group", "router_w_expert", "router_b_expert", "expert_w_gate", "expert_w_up",
               "expert_w_down")


def kernel(x, attn_norm_g, w_in, cmp_pos_k, cmp_w1_k, cmp_w2_k, cmp_pos_v, cmp_w1_v, cmp_w2_v, nsa_out_g, mlstm_conv_w, mlstm_conv_b, mlstm_i_bias, mlstm_f_bias, mlstm_out_g, conv_w, conv_b, conv_ln_g, conv_ln_b, w_out, ffn_norm_g, router_w_group, router_b_group, router_w_expert, router_b_expert, expert_w_gate, expert_w_up, expert_w_down, final_norm_g):
    stacked = (attn_norm_g, w_in, cmp_pos_k, cmp_w1_k, cmp_w2_k, cmp_pos_v, cmp_w1_v, cmp_w2_v, nsa_out_g,
               mlstm_conv_w, mlstm_conv_b, mlstm_i_bias, mlstm_f_bias, mlstm_out_g, conv_w, conv_b, conv_ln_g,
               conv_ln_b, w_out, ffn_norm_g, router_w_group, router_b_group, router_w_expert, router_b_expert,
               expert_w_gate, expert_w_up, expert_w_down)
    B, S, D = x.shape
    assert B == 1 and D == D_MODEL and S % 1024 == 0
    h = x.reshape(S, D)
    stacked = dict(zip(_LAYER_KEYS, stacked))
    for l in range(attn_norm_g.shape[0]):
        p = {k: v[l] for k, v in stacked.items() if not k.startswith("expert_w")}
        h = mixer(h, p, S)
        h = moe(h, p, stacked, l, S)
    return final_norm(h, final_norm_g[None, :], min(512, S)).reshape(B, S, D)
```

```python
import functools

import numpy as np
import jax
import jax.numpy as jnp
from jax import lax
from jax.experimental import pallas as pl
from jax.experimental.pallas import tpu as pltpu

F32 = jnp.float32
BF16 = jnp.bfloat16

D_MODEL = 2048
HEAD_DIM = 64
NSA_WIDTH = D_MODEL // 2
NSA_HEADS = NSA_WIDTH // HEAD_DIM
NSA_KV_HEADS = NSA_HEADS // 4
HPG = NSA_HEADS // NSA_KV_HEADS
NSA_KV_WIDTH = NSA_KV_HEADS * HEAD_DIM
CMP_BLOCK = 32
CMP_STRIDE = 16
CMP_HIDDEN = 4 * HEAD_DIM
SEL_BLOCK = 64
SEL_TOPK = 16
WINDOW = 512
Q_BLOCK = 128
SEL_FORCE = 1.0e4
MLSTM_WIDTH = D_MODEL // 4
MLSTM_HEADS = 4
MLSTM_HEAD_DIM = MLSTM_WIDTH // MLSTM_HEADS
MLSTM_CONV = 4
CONV_CHANNELS = D_MODEL // 4
CONV_WIDTH = 31
N_GROUPS = 4
EXPERTS_PER_GROUP = 8
N_EXPERTS = N_GROUPS * EXPERTS_PER_GROUP
EXPERT_HIDDEN = D_MODEL // 4
NORM_EPS = 1e-6
LN_EPS = 1e-5

NEG = -1.0e30
QL = HPG * Q_BLOCK
MLSTM_CHUNK = 512
MOE_ROWS = 256
VMEM_LIMIT = 52 * 1024 * 1024

RM_QK, RM_CU, RM_V, RM_O, RM_KC, RM_VC, RM_SMALL = 0, 1024, 2048, 2560, 3072, 3328, 3584
RM_WIDTH = 3840
SM_GATES, SM_I, SM_F = 0, 48, 52
FT_Q, FT_KS, FT_VS, FT_KW, FT_VW = 0, 1024, 1280, 1536, 1792
FT_WIDTH = 2048


def _cparams(sem, vmem=VMEM_LIMIT):
    return pltpu.CompilerParams(dimension_semantics=sem, vmem_limit_bytes=vmem)


def _sigmoid(x):
    return 1.0 / (1.0 + jnp.exp(-x))


def _rms_rows(x, g):
    ms = jnp.mean(x * x, axis=-1, keepdims=True)
    return x * lax.rsqrt(ms + NORM_EPS) * g


def _norm_mm_kernel(x_ref, g_ref, w_ref, o_ref, h_ref):
    @pl.when(pl.program_id(1) == 0)
    def _():
        h_ref[...] = _rms_rows(x_ref[...], g_ref[...]).astype(BF16)

    o_ref[...] = jnp.dot(h_ref[...], w_ref[...], preferred_element_type=F32).astype(o_ref.dtype)


def norm_matmul(x, g, w, tm, tn):
    M, K = x.shape
    N = w.shape[1]
    return pl.pallas_call(
        _norm_mm_kernel,
        out_shape=jax.ShapeDtypeStruct((M, N), F32),
        grid=(M // tm, N // tn),
        in_specs=[pl.BlockSpec((tm, K), lambda i, j: (i, 0)),
                  pl.BlockSpec((1, K), lambda i, j: (0, 0)),
                  pl.BlockSpec((K, tn), lambda i, j: (0, j))],
        out_specs=pl.BlockSpec((tm, tn), lambda i, j: (i, j)),
        scratch_shapes=[pltpu.VMEM((tm, K), BF16)],
        compiler_params=_cparams(("parallel", "arbitrary")),
        name="norm_matmul",
    )(x, g, w)


def _norm_mm_t_kernel(x_ref, g_ref, wt_ref, sc_ref, o_ref, h_ref):
    @pl.when(pl.program_id(1) == 0)
    def _():
        h_ref[...] = _rms_rows(x_ref[...], g_ref[...]).astype(BF16)

    o = lax.dot_general(wt_ref[...], h_ref[...], (((1,), (1,)), ((), ())), preferred_element_type=F32)
    o_ref[...] = (o * sc_ref[...]).astype(o_ref.dtype)


def norm_matmul_t(x, g, wt, row_scale, tm, tn):
    M, K = x.shape
    N = wt.shape[0]
    return pl.pallas_call(
        _norm_mm_t_kernel,
        out_shape=jax.ShapeDtypeStruct((N, M), BF16),
        grid=(M // tm, N // tn),
        in_specs=[pl.BlockSpec((tm, K), lambda i, j: (i, 0)),
                  pl.BlockSpec((1, K), lambda i, j: (0, 0)),
                  pl.BlockSpec((tn, K), lambda i, j: (j, 0)),
                  pl.BlockSpec((tn, 1), lambda i, j: (j, 0))],
        out_specs=pl.BlockSpec((tn, tm), lambda i, j: (j, i)),
        scratch_shapes=[pltpu.VMEM((tm, K), BF16)],
        compiler_params=_cparams(("parallel", "arbitrary")),
        name="norm_matmul_t",
    )(x, g, wt, row_scale)


def _out_mm_kernel(a_ref, m_ref, c_ref, x_ref, w_ref, o_ref):
    h = jnp.concatenate([a_ref[...], m_ref[...], c_ref[...]], axis=-1).astype(BF16)
    o_ref[...] = x_ref[...] + jnp.dot(h, w_ref[...], preferred_element_type=F32)


def out_proj(a, m, c, x, w, tm, tn):
    M = x.shape[0]
    N = w.shape[1]
    return pl.pallas_call(
        _out_mm_kernel,
        out_shape=jax.ShapeDtypeStruct((M, N), F32),
        grid=(N // tn, M // tm),
        in_specs=[pl.BlockSpec((tm, a.shape[1]), lambda j, i: (i, 0)),
                  pl.BlockSpec((tm, m.shape[1]), lambda j, i: (i, 0)),
                  pl.BlockSpec((tm, c.shape[1]), lambda j, i: (i, 0)),
                  pl.BlockSpec((tm, tn), lambda j, i: (i, j)),
                  pl.BlockSpec((w.shape[0], tn), lambda j, i: (0, j))],
        out_specs=pl.BlockSpec((tm, tn), lambda j, i: (i, j)),
        compiler_params=_cparams(("parallel", "parallel")),
        name="out_proj",
    )(a, m, c, x, w)


def _compress_kernel(blk_ref, pos_ref, w1_ref, w2t_ref, o_ref):
    x = (blk_ref[0, 0] + pos_ref[0]).astype(BF16)
    hid = jnp.dot(x, w1_ref[0], preferred_element_type=F32)
    hid = hid * _sigmoid(hid)
    o_ref[0, 0] = lax.dot_general(w2t_ref[0], hid.astype(BF16), (((1,), (1,)), ((), ())),
                                  preferred_element_type=F32).astype(o_ref.dtype)


def compress(blk, pos, w1, w2t):
    _, G, NCP, LD = blk.shape
    return pl.pallas_call(
        _compress_kernel,
        out_shape=jax.ShapeDtypeStruct((2, G, HEAD_DIM, NCP), BF16),
        grid=(2, G),
        in_specs=[pl.BlockSpec((1, 1, NCP, LD), lambda a, g: (a, g, 0, 0)),
                  pl.BlockSpec((1, 1, LD), lambda a, g: (a, 0, 0)),
                  pl.BlockSpec((1, LD, CMP_HIDDEN), lambda a, g: (a, 0, 0)),
                  pl.BlockSpec((1, HEAD_DIM, CMP_HIDDEN), lambda a, g: (a, 0, 0))],
        out_specs=pl.BlockSpec((1, 1, HEAD_DIM, NCP), lambda a, g: (a, g, 0, 0)),
        compiler_params=_cparams(("parallel", "parallel")),
        name="nsa_compress",
    )(blk, pos, w1, w2t)


def _tdot(a, b):
    return lax.dot_general(a, b, (((0,), (0,)), ((), ())), preferred_element_type=F32)


def _cmp_attn_kernel(q_ref, kct_ref, vct_ref, sl_ref, ov_ref, oc_ref, sel_ref, cnt_ref, imp_ref, *, n_sel, cch):
    qb = pl.program_id(1)
    gp = kct_ref.shape[0]
    q_augs = []
    for j in range(gp):
        q = q_ref[j * HPG * HEAD_DIM:(j + 1) * HPG * HEAD_DIM, :]
        qt = jnp.concatenate([q[h * HEAD_DIM:(h + 1) * HEAD_DIM, :] for h in range(HPG)], axis=1)
        aug = jnp.concatenate([sl_ref[j], jnp.zeros((8, QL), F32)], axis=0).astype(BF16)
        q_augs.append(jnp.concatenate([qt, aug], axis=0))
    ncp = kct_ref.shape[2]
    n_chunks = ncp // cch
    per_qb = Q_BLOCK // CMP_STRIDE
    last_end = CMP_BLOCK - 1
    need = jnp.minimum((qb * per_qb + per_qb - 2) // cch + 1, n_chunks)
    qoff = lax.broadcasted_iota(jnp.int32, (1, QL), 1) & (Q_BLOCK - 1)
    col_ok = (qb * Q_BLOCK + qoff) >= last_end

    for c in range(1, n_chunks + 1):
        @pl.when(need == c)
        def _(c=c):
            R = c * cch
            r8 = lax.broadcasted_iota(jnp.int32, (8, R), 0)
            n8 = lax.broadcasted_iota(jnp.int32, (8, R), 1)
            hi = ((n8 // per_qb - qb) * Q_BLOCK).astype(F32)
            lo = ((n8 % per_qb) * CMP_STRIDE + last_end).astype(F32)
            pos_rows = jnp.where(r8 < 3, hi, jnp.where(r8 < 6, lo, 0.0))
            k_rows = jnp.concatenate([pos_rows, jnp.zeros((8, R), F32)], axis=0).astype(BF16)
            t0 = max(R - 2 * cch, 0)
            n_io = t0 + lax.broadcasted_iota(jnp.int32, (R - t0, QL), 0)
            visible = n_io * CMP_STRIDE + last_end - qb * Q_BLOCK <= qoff
            ov = ov_ref[:, 0:R]
            for j in range(gp):
                s = _tdot(jnp.concatenate([kct_ref[j, :, 0:R], k_rows], axis=0), q_augs[j])
                tail = jnp.where(visible, s[t0:R], NEG)
                s = tail if t0 == 0 else jnp.concatenate([s[0:t0], tail], axis=0)
                m = jnp.max(s, axis=0, keepdims=True)
                p = jnp.exp2(s - m)
                l = jnp.maximum(jnp.sum(p, axis=0, keepdims=True), 1e-30)
                p = p * jnp.where(col_ok, 1.0 / l, 0.0)
                oc_ref[j, 0] = jnp.dot(vct_ref[j, :, 0:R], p.astype(BF16), preferred_element_type=F32)
                ps = p[:, 0:Q_BLOCK]
                for h in range(1, HPG):
                    ps = ps + p[:, h * Q_BLOCK:(h + 1) * Q_BLOCK]
                hi_p = ps.astype(BF16)
                r1 = ps - hi_p.astype(F32)
                mid_p = r1.astype(BF16)
                lo_p = (r1 - mid_p.astype(F32)).astype(BF16)
                imp_ref[j] = (jnp.dot(ov, hi_p, preferred_element_type=F32)
                              + jnp.dot(ov, mid_p, preferred_element_type=F32)
                              + jnp.dot(ov, lo_p, preferred_element_type=F32))

    n_blk = imp_ref.shape[1]
    j_io = lax.broadcasted_iota(jnp.int32, (n_blk, Q_BLOCK), 0)
    tq = qb * Q_BLOCK + lax.broadcasted_iota(jnp.int32, (n_blk, Q_BLOCK), 1)
    cur = tq // SEL_BLOCK
    forced = (j_io == 0) | (j_io == cur) | (j_io == cur - 1)
    for j in range(gp):
        v = jnp.where(forced, -jnp.inf, jnp.where(j_io <= cur, imp_ref[j], -SEL_FORCE))
        sel = jnp.where(forced, 1.0, 0.0)
        for _ in range(n_sel - 3):
            mx = jnp.max(v, axis=0, keepdims=True)
            idx = jnp.min(jnp.where(v == mx, j_io, n_blk), axis=0, keepdims=True)
            pick = j_io == idx
            v = jnp.where(pick, -jnp.inf, v)
            sel = jnp.where(pick, 1.0, sel)
        live = (sel > 0.5) & (j_io <= cur)
        sel_ref[j, 0, 0:n_blk, :] = jnp.where(live, 0.0, NEG)
        sel_ref[j, 0, n_blk:n_blk + 8, :] = jnp.full((8, Q_BLOCK), NEG, F32)
        cnt_ref[j, 0] = lax.dot_general(jnp.ones((8, Q_BLOCK), BF16), jnp.where(live, 1.0, 0.0).astype(BF16),
                                        (((1,), (1,)), ((), ())), preferred_element_type=F32)


def cmp_attention(feat_t, kc_t, vc_t, slope_rows, ov_t, S):
    G = NSA_KV_HEADS
    n_qb = S // Q_BLOCK
    n_blk = S // SEL_BLOCK
    ncp = kc_t.shape[2]
    n_sel = min(SEL_TOPK, n_blk)
    gp = G
    return pl.pallas_call(
        functools.partial(_cmp_attn_kernel, n_sel=n_sel, cch=min(128, ncp)),
        out_shape=(jax.ShapeDtypeStruct((G, n_qb, HEAD_DIM, QL), F32),
                   jax.ShapeDtypeStruct((G, n_qb, n_blk + 8, Q_BLOCK), F32),
                   jax.ShapeDtypeStruct((G, n_qb, 8, n_blk), F32)),
        grid=(G // gp, n_qb),
        in_specs=[pl.BlockSpec((gp * HPG * HEAD_DIM, Q_BLOCK), lambda g, i: (g, i)),
                  pl.BlockSpec((gp, HEAD_DIM, ncp), lambda g, i: (g, 0, 0)),
                  pl.BlockSpec((gp, HEAD_DIM, ncp), lambda g, i: (g, 0, 0)),
                  pl.BlockSpec((gp, 8, QL), lambda g, i: (g, 0, 0)),
                  pl.BlockSpec((n_blk, ncp), lambda g, i: (0, 0))],
        out_specs=(pl.BlockSpec((gp, 1, HEAD_DIM, QL), lambda g, i: (g, i, 0, 0)),
                   pl.BlockSpec((gp, 1, n_blk + 8, Q_BLOCK), lambda g, i: (g, i, 0, 0)),
                   pl.BlockSpec((gp, 1, 8, n_blk), lambda g, i: (g, i, 0, 0))),
        scratch_shapes=[pltpu.VMEM((gp, n_blk, Q_BLOCK), F32)],
        compiler_params=_cparams(("parallel", "parallel")),
        name="nsa_cmp_select",
    )(feat_t, kc_t, vc_t, slope_rows, ov_t)


KT = 128
SUP = 4
AUG = 16


def _sel_win_kernel(ids_ref, nsup_ref, q_ref, ks_ref, vs_ref, kw_ref, vw_ref, selb_ref, oc_ref, sl_ref, gate_ref,
                    gain_ref, o_ref, *, nt, gp):
    qb = pl.program_id(1)
    n_qb = pl.num_programs(1)

    def k_aug(kt, hi_lane, with_blocks):
        n = kt.shape[1]
        r = lax.broadcasted_iota(jnp.int32, (8, n), 0)
        lane = lax.broadcasted_iota(jnp.int32, (8, n), 1)
        lo = (lane & (KT - 1)).astype(F32)
        pos_rows = jnp.where(r < 3, hi_lane, jnp.where(r < 6, lo, 0.0))
        if with_blocks:
            blk_rows = jnp.where((lane // SEL_BLOCK) == r, 1.0, 0.0)
        else:
            blk_rows = jnp.zeros((8, n), F32)
        return jnp.concatenate([kt, jnp.concatenate([pos_rows, blk_rows], axis=0).astype(BF16)], axis=0)

    def v_aug(vt):
        n = vt.shape[1]
        ones = jnp.where(lax.broadcasted_iota(jnp.int32, (AUG, n), 0) == 0, 1.0, 0.0).astype(BF16)
        return jnp.concatenate([vt, ones], axis=0)

    def tdot(a, b):
        return lax.dot_general(a, b, (((0,), (0,)), ((), ())), preferred_element_type=F32)

    row_io = lax.broadcasted_iota(jnp.int32, (KT, QL), 0)
    qoff = lax.broadcasted_iota(jnp.int32, (KT, QL), 1) & (Q_BLOCK - 1)
    lane_s = lax.broadcasted_iota(jnp.int32, (1, SUP * KT), 1)
    nw = WINDOW // KT + 1
    tiles_w = [qb - (nw - 1) + i for i in range(nw)]
    k0w = [pl.multiple_of(jnp.maximum(t, 0) * KT, KT) for t in tiles_w]
    hi_w = ((lax.broadcasted_iota(jnp.int32, (1, nw * KT), 1) // KT - (nw - 1)) * KT).astype(F32)
    k0d = pl.multiple_of(qb * KT, KT)
    zero_hi = jnp.zeros((1, KT), F32)

    def make_group(j):
        g = pl.program_id(0) * gp + j
        rows = slice(j * HEAD_DIM, (j + 1) * HEAD_DIM)
        q = q_ref[j * HPG * HEAD_DIM:(j + 1) * HPG * HEAD_DIM, :]
        qt = jnp.concatenate([q[h * HEAD_DIM:(h + 1) * HEAD_DIM, :] for h in range(HPG)], axis=1)
        slope_rows = sl_ref[j]

        def q_aug(mask):
            aug = jnp.concatenate([slope_rows, mask], axis=0).astype(BF16)
            return jnp.concatenate([qt, aug], axis=0)

        def mask_rows(block_ids):
            r = lax.broadcasted_iota(jnp.int32, (8, Q_BLOCK), 0)
            out = jnp.zeros((8, Q_BLOCK), F32)
            for i, b in enumerate(block_ids):
                row = jnp.broadcast_to(selb_ref[j, 0, pl.ds(b, 1), :], (8, Q_BLOCK))
                out = jnp.where(r == i, row, out)
            return jnp.concatenate([out] * HPG, axis=1)

        ktw = jnp.concatenate([kw_ref[rows, pl.ds(k0, KT)] for k0 in k0w], axis=1)
        vtw = jnp.concatenate([vw_ref[rows, pl.ds(k0, KT)] for k0 in k0w], axis=1)
        s = tdot(k_aug(ktw, hi_w, False), q_aug(jnp.zeros((8, QL), F32)))
        parts = []
        for i in range(nw):
            si = s[i * KT:(i + 1) * KT]
            if i == 0:
                si = jnp.where((row_io > qoff) & (tiles_w[i] >= 0), si, NEG)
            elif i == nw - 1:
                si = jnp.where(row_io <= qoff, si, NEG)
            else:
                si = jnp.where(tiles_w[i] >= 0, si, NEG)
            parts.append(si)
        s = jnp.concatenate(parts, axis=0)
        m_w = jnp.max(s, axis=0, keepdims=True)
        acc_w = jnp.dot(v_aug(vtw), jnp.exp2(s - m_w).astype(BF16), preferred_element_type=F32)

        s = tdot(k_aug(ks_ref[rows, pl.ds(k0d, KT)], zero_hi, True), q_aug(mask_rows([2 * qb, 2 * qb + 1])))
        s = jnp.where(row_io <= qoff, s, NEG)
        m_s = jnp.max(s, axis=0, keepdims=True)
        acc_s = jnp.dot(v_aug(vs_ref[rows, pl.ds(k0d, KT)]), jnp.exp2(s - m_s).astype(BF16),
                        preferred_element_type=F32)

        def update(si, carry):
            m, acc = carry
            base = (g * n_qb + qb) * nt + si * SUP
            tids = [ids_ref[base + i] for i in range(SUP)]
            tcl = [jnp.minimum(t, nt - 1) for t in tids]
            k0s = [pl.multiple_of(t * KT, KT) for t in tcl]
            kt = jnp.concatenate([ks_ref[rows, pl.ds(k0, KT)] for k0 in k0s], axis=1)
            vt = jnp.concatenate([vs_ref[rows, pl.ds(k0, KT)] for k0 in k0s], axis=1)
            hi = (tcl[SUP - 1] - qb) * KT
            for i in range(SUP - 2, -1, -1):
                hi = jnp.where(lane_s < (i + 1) * KT, (tcl[i] - qb) * KT, hi)
            blocks = [2 * t + b for t in tids for b in range(2)]
            s = tdot(k_aug(kt, hi.astype(F32), True), q_aug(mask_rows(blocks)))
            m_new = jnp.maximum(m, jnp.max(s, axis=0, keepdims=True))
            p = jnp.exp2(s - m_new).astype(BF16)
            acc = jnp.exp2(m - m_new) * acc + jnp.dot(v_aug(vt), p, preferred_element_type=F32)
            return m_new, acc

        def finish(carry):
            _, acc = carry
            gts = _sigmoid(gate_ref[j, 0])
            o_s = acc[0:HEAD_DIM] / acc[HEAD_DIM:HEAD_DIM + 1]
            o_w = acc_w[0:HEAD_DIM] / acc_w[HEAD_DIM:HEAD_DIM + 1]
            o = gts[0:1] * oc_ref[j, 0] + gts[1:2] * o_s + gts[2:3] * o_w
            ms = jnp.mean(o * o, axis=0, keepdims=True)
            y = o * lax.rsqrt(ms + NORM_EPS) * gain_ref[j]
            yt = jnp.concatenate([y[:, h * Q_BLOCK:(h + 1) * Q_BLOCK] for h in range(HPG)], axis=0)
            o_ref[:, j * HPG * HEAD_DIM:(j + 1) * HPG * HEAD_DIM] = yt.T

        return (m_s, acc_s), nsup_ref[g * n_qb + qb], update, finish

    groups = [make_group(j) for j in range(gp)]
    n_iter = groups[0][1]
    for grp in groups[1:]:
        n_iter = jnp.maximum(n_iter, grp[1])
    carries = lax.fori_loop(0, n_iter, lambda si, cs: tuple(grp[2](si, c) for grp, c in zip(groups, cs)),
                            tuple(grp[0] for grp in groups))
    for grp, c in zip(groups, carries):
        grp[3](c)


def sel_win_attention(tile_ids, n_sup, feat_t, selb, oc_t, slope_rows, gates_t, gain_t, S):
    G = NSA_KV_HEADS
    gp = 2
    n_qb = S // Q_BLOCK
    n_blk = S // SEL_BLOCK
    nt = S // KT
    kv_rows = gp * HEAD_DIM
    kv_spec = lambda base: pl.BlockSpec((kv_rows, S), lambda g, i, ids, ns: (base // kv_rows + g, 0))
    grid_spec = pltpu.PrefetchScalarGridSpec(
        num_scalar_prefetch=2,
        grid=(G // gp, n_qb),
        in_specs=[pl.BlockSpec((gp * HPG * HEAD_DIM, Q_BLOCK), lambda g, i, ids, ns: (g, i)),
                  kv_spec(FT_KS), kv_spec(FT_VS), kv_spec(FT_KW), kv_spec(FT_VW),
                  pl.BlockSpec((gp, 1, n_blk + 8, Q_BLOCK), lambda g, i, ids, ns: (g, i, 0, 0)),
                  pl.BlockSpec((gp, 1, HEAD_DIM, QL), lambda g, i, ids, ns: (g, i, 0, 0)),
                  pl.BlockSpec((gp, 8, QL), lambda g, i, ids, ns: (g, 0, 0)),
                  pl.BlockSpec((gp, 1, 3, QL), lambda g, i, ids, ns: (g, i, 0, 0)),
                  pl.BlockSpec((gp, HEAD_DIM, QL), lambda g, i, ids, ns: (g, 0, 0))],
        out_specs=pl.BlockSpec((Q_BLOCK, gp * HPG * HEAD_DIM), lambda g, i, ids, ns: (i, g)),
    )
    return pl.pallas_call(
        functools.partial(_sel_win_kernel, nt=nt, gp=gp),
        out_shape=jax.ShapeDtypeStruct((S, NSA_WIDTH), F32),
        grid_spec=grid_spec,
        compiler_params=_cparams(("parallel", "parallel")),
        name="nsa_sel_win",
    )(tile_ids, n_sup, feat_t, feat_t, feat_t, feat_t, feat_t, selb, oc_t, slope_rows, gates_t, gain_t)


def _conv_kernel(*refs, width, glu, post):
    if post == "ln_silu":
        x_ref, halo_ref, w_ref, b_ref, lg_ref, lb_ref, o_ref, u_ref, s_ref = refs
    else:
        x_ref, halo_ref, w_ref, b_ref, o_ref, u_ref, s_ref = refs
    i = pl.program_id(0)
    T = o_ref.shape[0]
    H = halo_ref.shape[0]
    C = o_ref.shape[1]

    def pre(v):
        return v[:, :C] * _sigmoid(v[:, C:]) if glu else v

    u_ref[0:H, :] = jnp.where(i > 0, pre(halo_ref[...]), 0.0)
    u_ref[H:H + T, :] = pre(x_ref[...])
    acc = jnp.broadcast_to(b_ref[...], (T, C))
    base = H - (width - 1)
    phases = {}
    for k in range(width):
        phases.setdefault((base + k) % 8, []).append((k, (base + k) // 8))
    for ph, taps in phases.items():
        n_rows = 8 * max(a for _, a in taps) + T
        if ph == 0 or len(taps) == 1:
            src, off = u_ref, ph
        else:
            s_ref[0:n_rows, :] = u_ref[ph:ph + n_rows, :]
            src, off = s_ref, 0
        for k, a in taps:
            acc = acc + w_ref[k:k + 1, :] * src[off + 8 * a:off + 8 * a + T, :]
    if post == "ln_silu":
        mu = jnp.mean(acc, axis=-1, keepdims=True)
        xc = acc - mu
        var = jnp.mean(xc * xc, axis=-1, keepdims=True)
        acc = xc * lax.rsqrt(var + LN_EPS) * lg_ref[...] + lb_ref[...]
    o_ref[...] = acc * _sigmoid(acc)


def causal_conv(x, col_block, w, b, ln=None, *, glu, T):
    S = x.shape[0]
    width, C = w.shape
    cin = 2 * C if glu else C
    H = -(-(width - 1) // 8) * 8
    post = "ln_silu" if ln is not None else "silu"
    in_specs = [pl.BlockSpec((T, cin), lambda i: (i, col_block)),
                pl.BlockSpec((H, cin), lambda i: (jnp.maximum(i * (T // H) - 1, 0), col_block)),
                pl.BlockSpec((width, C), lambda i: (0, 0)),
                pl.BlockSpec((1, C), lambda i: (0, 0))]
    args = [x, x, w, b]
    if ln is not None:
        in_specs += [pl.BlockSpec((1, C), lambda i: (0, 0))] * 2
        args += list(ln)
    return pl.pallas_call(
        functools.partial(_conv_kernel, width=width, glu=glu, post=post),
        out_shape=jax.ShapeDtypeStruct((S, C), F32),
        grid=(S // T,),
        in_specs=in_specs,
        out_specs=pl.BlockSpec((T, C), lambda i: (i, 0)),
        scratch_shapes=[pltpu.VMEM((H + T, C), F32), pltpu.VMEM((H + T, C), F32)],
        compiler_params=_cparams(("parallel",)),
        name="causal_conv_glu" if glu else "causal_conv",
    )(*args)


def _log_sigmoid(x):
    return jnp.minimum(x, 0.0) - jnp.log(1.0 + jnp.exp(-jnp.abs(x)))


def _split3(x):
    hi = x.astype(BF16).astype(F32)
    mid = (x - hi).astype(BF16).astype(F32)
    lo = (x - hi - mid).astype(BF16).astype(F32)
    return hi, mid, lo


def _lane_scan(x, op, fill):
    n = x.shape[1]
    lane = lax.broadcasted_iota(jnp.int32, x.shape, 1)
    sh = 1
    while sh < n:
        x = op(x, jnp.where(lane >= sh, pltpu.roll(x, sh, axis=1), fill))
        sh *= 2
    return x


def _mlstm_kernel(qk_ref, v_ref, o_ref, gt_ref, bias_ref, gain_ref, out_ref, c_ref, m_ref):
    L = qk_ref.shape[0]
    DH = MLSTM_HEAD_DIM
    W = MLSTM_WIDTH
    H = MLSTM_HEADS

    @pl.when(pl.program_id(0) == 0)
    def _():
        c_ref[...] = jnp.zeros_like(c_ref)
        m_ref[...] = jnp.zeros_like(m_ref)

    g8 = gt_ref[...] + bias_ref[...]
    b8 = pltpu.roll(_lane_scan(_log_sigmoid(g8), jnp.add, 0.0), H, axis=0)
    e8 = g8 - b8
    m0 = m_ref[:, 0:1]
    u8 = jnp.maximum(_lane_scan(e8, jnp.maximum, -jnp.inf), m0)
    b_end = jnp.min(b8, axis=1, keepdims=True)
    a8 = b_end - b8 + g8
    m_new = jnp.maximum(b_end + m0, jnp.max(a8, axis=1, keepdims=True))
    sp8 = jnp.exp(b_end + m0 - m_new)
    m_ref[...] = jnp.broadcast_to(m_new, m_ref.shape)

    ones_rows = jnp.ones((3, L), F32)
    ones_rep = jnp.ones((8, DH), BF16)
    ones_sq = jnp.ones((DH, DH), BF16)
    causal = lax.broadcasted_iota(jnp.int32, (L, L), 1) <= lax.broadcasted_iota(jnp.int32, (L, L), 0)

    def rows8(*parts):
        n = sum(p.shape[0] for p in parts)
        return jnp.concatenate(list(parts) + [jnp.zeros((8 - n, L), F32)], axis=0).astype(BF16)

    def rep(row):
        return _tdot(rows8(*_split3(row)), ones_rep)

    for h in range(H):
        q = qk_ref[:, h * DH:(h + 1) * DH].astype(BF16)
        k = qk_ref[:, W + h * DH:W + (h + 1) * DH] * (DH ** -0.5)
        v = v_ref[:, h * DH:(h + 1) * DH].astype(BF16)
        vaug = jnp.concatenate([v, jnp.ones((L, DH), BF16)], axis=1)
        c0 = c_ref[h]
        e_row, u_row = e8[h:h + 1], u8[h:h + 1]
        x = _tdot(rows8(*_split3(-u_row), ones_rows), rows8(ones_rows, *_split3(e_row)))
        decay = jnp.where(causal, jnp.exp(x), 0.0)
        qk = lax.dot_general(q, k.astype(BF16), (((1,), (1,)), ((), ())), preferred_element_type=F32)
        w_inter = jnp.exp(rep(m0[h:h + 1] - u_row))
        r = (jnp.dot((qk * decay).astype(BF16), vaug, preferred_element_type=F32)
             + jnp.concatenate([w_inter, w_inter], axis=1)
             * jnp.dot(q, c0.astype(BF16), preferred_element_type=F32))
        m_t = rep(b8[h:h + 1] + u_row)
        hh = r[:, :DH] / jnp.maximum(jnp.abs(r[:, DH:]), jnp.exp(-m_t))

        kw = (k * jnp.exp(rep(a8[h:h + 1] - m_new[h:h + 1]))).astype(BF16)
        c_ref[h] = sp8[h:h + 1] * c0 + _tdot(kw, vaug)

        y = _sigmoid(o_ref[:, h * DH:(h + 1) * DH]) * hh
        y2 = y * y
        y2_hi = y2.astype(BF16)
        y2_lo = (y2 - y2_hi.astype(F32)).astype(BF16)
        ms = (jnp.dot(y2_hi, ones_sq, preferred_element_type=F32)
              + jnp.dot(y2_lo, ones_sq, preferred_element_type=F32)) * (1.0 / DH)
        out_ref[:, h * DH:(h + 1) * DH] = y * lax.rsqrt(ms + NORM_EPS) * gain_ref[:, h * DH:(h + 1) * DH]


def mlstm(qk, rm, gates_t, bias_col, gain, S):
    L = min(MLSTM_CHUNK, S)
    W = MLSTM_WIDTH
    return pl.pallas_call(
        _mlstm_kernel,
        out_shape=jax.ShapeDtypeStruct((S, W), F32),
        grid=(S // L,),
        in_specs=[pl.BlockSpec((L, 2 * W), lambda c: (c, 0)),
                  pl.BlockSpec((L, W), lambda c: (c, RM_V // W)),
                  pl.BlockSpec((L, W), lambda c: (c, RM_O // W)),
                  pl.BlockSpec((2 * MLSTM_HEADS, L), lambda c: (0, c)),
                  pl.BlockSpec((2 * MLSTM_HEADS, 1), lambda c: (0, 0)),
                  pl.BlockSpec((1, W), lambda c: (0, 0))],
        out_specs=pl.BlockSpec((L, W), lambda c: (c, 0)),
        scratch_shapes=[pltpu.VMEM((MLSTM_HEADS, MLSTM_HEAD_DIM, 2 * MLSTM_HEAD_DIM), F32),
                        pltpu.VMEM((8, 128), F32)],
        compiler_params=_cparams(("arbitrary",)),
        name="mlstm",
    )(qk, rm, rm, gates_t, bias_col, gain)


def _router_kernel(x_ref, g_ref, wr_ref, br_ref, xn_ref, eid_ref, gate_ref):
    xn = _rms_rows(x_ref[...], g_ref[...])
    xn_ref[:, 0, :] = xn
    x_hi = xn.astype(BF16)
    x_lo = (xn - x_hi.astype(F32)).astype(BF16)
    w_hi, w_lo = wr_ref[0], wr_ref[1]
    logits = (jnp.dot(x_hi, w_hi, preferred_element_type=F32)
              + (jnp.dot(x_hi, w_lo, preferred_element_type=F32)
                 + jnp.dot(x_lo, w_hi, preferred_element_type=F32)))
    logits = logits.T[0:br_ref.shape[0], :] + br_ref[...]
    tm = logits.shape[1]
    lg = logits[N_EXPERTS:N_EXPERTS + N_GROUPS, :]
    eg = jnp.exp(lg - jnp.max(lg, axis=0, keepdims=True))
    pg = eg / jnp.sum(eg, axis=0, keepdims=True)
    pg_top = jnp.max(pg, axis=0, keepdims=True)
    g_io = lax.broadcasted_iota(jnp.int32, (N_GROUPS, tm), 0)
    grp = jnp.min(jnp.where(pg == pg_top, g_io, N_GROUPS), axis=0, keepdims=True)
    le = logits[0:EXPERTS_PER_GROUP, :]
    for g in range(1, N_GROUPS):
        le = jnp.where(grp == g, logits[g * EXPERTS_PER_GROUP:(g + 1) * EXPERTS_PER_GROUP, :], le)
    ee = jnp.exp(le - jnp.max(le, axis=0, keepdims=True))
    pe = ee / jnp.sum(ee, axis=0, keepdims=True)
    e_io = lax.broadcasted_iota(jnp.int32, (EXPERTS_PER_GROUP, tm), 0)
    p1 = jnp.max(pe, axis=0, keepdims=True)
    i1 = jnp.min(jnp.where(pe == p1, e_io, EXPERTS_PER_GROUP), axis=0, keepdims=True)
    pe2 = jnp.where(e_io == i1, -1.0, pe)
    p2 = jnp.max(pe2, axis=0, keepdims=True)
    i2 = jnp.min(jnp.where(pe2 == p2, e_io, EXPERTS_PER_GROUP), axis=0, keepdims=True)
    tot = p1 + p2
    eid_ref[...] = jnp.concatenate([grp * EXPERTS_PER_GROUP + i1, grp * EXPERTS_PER_GROUP + i2], axis=0)
    gate_ref[...] = jnp.concatenate([pg_top * p1 / tot, pg_top * p2 / tot], axis=0)


def router(x, g, wr_t, br_t, tm):
    S, D = x.shape
    R = br_t.shape[0]
    return pl.pallas_call(
        _router_kernel,
        out_shape=(jax.ShapeDtypeStruct((S, 1, D), F32),
                   jax.ShapeDtypeStruct((2, S), jnp.int32),
                   jax.ShapeDtypeStruct((2, S), F32)),
        grid=(S // tm,),
        in_specs=[pl.BlockSpec((tm, D), lambda i: (i, 0)),
                  pl.BlockSpec((1, D), lambda i: (0, 0)),
                  pl.BlockSpec((2, D, 128), lambda i: (0, 0, 0)),
                  pl.BlockSpec((R, 1), lambda i: (0, 0))],
        out_specs=(pl.BlockSpec((tm, 1, D), lambda i: (i, 0, 0)),
                   pl.BlockSpec((2, tm), lambda i: (0, i)),
                   pl.BlockSpec((2, tm), lambda i: (0, i))),
        compiler_params=_cparams(("parallel",)),
        name="moe_router",
    )(x, g, wr_t, br_t)


def _row_gather(idx_ref, idx0, stride, src_hbm, dst, sem, n_rows):
    def body(r, carry):
        i = idx_ref[idx0 + r * stride]
        pltpu.make_async_copy(src_hbm.at[i], dst.at[pl.ds(r, 1), :], sem).start()
        return carry

    lax.fori_loop(0, n_rows, body, 0, unroll=8)


def _rows_wait(src_hbm, dst, sem):
    pltpu.make_async_copy(src_hbm.at[pl.ds(0, dst.shape[0]), 0], dst, sem).wait()


def _expert_kernel(be_ref, nu_ref, tok_ref, dst_ref, xn_hbm, wg_ref, wu_ref, wd_ref, y_hbm,
                   xbuf0, xbuf1, obuf, gsem, ssem, wg_s, wu_s, wd_s):
    b = pl.program_id(0)
    n_used = nu_ref[0]
    e = be_ref[b]
    prev = be_ref[jnp.maximum(b - 1, 0)]
    spare0 = y_hbm.shape[0] - MOE_ROWS

    def scatter_wait():
        pltpu.make_async_copy(obuf, y_hbm.at[pl.ds(0, MOE_ROWS), 0], ssem.at[0]).wait()

    xbufs = (xbuf0, xbuf1)

    @pl.when(b == 0)
    def _():
        _row_gather(tok_ref, 0, 1, xn_hbm, xbuf0, gsem.at[0], MOE_ROWS)
        obuf[...] = jnp.zeros_like(obuf)
        pltpu.make_async_copy(obuf, y_hbm.at[pl.ds(spare0, MOE_ROWS), 0], ssem.at[0]).start()

    @pl.when((b == 0) | (e != prev))
    def _():
        wg_s[...] = wg_ref[0, 0].astype(BF16)
        wu_s[...] = wu_ref[0, 0].astype(BF16)
        wd_s[...] = wd_ref[0, 0].astype(BF16)

    half = MOE_ROWS // 2

    def run_block(slot):
        cur, oth = xbufs[slot], xbufs[1 - slot]
        _rows_wait(xn_hbm, cur, gsem.at[slot])
        nxt = jnp.minimum(b + 1, n_used - 1) * MOE_ROWS

        def ffn(r0):
            x = cur[r0:r0 + half, :].astype(BF16)
            hg = jnp.dot(x, wg_s[...], preferred_element_type=F32)
            hu = jnp.dot(x, wu_s[...], preferred_element_type=F32)
            hb = (hg * _sigmoid(hg) * hu).astype(BF16)
            return jnp.dot(hb, wd_s[...], preferred_element_type=F32)

        def gather_next(r0):
            for r in range(r0, r0 + half):
                pltpu.make_async_copy(xn_hbm.at[tok_ref[nxt + r]], oth.at[pl.ds(r, 1), :],
                                      gsem.at[1 - slot]).start()

        def scatter(r0):
            for r in range(r0, r0 + half):
                pltpu.make_async_copy(obuf.at[pl.ds(r, 1), :], y_hbm.at[dst_ref[b * MOE_ROWS + r]],
                                      ssem.at[0]).start()

        gather_next(0)
        y0 = ffn(0)
        scatter_wait()
        obuf[0:half, :] = y0
        scatter(0)
        gather_next(half)
        obuf[half:MOE_ROWS, :] = ffn(half)
        scatter(half)

    for slot in range(2):
        @pl.when((b < n_used) & (b % 2 == slot))
        def _(slot=slot):
            run_block(slot)

    @pl.when(b == n_used - 1)
    def _():
        scatter_wait()
        for slot in range(2):
            @pl.when((b + 1) % 2 == slot)
            def _(slot=slot):
                _rows_wait(xn_hbm, xbufs[slot], gsem.at[slot])


def expert_ffn(blk_e, n_used, row_tok, row_dst, xn, w_gate, w_up, w_down, layer, n_out):
    P = row_tok.shape[0]
    D = xn.shape[2]
    n_blocks = P // MOE_ROWS
    Hd = w_gate.shape[3]
    grid_spec = pltpu.PrefetchScalarGridSpec(
        num_scalar_prefetch=4,
        grid=(n_blocks,),
        in_specs=[pl.BlockSpec(memory_space=pl.ANY),
                  pl.BlockSpec((1, 1, D, Hd), lambda b, be, nu, tk, ds: (layer, be[b], 0, 0)),
                  pl.BlockSpec((1, 1, D, Hd), lambda b, be, nu, tk, ds: (layer, be[b], 0, 0)),
                  pl.BlockSpec((1, 1, Hd, D), lambda b, be, nu, tk, ds: (layer, be[b], 0, 0))],
        out_specs=pl.BlockSpec(memory_space=pl.ANY),
        scratch_shapes=[pltpu.VMEM((MOE_ROWS, D), F32), pltpu.VMEM((MOE_ROWS, D), F32),
                        pltpu.VMEM((MOE_ROWS, D), F32),
                        pltpu.SemaphoreType.DMA((2,)), pltpu.SemaphoreType.DMA((1,)),
                        pltpu.VMEM((D, Hd), BF16), pltpu.VMEM((D, Hd), BF16), pltpu.VMEM((Hd, D), BF16)],
    )
    return pl.pallas_call(
        _expert_kernel,
        out_shape=jax.ShapeDtypeStruct((n_out + MOE_ROWS, 1, D), F32),
        grid_spec=grid_spec,
        compiler_params=_cparams(("arbitrary",)),
        name="moe_experts",
    )(blk_e, n_used, row_tok, row_dst, xn, w_gate, w_up, w_down)


def _combine_kernel(x_ref, gate_ref, y0_ref, y1_ref, o_ref):
    gate = gate_ref[...]
    o_ref[...] = x_ref[...] + gate[:, 0:1] * y0_ref[:, 0, :] + gate[:, 1:2] * y1_ref[:, 0, :]


def moe_combine(x, gate, y, tt):
    S, D = x.shape
    return pl.pallas_call(
        _combine_kernel,
        out_shape=jax.ShapeDtypeStruct((S, D), F32),
        grid=(S // tt,),
        in_specs=[pl.BlockSpec((tt, D), lambda i: (i, 0)),
                  pl.BlockSpec((tt, 2), lambda i: (i, 0)),
                  pl.BlockSpec((tt, 1, D), lambda i: (i, 0, 0)),
                  pl.BlockSpec((tt, 1, D), lambda i: (S // tt + i, 0, 0))],
        out_specs=pl.BlockSpec((tt, D), lambda i: (i, 0)),
        compiler_params=_cparams(("parallel",)),
        name="moe_combine",
    )(x, gate, y, y)


def _final_norm_kernel(x_ref, g_ref, o_ref):
    o_ref[...] = _rms_rows(x_ref[...], g_ref[...])


def final_norm(x, g, tm):
    S, D = x.shape
    return pl.pallas_call(
        _final_norm_kernel,
        out_shape=jax.ShapeDtypeStruct((S, D), F32),
        grid=(S // tm,),
        in_specs=[pl.BlockSpec((tm, D), lambda i: (i, 0)), pl.BlockSpec((1, D), lambda i: (0, 0))],
        out_specs=pl.BlockSpec((tm, D), lambda i: (i, 0)),
        compiler_params=_cparams(("parallel",)),
        name="final_norm",
    )(x, g)


LOG2E = float(np.log2(np.e))


def _alibi_lanes():
    sl = 2.0 ** (-8.0 * np.arange(1, NSA_HEADS + 1) / NSA_HEADS) * LOG2E
    sl = np.repeat(sl.reshape(NSA_KV_HEADS, HPG, 1), Q_BLOCK, axis=2).reshape(NSA_KV_HEADS, 1, QL)
    sl = jnp.asarray(sl, F32)
    s1 = sl.astype(BF16).astype(F32)
    s2 = (sl - s1).astype(BF16).astype(F32)
    s3 = (sl - s1 - s2).astype(BF16).astype(F32)
    zero = jnp.zeros_like(sl)
    return jnp.concatenate([s1, s2, s3, s1, s2, s3, zero, zero], axis=1)


def _active_tiles(cnt, S):
    G, n_qb, n_blk = cnt.shape
    nt = S // KT
    act = (cnt > 0.5).reshape(G, n_qb, nt, KT // SEL_BLOCK).any(-1)
    tile = jnp.arange(nt, dtype=jnp.int32)
    act = act & (tile[None, None, :] < (jnp.arange(n_qb) * (Q_BLOCK // KT))[None, :, None])
    rank = jnp.cumsum(act.astype(jnp.int32), axis=-1) - 1
    n_act = rank[..., -1] + 1
    hit = act[..., None, :] & (rank[..., None, :] == tile[None, None, :, None])
    ids = jnp.sum(jnp.where(hit, tile[None, None, None, :], 0), axis=-1)
    ids = jnp.where(tile[None, None, :] < n_act[..., None], ids, nt).astype(jnp.int32)
    return ids.reshape(-1), ((n_act + SUP - 1) // SUP).astype(jnp.int32).reshape(-1)


def _overlap_t(n_cmp_pad, n_cmp, n_blk):
    cs = np.arange(n_cmp_pad) * CMP_STRIDE
    ss = np.arange(n_blk) * SEL_BLOCK
    ov = np.minimum(cs[None, :] + CMP_BLOCK, ss[:, None] + SEL_BLOCK) - np.maximum(cs[None, :], ss[:, None])
    ov = np.clip(ov, 0, None) / CMP_STRIDE
    ov[:, n_cmp:] = 0.0
    return jnp.asarray(ov, BF16)


def mixer(x, p, S):
    G = NSA_KV_HEADS
    tm = min(1024, S)
    w_in = p["w_in"]
    c = np.cumsum((0, NSA_WIDTH) + (NSA_KV_WIDTH,) * 6 + (3 * NSA_HEADS,) + (MLSTM_WIDTH,) * 4
                  + (MLSTM_HEADS,) * 2 + (2 * CONV_CHANNELS,))
    (q0, kc0, vc0, ks0, vs0, kw0, vw0, gt0, mq0, mk0, mv0, mo0, mi0, mf0, cu0, end) = [int(v) for v in c]
    w_rm = jnp.concatenate(
        [w_in[:, mq0:mv0], w_in[:, cu0:end], w_in[:, mv0:mi0], w_in[:, kc0:ks0], w_in[:, gt0:mq0],
         w_in[:, mi0:cu0], jnp.zeros((D_MODEL, RM_WIDTH - RM_SMALL - 56), F32)], axis=1).astype(BF16)
    w_ft = jnp.concatenate([w_in[:, q0:kc0], w_in[:, ks0:gt0]], axis=1).T.astype(BF16)
    g = p["attn_norm_g"][None, :]
    rm = norm_matmul(x, g, w_rm, tm, 768)
    q_scale = jnp.where(jnp.arange(FT_WIDTH) < FT_KS, HEAD_DIM ** -0.5 * LOG2E, 1.0).astype(F32)[:, None]
    feat_t = norm_matmul_t(x, g, w_ft, q_scale, tm, 512)

    ncp = S // CMP_STRIDE
    n_cmp = (S - CMP_BLOCK) // CMP_STRIDE + 1
    n_blk = S // SEL_BLOCK
    n_qb = S // Q_BLOCK
    kv = jnp.stack([rm[:, RM_KC:RM_KC + NSA_KV_WIDTH], rm[:, RM_VC:RM_VC + NSA_KV_WIDTH]])
    rows = kv.reshape(2, ncp, CMP_STRIDE, G, HEAD_DIM).transpose(0, 3, 1, 2, 4).reshape(2, G, ncp, -1)
    nxt = jnp.concatenate([rows[:, :, 1:], jnp.zeros_like(rows[:, :, :1])], axis=2)
    blk = jnp.concatenate([rows, nxt], axis=-1)
    pos = jnp.stack([p["cmp_pos_k"].reshape(1, -1), p["cmp_pos_v"].reshape(1, -1)])
    w1 = jnp.stack([p["cmp_w1_k"], p["cmp_w1_v"]]).astype(BF16)
    w2t = jnp.stack([p["cmp_w2_k"].T, p["cmp_w2_v"].T]).astype(BF16)
    kvc_t = compress(blk, pos, w1, w2t)
    slope_rows = _alibi_lanes()
    oc_t, selb, cnt = cmp_attention(feat_t, kvc_t[0], kvc_t[1], slope_rows, _overlap_t(ncp, n_cmp, n_blk), S)
    tile_ids, n_sup = _active_tiles(cnt[:, :, 0, :], S)
    small = rm[:, RM_SMALL:RM_SMALL + 128]
    gates_t = small[:, SM_GATES:SM_GATES + 3 * NSA_HEADS].reshape(n_qb, Q_BLOCK, G, HPG, 3)
    gates_t = gates_t.transpose(2, 0, 4, 3, 1).reshape(G, n_qb, 3, QL)
    gain_t = jnp.broadcast_to(p["nsa_out_g"].reshape(G, HPG, HEAD_DIM, 1).transpose(0, 2, 1, 3),
                              (G, HEAD_DIM, HPG, Q_BLOCK)).reshape(G, HEAD_DIM, QL)
    a_out = sel_win_attention(tile_ids, n_sup, feat_t, selb, oc_t, slope_rows, gates_t, gain_t, S)

    tc = min(512, S)
    qk = causal_conv(rm, RM_QK // (2 * MLSTM_WIDTH), p["mlstm_conv_w"], p["mlstm_conv_b"][None, :],
                     glu=False, T=tc)
    gates_m = small[:, SM_I:SM_I + 2 * MLSTM_HEADS].T
    bias = jnp.concatenate([p["mlstm_i_bias"], p["mlstm_f_bias"]])
    hm = mlstm(qk, rm, gates_m, bias[:, None], p["mlstm_out_g"][None, :], S)

    cv = causal_conv(rm, RM_CU // (2 * CONV_CHANNELS), p["conv_w"], p["conv_b"][None, :],
                     (p["conv_ln_g"][None, :], p["conv_ln_b"][None, :]), glu=True, T=tc)

    return out_proj(a_out, hm, cv, x, p["w_out"].astype(BF16), min(512, S), 1024)


def moe(x, p, stacked_w, layer, S):
    tm = min(512, S)
    wr_t = jnp.concatenate([p["router_w_expert"], p["router_w_group"],
                            jnp.zeros((D_MODEL, 128 - N_EXPERTS - N_GROUPS), F32)], axis=1)
    wr_hi = wr_t.astype(BF16)
    wr_t = jnp.stack([wr_hi, (wr_t - wr_hi.astype(F32)).astype(BF16)])
    br_t = jnp.concatenate([p["router_b_expert"].reshape(-1), p["router_b_group"], jnp.zeros((12,), F32)])[:, None]
    xn, eid, gate = router(x, p["ffn_norm_g"][None, :], wr_t, br_t, tm)

    flat_e = eid.T.reshape(-1)
    A = flat_e.shape[0]
    onehot = (flat_e[:, None] == jnp.arange(N_EXPERTS)[None, :]).astype(jnp.int32)
    rank = jnp.take_along_axis(jnp.cumsum(onehot, axis=0), flat_e[:, None], axis=1)[:, 0] - 1
    counts = jnp.sum(onehot, axis=0)
    padded = (counts + MOE_ROWS - 1) // MOE_ROWS * MOE_ROWS
    pends = jnp.cumsum(padded)
    dest = (pends - padded)[flat_e] + rank
    n_blocks = A // MOE_ROWS + N_EXPERTS
    P = n_blocks * MOE_ROWS
    row_a = jnp.full((P,), -1, jnp.int32).at[dest].set(jnp.arange(A, dtype=jnp.int32), unique_indices=True)
    row_tok = jnp.maximum(row_a, 0) // 2
    spare = 2 * S + jnp.arange(P, dtype=jnp.int32) % MOE_ROWS
    row_dst = jnp.where(row_a >= 0, (row_a % 2) * S + row_a // 2, spare)
    n_used = (pends[-1] // MOE_ROWS).astype(jnp.int32)
    blk_start = jnp.minimum(jnp.arange(n_blocks), n_used - 1) * MOE_ROWS
    blk_e = jnp.minimum(jnp.sum(pends[None, :] <= blk_start[:, None], axis=1), N_EXPERTS - 1).astype(jnp.int32)

    y = expert_ffn(blk_e, n_used[None], row_tok, row_dst, xn, stacked_w["expert_w_gate"],
                   stacked_w["expert_w_up"], stacked_w["expert_w_down"], layer, 2 * S)
    return moe_combine(x, gate.T, y, min(512, S))


_LAYER_KEYS = ("attn_norm_g", "w_in", "cmp_pos_k", "cmp_w1_k", "cmp_w2_k", "cmp_pos_v", "cmp_w1_v", "cmp_w2_v",
               "nsa_out_g", "mlstm_conv_w", "mlstm_conv_b", "mlstm_i_bias", "mlstm_f_bias", "mlstm_out_g",
               "conv_w", "conv_b", "conv_ln_g", "conv_ln_b", "w_out", "ffn_norm_g", "router_w_group",
               "router_b_group", "router_w_expert", "router_b_expert", "expert_w_gate", "expert_w_up",
               "expert_w_down")


def kernel(x, attn_norm_g, w_in, cmp_pos_k, cmp_w1_k, cmp_w2_k, cmp_pos_v, cmp_w1_v, cmp_w2_v, nsa_out_g, mlstm_conv_w, mlstm_conv_b, mlstm_i_bias, mlstm_f_bias, mlstm_out_g, conv_w, conv_b, conv_ln_g, conv_ln_b, w_out, ffn_norm_g, router_w_group, router_b_group, router_w_expert, router_b_expert, expert_w_gate, expert_w_up, expert_w_down, final_norm_g):
    stacked = (attn_norm_g, w_in, cmp_pos_k, cmp_w1_k, cmp_w2_k, cmp_pos_v, cmp_w1_v, cmp_w2_v, nsa_out_g,
               mlstm_conv_w, mlstm_conv_b, mlstm_i_bias, mlstm_f_bias, mlstm_out_g, conv_w, conv_b, conv_ln_g,
               conv_ln_b, w_out, ffn_norm_g, router_w_group, router_b_group, router_w_expert, router_b_expert,
               expert_w_gate, expert_w_up, expert_w_down)
    B, S, D = x.shape
    assert B == 1 and D == D_MODEL and S % 1024 == 0
    h = x.reshape(S, D)
    stacked = dict(zip(_LAYER_KEYS, stacked))
    for l in range(attn_norm_g.shape[0]):
        p = {k: v[l] for k, v in stacked.items() if not k.startswith("expert_w")}
        h = mixer(h, p, S)
        h = moe(h, p, stacked, l, S)
    return final_norm(h, final_norm_g[None, :], min(512, S)).reshape(B, S, D)
```
